```python
import math
import jax
import jax.numpy as jnp
from jax import lax
import numpy as np

D_MODEL = 1024
BATCH = 8
SEQ = 2048
DEPTH = 4

CTX_LEN = 256
GRID_W = 64

N_MOD = 6
EPS = 1e-6

N_HEADS = 8
N_KV_HEADS = 2
HEAD_DIM = 64
Q_REP = N_HEADS // N_KV_HEADS
ATTN_W = N_HEADS * HEAD_DIM
KV_W = N_KV_HEADS * HEAD_DIM
WINDOW = 128
ATTN_BLOCK = 128
ATTN_HALO = WINDOW // ATTN_BLOCK
ATTN_SCALE = HEAD_DIM ** -0.5
ROPE_BASE = 10000.0
ROPE_AXIS_DIM = HEAD_DIM // 2

SSM_W = D_MODEL // 4
SSM_GROUP = 16
SSM_GROUPS = SSM_W // SSM_GROUP
SSM_STATE = 64
DT_MIN = 1e-3
DT_MAX = 1e-1

POOL_W = D_MODEL // 4
POOL_WINDOWS = (2, 4, 8, 16)
POOL_GROUP = POOL_W // len(POOL_WINDOWS)

Q_END = ATTN_W
K_END = Q_END + KV_W
V_END = K_END + KV_W
U_END = V_END + SSM_W
P_END = U_END + POOL_W
IN_W = P_END
MIX_W = ATTN_W + SSM_W + POOL_W

N_EXPERTS = 32
TOP_K = 4
D_FF_EXPERT = D_MODEL
SWIGLU_LIMIT = 7.0
SWIGLU_ALPHA = 1.702
MOE_BLOCK = 128

kernel_name = 'hybrid_diffusion_trunk'


def _rmsnorm(x, w):
    xf = x.astype(jnp.float32)
    y = xf * lax.rsqrt(jnp.mean(xf * xf, axis=-1, keepdims=True) + EPS)
    return (y * w.astype(jnp.float32)).astype(x.dtype)


def _modulate(h, shift, scale):
    return h * (1 + scale) + shift


def _cols(p, lo, a, b):
    return p[..., a - lo:b - lo]


def _axial_rope_tables(length):
    rows = length // GRID_W
    row = jnp.repeat(jnp.arange(rows), GRID_W).astype(jnp.float32)
    col = jnp.tile(jnp.arange(GRID_W), rows).astype(jnp.float32)
    inv = ROPE_BASE ** (-jnp.arange(0, ROPE_AXIS_DIM, 2, dtype=jnp.float32) / ROPE_AXIS_DIM)
    ang_r = row[:, None] * inv
    ang_c = col[:, None] * inv
    ang = jnp.concatenate([ang_r, ang_r, ang_c, ang_c], axis=-1)
    return jnp.cos(ang), jnp.sin(ang)


def _apply_rope(t, cos, sin):
    tf = t.astype(jnp.float32)
    s = tf.reshape(tf.shape[:-1] + (2, 2, ROPE_AXIS_DIM // 2))
    rot = jnp.stack([-s[..., 1, :], s[..., 0, :]], axis=-2).reshape(tf.shape)
    return (tf * cos[None, :, None, :] + rot * sin[None, :, None, :]).astype(t.dtype)


def _window_attention(q, k, v, kc, vc, sink):
    B, L = q.shape[:2]
    Lc = kc.shape[1]
    nb = L // ATTN_BLOCK
    span = (2 * ATTN_HALO + 1) * ATTN_BLOCK
    qb = q.reshape(B, nb, ATTN_BLOCK, N_KV_HEADS, Q_REP, HEAD_DIM)
    pad = ((0, 0), (ATTN_HALO * ATTN_BLOCK, ATTN_HALO * ATTN_BLOCK), (0, 0), (0, 0))
    kpb = jnp.pad(k, pad).reshape(B, nb + 2 * ATTN_HALO, ATTN_BLOCK, N_KV_HEADS, HEAD_DIM)
    vpb = jnp.pad(v, pad).reshape(B, nb + 2 * ATTN_HALO, ATTN_BLOCK, N_KV_HEADS, HEAD_DIM)
    kband = jnp.concatenate([kpb[:, j:j + nb] for j in range(2 * ATTN_HALO + 1)], axis=2)
    vband = jnp.concatenate([vpb[:, j:j + nb] for j in range(2 * ATTN_HALO + 1)], axis=2)
    qpos = jnp.arange(L).reshape(nb, ATTN_BLOCK)
    kpos = (jnp.arange(nb)[:, None] - ATTN_HALO) * ATTN_BLOCK + jnp.arange(span)[None, :]
    valid = ((jnp.abs(qpos[:, :, None] - kpos[:, None, :]) <= WINDOW)
             & (kpos[:, None, :] >= 0) & (kpos[:, None, :] < L))
    s_loc = jnp.einsum('bnqgrd,bnkgd->bngrqk', qb, kband, preferred_element_type=jnp.float32) * ATTN_SCALE
    s_loc = jnp.where(valid[None, :, None, None], s_loc, -jnp.inf)
    s_ctx = jnp.einsum('bnqgrd,bcgd->bngrqc', qb, kc, preferred_element_type=jnp.float32) * ATTN_SCALE
    s_sink = jnp.broadcast_to(sink.astype(jnp.float32).reshape(1, 1, N_KV_HEADS, Q_REP, 1, 1),
                              s_ctx.shape[:-1] + (1,))
    p = jax.nn.softmax(jnp.concatenate([s_loc, s_ctx, s_sink], axis=-1), axis=-1)
    o = (jnp.einsum('bngrqk,bnkgd->bnqgrd', p[..., :span], vband.astype(jnp.float32))
         + jnp.einsum('bngrqc,bcgd->bnqgrd', p[..., span:span + Lc], vc.astype(jnp.float32)))
    return o.reshape(B, L, ATTN_W).astype(q.dtype)


def _ctx_attention(qc, kc, vc, sink):
    B, Lc = qc.shape[:2]
    qg = qc.reshape(B, Lc, N_KV_HEADS, Q_REP, HEAD_DIM)
    s = jnp.einsum('bqgrd,bkgd->bgrqk', qg, kc, preferred_element_type=jnp.float32) * ATTN_SCALE
    s_sink = jnp.broadcast_to(sink.astype(jnp.float32).reshape(1, N_KV_HEADS, Q_REP, 1, 1), s.shape[:-1] + (1,))
    p = jax.nn.softmax(jnp.concatenate([s, s_sink], axis=-1), axis=-1)[..., :Lc]
    o = jnp.einsum('bgrqk,bkgd->bqgrd', p, vc.astype(jnp.float32))
    return o.reshape(B, Lc, ATTN_W).astype(qc.dtype)


def _s5_discretize(a_re, a_im, log_dt, b_re, b_im):
    a_re = a_re.astype(jnp.float32)
    a_im = a_im.astype(jnp.float32)
    dt = jnp.exp(log_dt.astype(jnp.float32))[:, None]
    mag = jnp.exp(a_re * dt)
    ar = mag * jnp.cos(a_im * dt)
    ai = mag * jnp.sin(a_im * dt)
    den = a_re * a_re + a_im * a_im
    qr = ((ar - 1) * a_re + ai * a_im) / den
    qi = (ai * a_re - (ar - 1) * a_im) / den
    br = b_re.astype(jnp.float32)
    bi = b_im.astype(jnp.float32)
    bbr = qr[..., None] * br - qi[..., None] * bi
    bbi = qr[..., None] * bi + qi[..., None] * br
    return ar, ai, bbr, bbi


def _complex_affine_combine(e1, e2):
    a1r, a1i, b1r, b1i = e1
    a2r, a2i, b2r, b2i = e2
    return (a2r * a1r - a2i * a1i,
            a2r * a1i + a2i * a1r,
            a2r * b1r - a2i * b1i + b2r,
            a2r * b1i + a2i * b1r + b2i)


def _s5_scan(u, disc, s0, reverse):
    ar, ai, bbr, bbi = disc
    xr = jnp.einsum('blgp,gnp->blgn', u, bbr)
    xi = jnp.einsum('blgp,gnp->blgn', u, bbi)
    if s0 is not None:
        s0r, s0i = s0
        edge = -1 if reverse else 0
        xr = xr.at[:, edge].add(ar * s0r - ai * s0i)
        xi = xi.at[:, edge].add(ar * s0i + ai * s0r)
    a_r = jnp.broadcast_to(ar, xr.shape)
    a_i = jnp.broadcast_to(ai, xi.shape)
    _, _, sr, si = lax.associative_scan(_complex_affine_combine, (a_r, a_i, xr, xi), reverse=reverse, axis=1)
    return sr, si


def _s5_readout(s, c_re, c_im):
    sr, si = s
    return (jnp.einsum('blgn,gpn->blgp', sr, c_re.astype(jnp.float32))
            - jnp.einsum('blgn,gpn->blgp', si, c_im.astype(jnp.float32)))


def _s5_output(s_fwd, s_bwd, u, c_re, c_im, d, glu_w, glu_b):
    Bu, Lu = u.shape[:2]
    y = _s5_readout(s_fwd, c_re[0], c_im[0]) + _s5_readout(s_bwd, c_re[1], c_im[1])
    y = y.reshape(Bu, Lu, SSM_W) + d.astype(jnp.float32) * u.reshape(Bu, Lu, SSM_W)
    g = jax.nn.gelu(y)
    return g * jax.nn.sigmoid(g @ glu_w.astype(jnp.float32) + glu_b.astype(jnp.float32))


def _multiscale_pool(p, pool_w, pool_scale):
    Bp, Lp = p.shape[:2]
    pf = p.astype(jnp.float32)
    cs = jnp.pad(jnp.cumsum(pf, axis=1), ((0, 0), (1, 0), (0, 0)))
    t = jnp.arange(Lp)
    diffs = []
    for g, w in enumerate(POOL_WINDOWS):
        lo = jnp.clip(t - w // 2, 0, Lp)
        hi = jnp.clip(t + w // 2, 0, Lp)
        sl = slice(g * POOL_GROUP, (g + 1) * POOL_GROUP)
        cg = cs[..., sl]
        mean = (cg[:, hi] - cg[:, lo]) / (hi - lo).astype(jnp.float32)[None, :, None]
        diffs.append(mean - pf[..., sl])
    dlt = jnp.stack(diffs, axis=2)
    y = jnp.einsum('blgc,gcd->blgd', dlt, pool_w.astype(jnp.float32)).reshape(Bp, Lp, POOL_W)
    return (y * pool_scale.astype(jnp.float32)).astype(p.dtype)


def _merge_groups(attn, ssm, pool, out_norm_w):
    return jnp.concatenate([_rmsnorm(attn, out_norm_w[:ATTN_W]),
                            _rmsnorm(ssm, out_norm_w[ATTN_W:]),
                            pool], axis=-1)


def _moe(h, router_w, router_b, w_gu, b_gu, w_down, b_down):
    N, D = h.shape
    logits = (h @ router_w + router_b).astype(jnp.float32)
    top_v, top_i = lax.top_k(logits, TOP_K)
    gates = jax.nn.softmax(top_v, axis=-1)
    flat_e = top_i.reshape(-1)
    order = jnp.argsort(flat_e)
    sorted_e = flat_e[order]
    sorted_tok = order // TOP_K
    counts = jnp.bincount(flat_e, length=N_EXPERTS)
    start = jnp.cumsum(counts) - counts
    padded = (counts + MOE_BLOCK - 1) // MOE_BLOCK * MOE_BLOCK
    pend = jnp.cumsum(padded)
    pstart = pend - padded
    n_assign = N * TOP_K
    dest_sorted = pstart[sorted_e] + jnp.arange(n_assign) - start[sorted_e]
    n_blocks = -(-n_assign // MOE_BLOCK) + N_EXPERTS
    rows = n_blocks * MOE_BLOCK
    row_tok = jnp.full((rows,), N, jnp.int32).at[dest_sorted].set(sorted_tok.astype(jnp.int32))
    block_e = jnp.minimum(jnp.searchsorted(pend, jnp.arange(n_blocks) * MOE_BLOCK, side='right'), N_EXPERTS - 1)
    h_pad = jnp.concatenate([h, jnp.zeros((1, D), h.dtype)], axis=0)
    xb = h_pad[row_tok].reshape(n_blocks, MOE_BLOCK, D)

    def expert_block(args):
        xblk, e = args
        gu = xblk @ w_gu[e] + b_gu[e]
        gate, up = jnp.split(gu, 2, axis=-1)
        gate = jnp.minimum(gate, SWIGLU_LIMIT)
        up = jnp.clip(up, -SWIGLU_LIMIT, SWIGLU_LIMIT)
        act = (up + 1) * (gate * jax.nn.sigmoid(SWIGLU_ALPHA * gate))
        return act @ w_down[e] + b_down[e]

    yb = lax.map(expert_block, (xb, block_e)).reshape(rows, D)
    dest = jnp.zeros((n_assign,), dest_sorted.dtype).at[order].set(dest_sorted)
    y = yb[dest].reshape(N, TOP_K, D)
    return jnp.einsum('nkd,nk->nd', y, gates.astype(y.dtype))


def setup_inputs(seed: int = 0) -> dict:
    key = jax.random.key(seed)
    ks = iter(jax.random.split(key, 40))
    f32 = jnp.float32
    D = D_MODEL

    def nrm(shape, scale):
        return jax.random.normal(next(ks), shape, f32) * scale

    x = nrm((BATCH, SEQ, D), 1.0)
    c = nrm((BATCH, D), 1.0)
    ctx = nrm((BATCH, CTX_LEN, D), 1.0)
    c_ctx = nrm((D,), 1.0)
    w_mod = nrm((DEPTH, D, N_MOD * D), 0.5 * D ** -0.5)
    b_mod = nrm((DEPTH, N_MOD * D), 0.02)
    norm1_w = 1.0 + nrm((DEPTH, D), 0.02)
    norm2_w = 1.0 + nrm((DEPTH, D), 0.02)
    w_in = nrm((DEPTH, D, IN_W), D ** -0.5)
    q_norm_w = 1.0 + nrm((DEPTH, HEAD_DIM), 0.02)
    k_norm_w = 1.0 + nrm((DEPTH, HEAD_DIM), 0.02)
    attn_sink = nrm((DEPTH, N_HEADS), 0.5)
    n_idx = jnp.arange(SSM_STATE, dtype=f32)
    ssm_a_re = -0.5 + nrm((DEPTH, 2, SSM_GROUPS, SSM_STATE), 0.01)
    ssm_a_im = math.pi * n_idx + nrm((DEPTH, 2, SSM_GROUPS, SSM_STATE), 0.01)
    ssm_log_dt = jax.random.uniform(next(ks), (DEPTH, 2, SSM_GROUPS), f32, math.log(DT_MIN), math.log(DT_MAX))
    ssm_b_re = nrm((DEPTH, 2, SSM_GROUPS, SSM_STATE, SSM_GROUP), (2 * SSM_GROUP) ** -0.5)
    ssm_b_im = nrm((DEPTH, 2, SSM_GROUPS, SSM_STATE, SSM_GROUP), (2 * SSM_GROUP) ** -0.5)
    ssm_c_re = nrm((DEPTH, 2, SSM_GROUPS, SSM_GROUP, SSM_STATE), SSM_STATE ** -0.5)
    ssm_c_im = nrm((DEPTH, 2, SSM_GROUPS, SSM_GROUP, SSM_STATE), SSM_STATE ** -0.5)
    ssm_d = nrm((DEPTH, SSM_W), 1.0)
    glu_w = nrm((DEPTH, SSM_W, SSM_W), SSM_W ** -0.5)
    glu_b = nrm((DEPTH, SSM_W), 0.02)
    pool_w = nrm((DEPTH, len(POOL_WINDOWS), POOL_GROUP, POOL_GROUP), POOL_GROUP ** -0.5)
    pool_scale = 1.0 + nrm((DEPTH, POOL_W), 0.02)
    out_norm_w = 1.0 + nrm((DEPTH, ATTN_W + SSM_W), 0.02)
    w_out = nrm((DEPTH, MIX_W, D), MIX_W ** -0.5)
    router_w = nrm((DEPTH, D, N_EXPERTS), D ** -0.5)
    router_b = nrm((DEPTH, N_EXPERTS), 0.01)
    exp_w_gu = nrm((DEPTH, N_EXPERTS, D, 2 * D_FF_EXPERT), D ** -0.5)
    exp_b_gu = nrm((DEPTH, N_EXPERTS, 2 * D_FF_EXPERT), 0.02)
    exp_w_down = nrm((DEPTH, N_EXPERTS, D_FF_EXPERT, D), D_FF_EXPERT ** -0.5)
    exp_b_down = nrm((DEPTH, N_EXPERTS, D), 0.02)
    return {'x': x, 'c': c, 'ctx': ctx, 'c_ctx': c_ctx, 'w_mod': w_mod, 'b_mod': b_mod,
            'norm1_w': norm1_w, 'norm2_w': norm2_w, 'w_in': w_in, 'q_norm_w': q_norm_w,
            'k_norm_w': k_norm_w, 'attn_sink': attn_sink, 'ssm_a_re': ssm_a_re, 'ssm_a_im': ssm_a_im,
            'ssm_log_dt': ssm_log_dt, 'ssm_b_re': ssm_b_re, 'ssm_b_im': ssm_b_im, 'ssm_c_re': ssm_c_re,
            'ssm_c_im': ssm_c_im, 'ssm_d': ssm_d, 'glu_w': glu_w, 'glu_b': glu_b, 'pool_w': pool_w,
            'pool_scale': pool_scale, 'out_norm_w': out_norm_w, 'w_out': w_out, 'router_w': router_w,
            'router_b': router_b, 'exp_w_gu': exp_w_gu, 'exp_b_gu': exp_b_gu, 'exp_w_down': exp_w_down,
            'exp_b_down': exp_b_down}


def reference(x, c, ctx, c_ctx, w_mod, b_mod, norm1_w, norm2_w, w_in, q_norm_w, k_norm_w, attn_sink,
              ssm_a_re, ssm_a_im, ssm_log_dt, ssm_b_re, ssm_b_im, ssm_c_re, ssm_c_im, ssm_d, glu_w, glu_b,
              pool_w, pool_scale, out_norm_w, w_out, router_w, router_b, exp_w_gu, exp_b_gu, exp_w_down,
              exp_b_down):
    B, L, D = x.shape
    Lc = ctx.shape[1]
    cos, sin = _axial_rope_tables(L)
    silu_c = jax.nn.silu(c)
    silu_cc = jax.nn.silu(c_ctx)
    for l in range(DEPTH):
        last = l == DEPTH - 1
        mod = jnp.split((silu_c @ w_mod[l] + b_mod[l])[:, None, :], N_MOD, axis=-1)
        n_cm = 2 if last else N_MOD
        mod_c = jnp.split(silu_cc @ w_mod[l][:, :n_cm * D] + b_mod[l][:n_cm * D], n_cm)

        hx = _modulate(_rmsnorm(x, norm1_w[l]), mod[0], mod[1])
        hc = _modulate(_rmsnorm(ctx, norm1_w[l]), mod_c[0], mod_c[1])
        px = hx @ w_in[l]
        c_lo, c_hi = (Q_END, U_END) if last else (0, P_END)
        pc = hc @ w_in[l][:, c_lo:c_hi]

        q = _apply_rope(_rmsnorm(px[..., :Q_END].reshape(B, L, N_HEADS, HEAD_DIM), q_norm_w[l]), cos, sin)
        k = _apply_rope(_rmsnorm(px[..., Q_END:K_END].reshape(B, L, N_KV_HEADS, HEAD_DIM), k_norm_w[l]), cos, sin)
        v = px[..., K_END:V_END].reshape(B, L, N_KV_HEADS, HEAD_DIM)
        kc = _rmsnorm(_cols(pc, c_lo, Q_END, K_END).reshape(B, Lc, N_KV_HEADS, HEAD_DIM), k_norm_w[l])
        vc = _cols(pc, c_lo, K_END, V_END).reshape(B, Lc, N_KV_HEADS, HEAD_DIM)
        attn_x = _window_attention(q, k, v, kc, vc, attn_sink[l])

        disc_f = _s5_discretize(ssm_a_re[l, 0], ssm_a_im[l, 0], ssm_log_dt[l, 0], ssm_b_re[l, 0], ssm_b_im[l, 0])
        disc_b = _s5_discretize(ssm_a_re[l, 1], ssm_a_im[l, 1], ssm_log_dt[l, 1], ssm_b_re[l, 1], ssm_b_im[l, 1])
        u_x = px[..., V_END:U_END].reshape(B, L, SSM_GROUPS, SSM_GROUP).astype(jnp.float32)
        u_c = _cols(pc, c_lo, V_END, U_END).reshape(B, Lc, SSM_GROUPS, SSM_GROUP).astype(jnp.float32)
        sc_f = _s5_scan(u_c, disc_f, None, reverse=False)
        sc_b = _s5_scan(u_c, disc_b, None, reverse=True)
        sx_f = _s5_scan(u_x, disc_f, (sc_f[0][:, -1], sc_f[1][:, -1]), reverse=False)
        sx_b = _s5_scan(u_x, disc_b, (sc_b[0][:, 0], sc_b[1][:, 0]), reverse=True)
        ssm_x = _s5_output(sx_f, sx_b, u_x, ssm_c_re[l], ssm_c_im[l], ssm_d[l], glu_w[l], glu_b[l]).astype(x.dtype)

        pool_x = _multiscale_pool(px[..., U_END:P_END], pool_w[l], pool_scale[l])

        x = x + mod[2] * (_merge_groups(attn_x, ssm_x, pool_x, out_norm_w[l]) @ w_out[l])
        if not last:
            qc = _rmsnorm(_cols(pc, c_lo, 0, Q_END).reshape(B, Lc, N_HEADS, HEAD_DIM), q_norm_w[l])
            attn_c = _ctx_attention(qc, kc, vc, attn_sink[l])
            ssm_c = _s5_output(sc_f, sc_b, u_c, ssm_c_re[l], ssm_c_im[l], ssm_d[l], glu_w[l], glu_b[l]).astype(ctx.dtype)
            pool_c = _multiscale_pool(_cols(pc, c_lo, U_END, P_END), pool_w[l], pool_scale[l])
            ctx = ctx + mod_c[2] * (_merge_groups(attn_c, ssm_c, pool_c, out_norm_w[l]) @ w_out[l])

        h2x = _modulate(_rmsnorm(x, norm2_w[l]), mod[3], mod[4]).reshape(B * L, D)
        if last:
            y2 = _moe(h2x, router_w[l], router_b[l], exp_w_gu[l], exp_b_gu[l], exp_w_down[l], exp_b_down[l])
            x = x + mod[5] * y2.reshape(B, L, D)
        else:
            h2c = _modulate(_rmsnorm(ctx, norm2_w[l]), mod_c[3], mod_c[4]).reshape(B * Lc, D)
            y2 = _moe(jnp.concatenate([h2x, h2c], axis=0), router_w[l], router_b[l], exp_w_gu[l],
                      exp_b_gu[l], exp_w_down[l], exp_b_down[l])
            x = x + mod[5] * y2[:B * L].reshape(B, L, D)
            ctx = ctx + mod_c[5] * y2[B * L:].reshape(B, Lc, D)
    return x
```

```python
import functools
import math

import jax
import jax.numpy as jnp
from jax import lax
from jax.experimental import pallas as pl
from jax.experimental.pallas import tpu as pltpu

D = 1024
NB = 8
L_LAT = 2048
L_CTX = 256
LT = L_CTX + L_LAT
DEPTH = 4
N_MOD = 6
EPS = 1e-6
N_HEADS = 8
HEAD_DIM = 64
ATTN_W = 512
WINDOW = 128
ATTN_SCALE = HEAD_DIM ** -0.5
ROPE_BASE = 10000.0
GRID_W = 64
SSM_W = 256
SSM_GROUP = 16
SSM_GROUPS = 16
SSM_STATE = 64
NS = SSM_GROUPS * SSM_STATE
POOL_W = 256
POOL_GROUP = 64
N_EXP = 32
TOP_K = 4
D_FF = 1024
SWIGLU_LIMIT = 7.0
SWIGLU_ALPHA = 1.702

LANES = 128
SUBLANES = 8
VMEM_LIMIT = 56 * 1024 * 1024

TM_TOK = 768
TQ = 256
KSPAN = TQ + 2 * WINDOW
T_SCAN = 32
N_CHUNK = LT // T_SCAN
N_CHUNK_CTX = L_CTX // T_SCAN
TM_MOE = 512
N_TOK = NB * LT
N_ASSIGN = N_TOK * TOP_K
NT_MOE = N_ASSIGN // TM_MOE + N_EXP
FF_CHUNK = 512

C_Q = 0
C_K = 512
C_V = 768
C_U = 1024
C_P = 1280
NW_IN = 1536

BF = jnp.bfloat16
F32 = jnp.float32


def _split_bf16(a):
    hi = a.astype(BF)
    lo = (a - hi.astype(F32)).astype(BF)
    return hi, lo


def _dot(a, b):
    return jnp.dot(a, b, preferred_element_type=F32)


def _dot3(a, b):
    ah, al = _split_bf16(a)
    bh, bl = _split_bf16(b)
    return _dot(ah, bh) + (_dot(ah, bl) + _dot(al, bh))


def _mod_kernel(c_ref, w_ref, b_ref, o_ref):
    c = c_ref[...]
    s = c * jax.nn.sigmoid(c)
    o_ref[0] = _dot3(s, w_ref[0]) + b_ref[0]


def _adaln(cc, w_mod, b_mod):
    tn = 1536
    return pl.pallas_call(
        _mod_kernel,
        out_shape=jax.ShapeDtypeStruct((DEPTH, 16, N_MOD * D), F32),
        grid=(DEPTH, N_MOD * D // tn),
        in_specs=[
            pl.BlockSpec((16, D), lambda l, j: (0, 0)),
            pl.BlockSpec((1, D, tn), lambda l, j: (l, 0, j)),
            pl.BlockSpec((1, 1, tn), lambda l, j: (l, 0, j)),
        ],
        out_specs=pl.BlockSpec((1, 16, tn), lambda l, j: (l, 0, j)),
        compiler_params=pltpu.CompilerParams(
            dimension_semantics=("arbitrary", "arbitrary"), vmem_limit_bytes=VMEM_LIMIT),
        name="adaln_mod",
    )(cc, w_mod, b_mod.reshape(DEPTH, 1, N_MOD * D))


def _mod_rows(mod_ref, b, is_ctx, idx):
    lat = mod_ref[pl.ds(b, 1), idx * D:(idx + 1) * D]
    ctx = mod_ref[NB:NB + 1, idx * D:(idx + 1) * D]
    return jnp.where(is_ctx, ctx, lat)


def _rmsnorm(x, w):
    ms = jnp.mean(x * x, axis=-1, keepdims=True)
    return x * lax.rsqrt(ms + EPS) * w


def _headnorm_pair(t, w2, lane):
    sq = t * t
    first = lane < HEAD_DIM
    a = jnp.sum(jnp.where(first, sq, 0.0), axis=-1, keepdims=True)
    b = jnp.sum(jnp.where(first, 0.0, sq), axis=-1, keepdims=True)
    ms = jnp.where(first, a, b) * (1.0 / HEAD_DIM)
    return t * lax.rsqrt(ms + EPS) * w2


def _rope_pair(t, cos, sin_signed, lane):
    lower = (lane % 32) < 16
    partner = jnp.where(lower, pltpu.roll(t, LANES - 16, 1), pltpu.roll(t, 16, 1))
    return t * cos + partner * sin_signed


def _inproj_kernel(x_ref, mod_ref, n1_ref, w_ref, qw_ref, kw_ref, cos_ref, sin_ref,
                   q_ref, k_ref, v_ref, u_ref, p_ref):
    b = pl.program_id(0)
    i = pl.program_id(1)
    x = x_ref[0]
    row = i * TM_TOK + lax.broadcasted_iota(jnp.int32, (TM_TOK, 1), 0)
    is_ctx = row < L_CTX
    h = _rmsnorm(x, n1_ref[...])
    h = h * (1.0 + _mod_rows(mod_ref, b, is_ctx, 1)) + _mod_rows(mod_ref, b, is_ctx, 0)
    px = _dot(h.astype(BF), w_ref[...])
    lane = lax.broadcasted_iota(jnp.int32, (TM_TOK, LANES), 1)
    cos = cos_ref[...]
    sin = sin_ref[...]
    for j in range(ATTN_W // LANES):
        t = px[:, C_Q + j * LANES:C_Q + (j + 1) * LANES]
        t = _rope_pair(_headnorm_pair(t, qw_ref[...], lane), cos, sin, lane) * ATTN_SCALE
        q_ref[0, :, j * LANES:(j + 1) * LANES] = t.astype(BF)
    for g in range(2):
        t = px[:, C_K + g * LANES:C_K + (g + 1) * LANES]
        t = _rope_pair(_headnorm_pair(t, kw_ref[...], lane), cos, sin, lane)
        k_ref[0, :, g * LANES:(g + 1) * LANES] = t.astype(BF)
    v_ref[0] = px[:, C_V:C_U].astype(BF)
    u_ref[0] = px[:, C_U:C_P]
    p_ref[0] = px[:, C_P:NW_IN]


def _inproj(xc, mod_l, n1, w_ext, qw2, kw2, cos2, sin2):
    tok = lambda w: pl.BlockSpec((1, TM_TOK, w), lambda b, i: (b, i, 0))
    full = lambda s: pl.BlockSpec(s, lambda b, i: (0,) * len(s))
    return pl.pallas_call(
        _inproj_kernel,
        out_shape=(
            jax.ShapeDtypeStruct((NB, LT, ATTN_W), BF),
            jax.ShapeDtypeStruct((NB, LT, 256), BF),
            jax.ShapeDtypeStruct((NB, LT, 256), BF),
            jax.ShapeDtypeStruct((NB, LT, SSM_W), F32),
            jax.ShapeDtypeStruct((NB, LT, POOL_W), F32),
        ),
        grid=(NB, LT // TM_TOK),
        in_specs=[
            tok(D), full((16, N_MOD * D)), full((1, D)), full((D, NW_IN)),
            full((1, LANES)), full((1, LANES)),
            pl.BlockSpec((TM_TOK, LANES), lambda b, i: (i, 0)),
            pl.BlockSpec((TM_TOK, LANES), lambda b, i: (i, 0)),
        ],
        out_specs=(tok(ATTN_W), tok(256), tok(256), tok(SSM_W), tok(POOL_W)),
        compiler_params=pltpu.CompilerParams(
            dimension_semantics=("arbitrary", "arbitrary"), vmem_limit_bytes=VMEM_LIMIT),
        name="inproj",
    )(xc, mod_l, n1, w_ext, qw2, kw2, cos2, sin2)


def _attn_kernel(sink_ref, q_ref, k_ref, v_ref, onw_ref, o_ref, acc_ref):
    j = pl.program_id(1)
    start = pl.multiple_of(jnp.clip(j * TQ - WINDOW, LANES, LT - KSPAN), LANES)
    lane = lax.broadcasted_iota(jnp.int32, (TQ, LANES), 1)
    first = lane < HEAD_DIM
    row2 = lax.broadcasted_iota(jnp.int32, (2 * TQ, 1), 0)
    qpos = j * TQ + jnp.where(row2 < TQ, row2, row2 - TQ)
    kpos = start + lax.broadcasted_iota(jnp.int32, (1, KSPAN), 1)
    valid = (jnp.abs(qpos - kpos) <= WINDOW) & (kpos >= L_CTX) & (j >= 1)
    for g in range(2):
        kc = k_ref[0, 0:L_CTX, g * LANES:(g + 1) * LANES]
        vc = v_ref[0, 0:L_CTX, g * LANES:(g + 1) * LANES]
        kl = k_ref[0, pl.ds(start, KSPAN), g * LANES:(g + 1) * LANES]
        vl = v_ref[0, pl.ds(start, KSPAN), g * LANES:(g + 1) * LANES]
        for hp in (2 * g, 2 * g + 1):
            q2 = q_ref[0, :, hp * LANES:(hp + 1) * LANES]
            zero = jnp.zeros_like(q2)
            qs = jnp.concatenate([jnp.where(first, q2, zero), jnp.where(first, zero, q2)], axis=0)
            dn = (((1,), (1,)), ((), ()))
            s_ctx = lax.dot_general(qs, kc, dn, preferred_element_type=F32)
            s_loc = lax.dot_general(qs, kl, dn, preferred_element_type=F32)
            s_loc = jnp.where(valid, s_loc, -1e30)
            sink = jnp.where(row2 < TQ, sink_ref[2 * hp], sink_ref[2 * hp + 1])
            m = jnp.maximum(jnp.maximum(jnp.max(s_loc, axis=-1, keepdims=True),
                                        jnp.max(s_ctx, axis=-1, keepdims=True)), sink)
            e_loc = jnp.exp(s_loc - m)
            e_ctx = jnp.exp(s_ctx - m)
            den = (jnp.sum(e_loc, axis=-1, keepdims=True) + jnp.sum(e_ctx, axis=-1, keepdims=True)
                   + jnp.exp(sink - m))
            o = _dot(e_loc.astype(BF), vl) + _dot(e_ctx.astype(BF), vc)
            o = o / den
            acc_ref[:, hp * LANES:(hp + 1) * LANES] = jnp.where(first, o[0:TQ], o[TQ:2 * TQ])
    o_ref[0] = _rmsnorm(acc_ref[...], onw_ref[...]).astype(BF)


def _attention(sink, q, kd, vd, onw_attn):
    return pl.pallas_call(
        _attn_kernel,
        out_shape=jax.ShapeDtypeStruct((NB, LT, ATTN_W), BF),
        grid_spec=pltpu.PrefetchScalarGridSpec(
            num_scalar_prefetch=1,
            grid=(NB, LT // TQ),
            in_specs=[
                pl.BlockSpec((1, TQ, ATTN_W), lambda b, j, s: (b, j, 0)),
                pl.BlockSpec((1, LT, 256), lambda b, j, s: (b, 0, 0)),
                pl.BlockSpec((1, LT, 256), lambda b, j, s: (b, 0, 0)),
                pl.BlockSpec((1, ATTN_W), lambda b, j, s: (0, 0)),
            ],
            out_specs=pl.BlockSpec((1, TQ, ATTN_W), lambda b, j, s: (b, j, 0)),
            scratch_shapes=[pltpu.VMEM((TQ, ATTN_W), F32)],
        ),
        compiler_params=pltpu.CompilerParams(
            dimension_semantics=("arbitrary", "arbitrary"), vmem_limit_bytes=VMEM_LIMIT),
        name="window_attn",
    )(sink, q, kd, vd, onw_attn)


def _bwd_chunk(i):
    return jnp.where(i < N_CHUNK_CTX, N_CHUNK_CTX - 1 - i, N_CHUNK - 1 - (i - N_CHUNK_CTX))


def _scan_kernel(uf_ref, ub_ref, pin_ref, pout_ref, a_ref, bm_ref, cm_ref, yf_ref, yb_ref,
                 xs_ref, st_ref):
    i = pl.program_id(0)
    rows = NB * T_SCAN

    @pl.when(i == 0)
    def _():
        st_ref[...] = jnp.zeros_like(st_ref)

    def one_direction(d, u_ref, y_ref, reverse):
        u = u_ref[...].reshape(rows, SSM_W).astype(BF)
        ui = _dot(pin_ref[...], u).astype(BF)
        xs_ref[...] = _dot(ui, bm_ref[d])
        ar = jnp.broadcast_to(a_ref[d, 0:1, :], (NB, NS))
        ai = jnp.broadcast_to(a_ref[d, 1:2, :], (NB, NS))
        sr = st_ref[d, 0]
        si = st_ref[d, 1]
        for t in (range(T_SCAN - 1, -1, -1) if reverse else range(T_SCAN)):
            r = pl.ds(t * NB, NB)
            nr = ar * sr - ai * si + xs_ref[r, 0:NS]
            ni = ar * si + ai * sr + xs_ref[r, NS:2 * NS]
            sr, si = nr, ni
            xs_ref[r, 0:NS] = sr
            xs_ref[r, NS:2 * NS] = si
        st_ref[d, 0] = sr
        st_ref[d, 1] = si
        y = _dot(xs_ref[...].astype(BF), cm_ref[d])
        yh, yl = _split_bf16(y)
        y = _dot(pout_ref[...], yh) + _dot(pout_ref[...], yl)
        y_ref[...] = y.reshape(NB, T_SCAN, SSM_W)

    one_direction(0, uf_ref, yf_ref, False)
    one_direction(1, ub_ref, yb_ref, True)


def _s5_scan(u, pin, pout, a_tab, bm, cm):
    rows = NB * T_SCAN
    full = lambda s: pl.BlockSpec(s, lambda i: (0,) * len(s))
    chunk_f = pl.BlockSpec((NB, T_SCAN, SSM_W), lambda i: (0, i, 0))
    chunk_b = pl.BlockSpec((NB, T_SCAN, SSM_W), lambda i: (0, _bwd_chunk(i), 0))
    return pl.pallas_call(
        _scan_kernel,
        out_shape=(jax.ShapeDtypeStruct((NB, LT, SSM_W), F32),
                   jax.ShapeDtypeStruct((NB, LT, SSM_W), F32)),
        grid=(N_CHUNK,),
        in_specs=[chunk_f, chunk_b, full((rows, rows)), full((rows, rows)),
                  full((2, 2, NS)), full((2, SSM_W, 2 * NS)), full((2, 2 * NS, SSM_W))],
        out_specs=(chunk_f, chunk_b),
        scratch_shapes=[pltpu.VMEM((rows, 2 * NS), F32), pltpu.VMEM((2, 2, NB, NS), F32)],
        compiler_params=pltpu.CompilerParams(
            dimension_semantics=("arbitrary",), vmem_limit_bytes=VMEM_LIMIT),
        name="s5_scan",
    )(u, u, pin, pout, a_tab, bm, cm)


def _pool_segment(ps, length):
    n = length + 2 * SUBLANES
    z = jnp.zeros((SUBLANES, POOL_W), F32)
    pe = jnp.concatenate([z, ps, z], axis=0)
    a1 = pe + pltpu.roll(pe, 1, 0)
    a2 = a1 + pltpu.roll(a1, 2, 0)
    a3 = a2 + pltpu.roll(a2, 4, 0)
    a4 = a3 + pltpu.roll(a3, 8, 0)
    lane = lax.broadcasted_iota(jnp.int32, (1, POOL_W), 1)
    half = jnp.where(lane < 64, 1, jnp.where(lane < 128, 2, jnp.where(lane < 192, 4, 8)))
    s = jnp.where(lane < 64, a1,
                  jnp.where(lane < 128, pltpu.roll(a2, n - 1, 0),
                            jnp.where(lane < 192, pltpu.roll(a3, n - 3, 0), pltpu.roll(a4, n - 7, 0))))
    s = s[SUBLANES:SUBLANES + length]
    t = lax.broadcasted_iota(jnp.int32, (length, 1), 0)
    cnt = jnp.minimum(t + half, length) - jnp.maximum(t - half, 0)
    return s / cnt.astype(F32) - ps


def _pool_kernel(p_ref, w_ref, sc_ref, o_ref):
    for lo, length in ((0, L_CTX), (L_CTX, L_LAT)):
        dlt = _pool_segment(p_ref[0, lo:lo + length, :], length)
        y = _dot(dlt.astype(BF), w_ref[...]) * sc_ref[...]
        o_ref[0, lo:lo + length, :] = y.astype(BF)


def _pool(p, w_bd, scale):
    return pl.pallas_call(
        _pool_kernel,
        out_shape=jax.ShapeDtypeStruct((NB, LT, POOL_W), BF),
        grid=(NB,),
        in_specs=[pl.BlockSpec((1, LT, POOL_W), lambda b: (b, 0, 0)),
                  pl.BlockSpec((POOL_W, POOL_W), lambda b: (0, 0)),
                  pl.BlockSpec((1, POOL_W), lambda b: (0, 0))],
        out_specs=pl.BlockSpec((1, LT, POOL_W), lambda b: (b, 0, 0)),
        compiler_params=pltpu.CompilerParams(
            dimension_semantics=("arbitrary",), vmem_limit_bytes=VMEM_LIMIT),
        name="pool",
    )(p, w_bd, scale)


def _mix_kernel(x_ref, at_ref, yf_ref, yb_ref, u_ref, pl_ref, mod_ref, d_ref, gw_ref, gb_ref, onw_ref,
                wo_ref, n2_ref, rw_ref, rb_ref, tri_ref,
                x1_ref, h2_ref, rt_ref, cnt_ref, run_ref):
    b = pl.program_id(0)
    i = pl.program_id(1)

    @pl.when((b == 0) & (i == 0))
    def _():
        run_ref[...] = jnp.zeros_like(run_ref)

    row = i * TM_TOK + lax.broadcasted_iota(jnp.int32, (TM_TOK, 1), 0)
    is_ctx = row < L_CTX
    y = yf_ref[0] + yb_ref[0] + d_ref[...] * u_ref[0]
    g = jax.nn.gelu(y, approximate=True)
    s = g * jax.nn.sigmoid(_dot(g.astype(BF), gw_ref[...]) + gb_ref[...])
    s = _rmsnorm(s, onw_ref[...]).astype(BF)
    mix = (_dot(at_ref[0], wo_ref[0:ATTN_W, :])
           + _dot(s, wo_ref[ATTN_W:ATTN_W + SSM_W, :])
           + _dot(pl_ref[0], wo_ref[ATTN_W + SSM_W:D, :]))
    x1 = x_ref[0] + _mod_rows(mod_ref, b, is_ctx, 2) * mix
    x1_ref[0] = x1
    h2 = _rmsnorm(x1, n2_ref[...])
    h2 = h2 * (1.0 + _mod_rows(mod_ref, b, is_ctx, 4)) + _mod_rows(mod_ref, b, is_ctx, 3)
    h2_ref[0] = h2.astype(BF)
    logits = _dot3(h2, rw_ref[...]) + rb_ref[...]
    lane = lax.broadcasted_iota(jnp.int32, (TM_TOK, LANES), 1)
    vals, idxs = [], []
    cur = logits
    for _ in range(TOP_K):
        m = jnp.max(cur, axis=-1, keepdims=True)
        idx = jnp.min(jnp.where(cur == m, lane, LANES), axis=-1, keepdims=True)
        vals.append(m)
        idxs.append(idx)
        cur = jnp.where(lane == idx, -jnp.inf, cur)
    ex = [jnp.exp(v - vals[0]) for v in vals]
    den = ex[0] + ex[1] + ex[2] + ex[3]
    onehot = jnp.zeros((TM_TOK, LANES), F32)
    for idx in idxs:
        onehot = onehot + (lane == idx).astype(F32)
    before = _dot(tri_ref[...], onehot.astype(BF)) + run_ref[...]
    route = jnp.zeros((TM_TOK, LANES), F32)
    for k in range(TOP_K):
        rank = jnp.sum(jnp.where(lane == idxs[k], before, 0.0), axis=-1, keepdims=True)
        route = jnp.where(lane == k, idxs[k].astype(F32), route)
        route = jnp.where(lane == TOP_K + k, ex[k] / den, route)
        route = jnp.where(lane == 2 * TOP_K + k, rank, route)
    rt_ref[0] = route
    run_ref[...] = run_ref[...] + jnp.sum(onehot, axis=0, keepdims=True)
    cnt_ref[...] = jnp.broadcast_to(run_ref[...], (SUBLANES, LANES))


def _mix(xc, attn, yf, yb, u, pool, mod_l, ssm_d, glu_w, glu_b, onw_ssm, w_out, n2, rw, rb, tri):
    tok = lambda w: pl.BlockSpec((1, TM_TOK, w), lambda b, i: (b, i, 0))
    full = lambda s: pl.BlockSpec(s, lambda b, i: (0,) * len(s))
    return pl.pallas_call(
        _mix_kernel,
        out_shape=(jax.ShapeDtypeStruct((NB, LT, D), F32),
                   jax.ShapeDtypeStruct((NB, LT, D), BF),
                   jax.ShapeDtypeStruct((NB, LT, LANES), F32),
                   jax.ShapeDtypeStruct((SUBLANES, LANES), F32)),
        grid=(NB, LT // TM_TOK),
        in_specs=[tok(D), tok(ATTN_W), tok(SSM_W), tok(SSM_W), tok(SSM_W), tok(POOL_W),
                  full((16, N_MOD * D)), full((1, SSM_W)), full((SSM_W, SSM_W)), full((1, SSM_W)),
                  full((1, SSM_W)), full((D, D)), full((1, D)), full((D, LANES)), full((1, LANES)),
                  full((TM_TOK, TM_TOK))],
        out_specs=(tok(D), tok(D), tok(LANES), full((SUBLANES, LANES))),
        scratch_shapes=[pltpu.VMEM((1, LANES), F32)],
        compiler_params=pltpu.CompilerParams(
            dimension_semantics=("arbitrary", "arbitrary"), vmem_limit_bytes=VMEM_LIMIT),
        name="mix_router",
    )(xc, attn, yf, yb, u, pool, mod_l, ssm_d, glu_w, glu_b, onw_ssm, w_out, n2, rw, rb, tri)


def _moe_kernel(te_ref, nu_ref, x_ref, wgu_ref, bgu_ref, wd_ref, bd_ref, o_ref, wgu_s, wd_s):
    i = pl.program_id(0)

    @pl.when(i < nu_ref[0])
    def _():
        e = te_ref[i]
        prev = te_ref[jnp.maximum(i - 1, 0)]

        @pl.when((i == 0) | (e != prev))
        def _():
            def cast_gu(r, c):
                rs = pl.ds(pl.multiple_of(r * 128, 128), 128)
                wgu_s[rs, :] = wgu_ref[rs, :].astype(BF)
                return c

            def cast_d(r, c):
                rs = pl.ds(pl.multiple_of(r * 128, 128), 128)
                wd_s[rs, :] = wd_ref[rs, :].astype(BF)
                return c

            lax.fori_loop(0, D // 128, cast_gu, 0)
            lax.fori_loop(0, D_FF // 128, cast_d, 0)

        x = x_ref[...]
        for c in range(D_FF // FF_CHUNK):
            lo = c * FF_CHUNK
            gate = _dot(x, wgu_s[:, lo:lo + FF_CHUNK]) + bgu_ref[:, lo:lo + FF_CHUNK]
            up = (_dot(x, wgu_s[:, D_FF + lo:D_FF + lo + FF_CHUNK])
                  + bgu_ref[:, D_FF + lo:D_FF + lo + FF_CHUNK])
            gate = jnp.minimum(gate, SWIGLU_LIMIT)
            up = jnp.clip(up, -SWIGLU_LIMIT, SWIGLU_LIMIT)
            act = (up + 1.0) * (gate * jax.nn.sigmoid(SWIGLU_ALPHA * gate))
            part = _dot(act.astype(BF), wd_s[lo:lo + FF_CHUNK, :])
            if c == 0:
                o_ref[...] = part + bd_ref[...]
            else:
                o_ref[...] += part


def _moe(l, tile_e, n_used, xs, w_gu, b_gu, w_down, b_down):
    def tile(i, te, nu):
        return (jnp.minimum(i, nu[0] - 1), 0)

    def expert(i, te, nu):
        return te[jnp.minimum(i, nu[0] - 1)]

    return pl.pallas_call(
        _moe_kernel,
        out_shape=jax.ShapeDtypeStruct((NT_MOE * TM_MOE, D), F32),
        grid_spec=pltpu.PrefetchScalarGridSpec(
            num_scalar_prefetch=2,
            grid=(NT_MOE,),
            in_specs=[
                pl.BlockSpec((TM_MOE, D), tile),
                pl.BlockSpec((None, None, D, 2 * D_FF), lambda i, te, nu: (l, expert(i, te, nu), 0, 0)),
                pl.BlockSpec((None, None, 1, 2 * D_FF), lambda i, te, nu: (l, expert(i, te, nu), 0, 0)),
                pl.BlockSpec((None, None, D_FF, D), lambda i, te, nu: (l, expert(i, te, nu), 0, 0)),
                pl.BlockSpec((None, None, 1, D), lambda i, te, nu: (l, expert(i, te, nu), 0, 0)),
            ],
            out_specs=pl.BlockSpec((TM_MOE, D), tile),
            scratch_shapes=[pltpu.VMEM((D, 2 * D_FF), BF), pltpu.VMEM((D_FF, D), BF)],
        ),
        compiler_params=pltpu.CompilerParams(
            dimension_semantics=("arbitrary",), vmem_limit_bytes=VMEM_LIMIT),
        name="moe_experts",
    )(tile_e, n_used, xs, w_gu, b_gu.reshape(DEPTH, N_EXP, 1, 2 * D_FF), w_down,
      b_down.reshape(DEPTH, N_EXP, 1, D))


def _rope_tables():
    pos = jnp.arange(L_LAT)
    row = (pos // GRID_W).astype(F32)
    col = (pos % GRID_W).astype(F32)
    inv = ROPE_BASE ** (-jnp.arange(0, 32, 2, dtype=F32) / 32)
    ang_r = row[:, None] * inv
    ang_c = col[:, None] * inv
    ang = jnp.concatenate([ang_r, ang_r, ang_c, ang_c], axis=-1)
    cos = jnp.concatenate([jnp.ones((L_CTX, HEAD_DIM), F32), jnp.cos(ang)], axis=0)
    sin = jnp.concatenate([jnp.zeros((L_CTX, HEAD_DIM), F32), jnp.sin(ang)], axis=0)
    sign = jnp.where((jnp.arange(HEAD_DIM) % 32) < 16, -1.0, 1.0).astype(F32)
    return jnp.tile(cos, (1, 2)), jnp.tile(sin * sign, (1, 2))


def _s5_tables(a_re, a_im, log_dt, b_re, b_im, c_re, c_im):
    dt = jnp.exp(log_dt)[..., None]
    mag = jnp.exp(a_re * dt)
    ar = mag * jnp.cos(a_im * dt)
    ai = mag * jnp.sin(a_im * dt)
    den = a_re * a_re + a_im * a_im
    qr = ((ar - 1) * a_re + ai * a_im) / den
    qi = (ai * a_re - (ar - 1) * a_im) / den
    bbr = qr[..., None] * b_re - qi[..., None] * b_im
    bbi = qr[..., None] * b_im + qi[..., None] * b_re
    eye = jnp.eye(SSM_GROUPS, dtype=F32)
    to_in = lambda m: jnp.einsum('dgnp,gh->dgphn', m, eye).reshape(2, SSM_W, NS)
    bm = jnp.concatenate([to_in(bbr), to_in(bbi)], axis=-1).astype(BF)
    to_out = lambda m: jnp.einsum('dgpn,gh->dgnhp', m, eye).reshape(2, NS, SSM_W)
    cm = jnp.concatenate([to_out(c_re), -to_out(c_im)], axis=1).astype(BF)
    a_tab = jnp.stack([ar.reshape(2, NS), ai.reshape(2, NS)], axis=1)
    return a_tab, bm, cm


def _interleave_perms():
    rows = NB * T_SCAN
    r = jnp.arange(rows)
    src = (r % NB) * T_SCAN + r // NB
    pin = (src[:, None] == jnp.arange(rows)[None, :]).astype(BF)
    return pin, pin.T


def kernel(x, c, ctx, c_ctx, w_mod, b_mod, norm1_w, norm2_w, w_in, q_norm_w, k_norm_w, attn_sink,
           ssm_a_re, ssm_a_im, ssm_log_dt, ssm_b_re, ssm_b_im, ssm_c_re, ssm_c_im, ssm_d, glu_w, glu_b,
           pool_w, pool_scale, out_norm_w, w_out, router_w, router_b, exp_w_gu, exp_b_gu, exp_w_down,
           exp_b_down):
    xc = jnp.concatenate([ctx, x], axis=1)
    cc = jnp.concatenate([c, c_ctx[None, :], jnp.zeros((16 - NB - 1, D), F32)], axis=0)
    mod = _adaln(cc, w_mod, b_mod)
    cos2, sin2 = _rope_tables()
    pin, pout = _interleave_perms()
    tri = (jnp.arange(TM_TOK)[:, None] > jnp.arange(TM_TOK)[None, :]).astype(BF)
    is_ctx_row = (jnp.arange(LT) < L_CTX)[None, :, None]

    for l in range(DEPTH):
        wl = w_in[l]
        kcols = wl[:, 512:640]
        vcols = wl[:, 640:768]
        dup = lambda m: jnp.concatenate([m[:, :64], m[:, :64], m[:, 64:], m[:, 64:]], axis=1)
        w_ext = jnp.concatenate([wl[:, :512], dup(kcols), dup(vcols), wl[:, 768:]], axis=1).astype(BF)
        qw2 = jnp.tile(q_norm_w[l], 2)[None, :]
        kw2 = jnp.tile(k_norm_w[l], 2)[None, :]
        q, kd, vd, u, p = _inproj(xc, mod[l], norm1_w[l][None, :], w_ext, qw2, kw2, cos2, sin2)

        attn = _attention(attn_sink[l], q, kd, vd, out_norm_w[l][None, :ATTN_W])

        a_tab, bm, cm = _s5_tables(ssm_a_re[l], ssm_a_im[l], ssm_log_dt[l], ssm_b_re[l], ssm_b_im[l],
                                   ssm_c_re[l], ssm_c_im[l])
        yf, yb = _s5_scan(u, pin, pout, a_tab, bm, cm)

        pw_bd = jnp.einsum('gcd,gh->gchd', pool_w[l], jnp.eye(4, dtype=F32)).reshape(POOL_W, POOL_W)
        pool = _pool(p, pw_bd.astype(BF), pool_scale[l][None, :])

        rw = jnp.concatenate([router_w[l], jnp.zeros((D, LANES - N_EXP), F32)], axis=1)
        rb = jnp.concatenate([router_b[l], jnp.full((LANES - N_EXP,), -1e30, F32)])[None, :]
        x1, h2, route, cnt = _mix(xc, attn, yf, yb, u, pool, mod[l], ssm_d[l][None, :],
                                  glu_w[l].astype(BF), glu_b[l][None, :], out_norm_w[l][None, ATTN_W:],
                                  w_out[l].astype(BF), norm2_w[l][None, :], rw, rb, tri)

        route = route.reshape(N_TOK, LANES)
        top_i = route[:, 0:TOP_K].astype(jnp.int32)
        gates = route[:, TOP_K:2 * TOP_K]
        rank = route[:, 2 * TOP_K:3 * TOP_K].astype(jnp.int32)
        counts = cnt[0, :N_EXP].astype(jnp.int32)
        padded = (counts + TM_MOE - 1) // TM_MOE * TM_MOE
        pend = jnp.cumsum(padded)
        pstart = pend - padded
        dest = pstart[top_i] + rank
        tok_of = jnp.repeat(jnp.arange(N_TOK, dtype=jnp.int32), TOP_K)
        row_tok = jnp.zeros((NT_MOE * TM_MOE,), jnp.int32).at[dest.reshape(-1)].set(tok_of)
        n_used = (pend[-1] // TM_MOE).astype(jnp.int32)[None]
        tile_e = jnp.minimum(
            jnp.searchsorted(pend, jnp.arange(NT_MOE, dtype=jnp.int32) * TM_MOE, side='right'),
            N_EXP - 1).astype(jnp.int32)
        xs = jnp.take(h2.reshape(N_TOK, D), row_tok, axis=0)

        yb_rows = _moe(l, tile_e, n_used, xs, exp_w_gu, exp_b_gu, exp_w_down, exp_b_down)

        y2 = jnp.einsum('nkd,nk->nd', jnp.take(yb_rows, dest.reshape(-1), axis=0).reshape(N_TOK, TOP_K, D),
                        gates).reshape(NB, LT, D)
        gate2 = jnp.where(is_ctx_row, mod[l][NB, 5 * D:][None, None, :], mod[l][:NB, None, 5 * D:])
        xc = x1 + gate2 * y2
    return xc[:, L_CTX:, :]
```

```python
import functools
import math

import jax
import jax.numpy as jnp
from jax import lax
from jax.experimental import pallas as pl
from jax.experimental.pallas import tpu as pltpu
from jax.experimental.pallas import tpu_sc as plsc

D = 1024
NB = 8
L_LAT = 2048
L_CTX = 256
LT = L_CTX + L_LAT
DEPTH = 4
N_MOD = 6
EPS = 1e-6
N_HEADS = 8
HEAD_DIM = 64
ATTN_W = 512
WINDOW = 128
ATTN_SCALE = HEAD_DIM ** -0.5
ROPE_BASE = 10000.0
GRID_W = 64
SSM_W = 256
SSM_GROUP = 16
SSM_GROUPS = 16
SSM_STATE = 64
NS = SSM_GROUPS * SSM_STATE
POOL_W = 256
POOL_GROUP = 64
N_EXP = 32
TOP_K = 4
D_FF = 1024
SWIGLU_LIMIT = 7.0
SWIGLU_ALPHA = 1.702

LANES = 128
SUBLANES = 8
VMEM_LIMIT = 56 * 1024 * 1024

TM_TOK = 768
TQ = 256
KSPAN = TQ + 2 * WINDOW
T_SCAN = 32
N_CHUNK = LT // T_SCAN
N_CHUNK_CTX = L_CTX // T_SCAN
TM_MOE = 512
N_TOK = NB * LT
N_ASSIGN = N_TOK * TOP_K
NT_MOE = N_ASSIGN // TM_MOE + N_EXP
R_MOE = NT_MOE * TM_MOE
FF_CHUNK = 512

SC_CORES = 2
SC_SUBCORES = 16
SC_WORKERS = SC_CORES * SC_SUBCORES
SC_CHUNK = 32

C_Q = 0
C_K = 512
C_V = 768
C_U = 1024
C_P = 1280
NW_IN = 1536

BF = jnp.bfloat16
F32 = jnp.float32


def _split_bf16(a):
    hi = a.astype(BF)
    lo = (a - hi.astype(F32)).astype(BF)
    return hi, lo


def _dot(a, b):
    return jnp.dot(a, b, preferred_element_type=F32)


def _dot3(a, b):
    ah, al = _split_bf16(a)
    bh, bl = _split_bf16(b)
    return _dot(ah, bh) + (_dot(ah, bl) + _dot(al, bh))


def _mod_kernel(c_ref, w_ref, b_ref, o_ref):
    c = c_ref[...]
    s = c * jax.nn.sigmoid(c)
    o_ref[0] = _dot3(s, w_ref[0]) + b_ref[0]


def _adaln(cc, w_mod, b_mod):
    tn = 1536
    return pl.pallas_call(
        _mod_kernel,
        out_shape=jax.ShapeDtypeStruct((DEPTH, 16, N_MOD * D), F32),
        grid=(DEPTH, N_MOD * D // tn),
        in_specs=[
            pl.BlockSpec((16, D), lambda l, j: (0, 0)),
            pl.BlockSpec((1, D, tn), lambda l, j: (l, 0, j)),
            pl.BlockSpec((1, 1, tn), lambda l, j: (l, 0, j)),
        ],
        out_specs=pl.BlockSpec((1, 16, tn), lambda l, j: (l, 0, j)),
        compiler_params=pltpu.CompilerParams(
            dimension_semantics=("arbitrary", "arbitrary"), vmem_limit_bytes=VMEM_LIMIT),
        name="adaln_mod",
    )(cc, w_mod, b_mod.reshape(DEPTH, 1, N_MOD * D))


def _mod_rows(mod_ref, b, is_ctx, idx):
    lat = mod_ref[pl.ds(b, 1), idx * D:(idx + 1) * D]
    ctx = mod_ref[NB:NB + 1, idx * D:(idx + 1) * D]
    return jnp.where(is_ctx, ctx, lat)


def _rmsnorm(x, w):
    ms = jnp.mean(x * x, axis=-1, keepdims=True)
    return x * lax.rsqrt(ms + EPS) * w


def _headnorm_pair(t, w2, lane):
    sq = t * t
    first = lane < HEAD_DIM
    a = jnp.sum(jnp.where(first, sq, 0.0), axis=-1, keepdims=True)
    b = jnp.sum(jnp.where(first, 0.0, sq), axis=-1, keepdims=True)
    ms = jnp.where(first, a, b) * (1.0 / HEAD_DIM)
    return t * lax.rsqrt(ms + EPS) * w2


def _rope_pair(t, cos, sin_signed, lane):
    lower = (lane % 32) < 16
    partner = jnp.where(lower, pltpu.roll(t, LANES - 16, 1), pltpu.roll(t, 16, 1))
    return t * cos + partner * sin_signed


def _inproj_kernel(x_ref, mod_ref, n1_ref, w_ref, qw_ref, kw_ref, cos_ref, sin_ref,
                   q_ref, k_ref, v_ref, u_ref, p_ref):
    b = pl.program_id(0)
    i = pl.program_id(1)
    x = x_ref[0]
    row = i * TM_TOK + lax.broadcasted_iota(jnp.int32, (TM_TOK, 1), 0)
    is_ctx = row < L_CTX
    h = _rmsnorm(x, n1_ref[...])
    h = h * (1.0 + _mod_rows(mod_ref, b, is_ctx, 1)) + _mod_rows(mod_ref, b, is_ctx, 0)
    px = _dot(h.astype(BF), w_ref[...])
    lane = lax.broadcasted_iota(jnp.int32, (TM_TOK, LANES), 1)
    cos = cos_ref[...]
    sin = sin_ref[...]
    for j in range(ATTN_W // LANES):
        t = px[:, C_Q + j * LANES:C_Q + (j + 1) * LANES]
        t = _rope_pair(_headnorm_pair(t, qw_ref[...], lane), cos, sin, lane) * ATTN_SCALE
        q_ref[0, :, j * LANES:(j + 1) * LANES] = t.astype(BF)
    for g in range(2):
        t = px[:, C_K + g * LANES:C_K + (g + 1) * LANES]
        t = _rope_pair(_headnorm_pair(t, kw_ref[...], lane), cos, sin, lane)
        k_ref[0, :, g * LANES:(g + 1) * LANES] = t.astype(BF)
    v_ref[0] = px[:, C_V:C_U].astype(BF)
    u_ref[0] = px[:, C_U:C_P]
    p_ref[0] = px[:, C_P:NW_IN]


def _inproj(xc, mod_l, n1, w_ext, qw2, kw2, cos2, sin2):
    tok = lambda w: pl.BlockSpec((1, TM_TOK, w), lambda b, i: (b, i, 0))
    full = lambda s: pl.BlockSpec(s, lambda b, i: (0,) * len(s))
    return pl.pallas_call(
        _inproj_kernel,
        out_shape=(
            jax.ShapeDtypeStruct((NB, LT, ATTN_W), BF),
            jax.ShapeDtypeStruct((NB, LT, 256), BF),
            jax.ShapeDtypeStruct((NB, LT, 256), BF),
            jax.ShapeDtypeStruct((NB, LT, SSM_W), F32),
            jax.ShapeDtypeStruct((NB, LT, POOL_W), F32),
        ),
        grid=(NB, LT // TM_TOK),
        in_specs=[
            tok(D), full((16, N_MOD * D)), full((1, D)), full((D, NW_IN)),
            full((1, LANES)), full((1, LANES)),
            pl.BlockSpec((TM_TOK, LANES), lambda b, i: (i, 0)),
            pl.BlockSpec((TM_TOK, LANES), lambda b, i: (i, 0)),
        ],
        out_specs=(tok(ATTN_W), tok(256), tok(256), tok(SSM_W), tok(POOL_W)),
        compiler_params=pltpu.CompilerParams(
            dimension_semantics=("arbitrary", "arbitrary"), vmem_limit_bytes=VMEM_LIMIT),
        name="inproj",
    )(xc, mod_l, n1, w_ext, qw2, kw2, cos2, sin2)


def _attn_kernel(sink_ref, q_ref, k_ref, v_ref, onw_ref, o_ref, acc_ref):
    j = pl.program_id(1)
    start = pl.multiple_of(jnp.clip(j * TQ - WINDOW, LANES, LT - KSPAN), LANES)
    lane = lax.broadcasted_iota(jnp.int32, (TQ, LANES), 1)
    first = lane < HEAD_DIM
    row2 = lax.broadcasted_iota(jnp.int32, (2 * TQ, 1), 0)
    qpos = j * TQ + jnp.where(row2 < TQ, row2, row2 - TQ)
    kpos = start + lax.broadcasted_iota(jnp.int32, (1, KSPAN), 1)
    valid = (jnp.abs(qpos - kpos) <= WINDOW) & (kpos >= L_CTX) & (j >= 1)
    for g in range(2):
        kc = k_ref[0, 0:L_CTX, g * LANES:(g + 1) * LANES]
        vc = v_ref[0, 0:L_CTX, g * LANES:(g + 1) * LANES]
        kl = k_ref[0, pl.ds(start, KSPAN), g * LANES:(g + 1) * LANES]
        vl = v_ref[0, pl.ds(start, KSPAN), g * LANES:(g + 1) * LANES]
        for hp in (2 * g, 2 * g + 1):
            q2 = q_ref[0, :, hp * LANES:(hp + 1) * LANES]
            zero = jnp.zeros_like(q2)
            qs = jnp.concatenate([jnp.where(first, q2, zero), jnp.where(first, zero, q2)], axis=0)
            dn = (((1,), (1,)), ((), ()))
            s_ctx = lax.dot_general(qs, kc, dn, preferred_element_type=F32)
            s_loc = lax.dot_general(qs, kl, dn, preferred_element_type=F32)
            s_loc = jnp.where(valid, s_loc, -1e30)
            sink = jnp.where(row2 < TQ, sink_ref[2 * hp], sink_ref[2 * hp + 1])
            m = jnp.maximum(jnp.maximum(jnp.max(s_loc, axis=-1, keepdims=True),
                                        jnp.max(s_ctx, axis=-1, keepdims=True)), sink)
            e_loc = jnp.exp(s_loc - m)
            e_ctx = jnp.exp(s_ctx - m)
            den = (jnp.sum(e_loc, axis=-1, keepdims=True) + jnp.sum(e_ctx, axis=-1, keepdims=True)
                   + jnp.exp(sink - m))
            o = _dot(e_loc.astype(BF), vl) + _dot(e_ctx.astype(BF), vc)
            o = o / den
            acc_ref[:, hp * LANES:(hp + 1) * LANES] = jnp.where(first, o[0:TQ], o[TQ:2 * TQ])
    o_ref[0] = _rmsnorm(acc_ref[...], onw_ref[...]).astype(BF)


def _attention(sink, q, kd, vd, onw_attn):
    return pl.pallas_call(
        _attn_kernel,
        out_shape=jax.ShapeDtypeStruct((NB, LT, ATTN_W), BF),
        grid_spec=pltpu.PrefetchScalarGridSpec(
            num_scalar_prefetch=1,
            grid=(NB, LT // TQ),
            in_specs=[
                pl.BlockSpec((1, TQ, ATTN_W), lambda b, j, s: (b, j, 0)),
                pl.BlockSpec((1, LT, 256), lambda b, j, s: (b, 0, 0)),
                pl.BlockSpec((1, LT, 256), lambda b, j, s: (b, 0, 0)),
                pl.BlockSpec((1, ATTN_W), lambda b, j, s: (0, 0)),
            ],
            out_specs=pl.BlockSpec((1, TQ, ATTN_W), lambda b, j, s: (b, j, 0)),
            scratch_shapes=[pltpu.VMEM((TQ, ATTN_W), F32)],
        ),
        compiler_params=pltpu.CompilerParams(
            dimension_semantics=("arbitrary", "arbitrary"), vmem_limit_bytes=VMEM_LIMIT),
        name="window_attn",
    )(sink, q, kd, vd, onw_attn)


def _bwd_chunk(i):
    return jnp.where(i < N_CHUNK_CTX, N_CHUNK_CTX - 1 - i, N_CHUNK - 1 - (i - N_CHUNK_CTX))


def _scan_kernel(uf_ref, ub_ref, pin_ref, pout_ref, a_ref, bm_ref, cm_ref, yf_ref, yb_ref,
                 xs_ref, st_ref):
    i = pl.program_id(0)
    rows = NB * T_SCAN

    @pl.when(i == 0)
    def _():
        st_ref[...] = jnp.zeros_like(st_ref)

    def one_direction(d, u_ref, y_ref, reverse):
        u = u_ref[...].reshape(rows, SSM_W).astype(BF)
        ui = _dot(pin_ref[...], u).astype(BF)
        xs_ref[...] = _dot(ui, bm_ref[d])
        ar = jnp.broadcast_to(a_ref[d, 0:1, :], (NB, NS))
        ai = jnp.broadcast_to(a_ref[d, 1:2, :], (NB, NS))
        sr = st_ref[d, 0]
        si = st_ref[d, 1]
        for t in (range(T_SCAN - 1, -1, -1) if reverse else range(T_SCAN)):
            r = pl.ds(t * NB, NB)
            nr = ar * sr - ai * si + xs_ref[r, 0:NS]
            ni = ar * si + ai * sr + xs_ref[r, NS:2 * NS]
            sr, si = nr, ni
            xs_ref[r, 0:NS] = sr
            xs_ref[r, NS:2 * NS] = si
        st_ref[d, 0] = sr
        st_ref[d, 1] = si
        y = _dot(xs_ref[...].astype(BF), cm_ref[d])
        yh, yl = _split_bf16(y)
        y = _dot(pout_ref[...], yh) + _dot(pout_ref[...], yl)
        y_ref[...] = y.reshape(NB, T_SCAN, SSM_W)

    one_direction(0, uf_ref, yf_ref, False)
    one_direction(1, ub_ref, yb_ref, True)


def _s5_scan(u, pin, pout, a_tab, bm, cm):
    rows = NB * T_SCAN
    full = lambda s: pl.BlockSpec(s, lambda i: (0,) * len(s))
    chunk_f = pl.BlockSpec((NB, T_SCAN, SSM_W), lambda i: (0, i, 0))
    chunk_b = pl.BlockSpec((NB, T_SCAN, SSM_W), lambda i: (0, _bwd_chunk(i), 0))
    return pl.pallas_call(
        _scan_kernel,
        out_shape=(jax.ShapeDtypeStruct((NB, LT, SSM_W), F32),
                   jax.ShapeDtypeStruct((NB, LT, SSM_W), F32)),
        grid=(N_CHUNK,),
        in_specs=[chunk_f, chunk_b, full((rows, rows)), full((rows, rows)),
                  full((2, 2, NS)), full((2, SSM_W, 2 * NS)), full((2, 2 * NS, SSM_W))],
        out_specs=(chunk_f, chunk_b),
        scratch_shapes=[pltpu.VMEM((rows, 2 * NS), F32), pltpu.VMEM((2, 2, NB, NS), F32)],
        compiler_params=pltpu.CompilerParams(
            dimension_semantics=("arbitrary",), vmem_limit_bytes=VMEM_LIMIT),
        name="s5_scan",
    )(u, u, pin, pout, a_tab, bm, cm)


def _pool_segment(ps, length):
    n = length + 2 * SUBLANES
    z = jnp.zeros((SUBLANES, POOL_W), F32)
    pe = jnp.concatenate([z, ps, z], axis=0)
    a1 = pe + pltpu.roll(pe, 1, 0)
    a2 = a1 + pltpu.roll(a1, 2, 0)
    a3 = a2 + pltpu.roll(a2, 4, 0)
    a4 = a3 + pltpu.roll(a3, 8, 0)
    lane = lax.broadcasted_iota(jnp.int32, (1, POOL_W), 1)
    half = jnp.where(lane < 64, 1, jnp.where(lane < 128, 2, jnp.where(lane < 192, 4, 8)))
    s = jnp.where(lane < 64, a1,
                  jnp.where(lane < 128, pltpu.roll(a2, n - 1, 0),
                            jnp.where(lane < 192, pltpu.roll(a3, n - 3, 0), pltpu.roll(a4, n - 7, 0))))
    s = s[SUBLANES:SUBLANES + length]
    t = lax.broadcasted_iota(jnp.int32, (length, 1), 0)
    cnt = jnp.minimum(t + half, length) - jnp.maximum(t - half, 0)
    return s / cnt.astype(F32) - ps


def _pool_kernel(p_ref, w_ref, sc_ref, o_ref):
    for lo, length in ((0, L_CTX), (L_CTX, L_LAT)):
        dlt = _pool_segment(p_ref[0, lo:lo + length, :], length)
        y = _dot(dlt.astype(BF), w_ref[...]) * sc_ref[...]
        o_ref[0, lo:lo + length, :] = y.astype(BF)


def _pool(p, w_bd, scale):
    return pl.pallas_call(
        _pool_kernel,
        out_shape=jax.ShapeDtypeStruct((NB, LT, POOL_W), BF),
        grid=(NB,),
        in_specs=[pl.BlockSpec((1, LT, POOL_W), lambda b: (b, 0, 0)),
                  pl.BlockSpec((POOL_W, POOL_W), lambda b: (0, 0)),
                  pl.BlockSpec((1, POOL_W), lambda b: (0, 0))],
        out_specs=pl.BlockSpec((1, LT, POOL_W), lambda b: (b, 0, 0)),
        compiler_params=pltpu.CompilerParams(
            dimension_semantics=("arbitrary",), vmem_limit_bytes=VMEM_LIMIT),
        name="pool",
    )(p, w_bd, scale)


def _mix_kernel(x_ref, at_ref, yf_ref, yb_ref, u_ref, pl_ref, mod_ref, d_ref, gw_ref, gb_ref, onw_ref,
                wo_ref, n2_ref, rw_ref, rb_ref, tri_ref,
                x1_ref, h2_ref, rt_ref, cnt_ref, run_ref):
    b = pl.program_id(0)
    i = pl.program_id(1)

    @pl.when((b == 0) & (i == 0))
    def _():
        run_ref[...] = jnp.zeros_like(run_ref)

    row = i * TM_TOK + lax.broadcasted_iota(jnp.int32, (TM_TOK, 1), 0)
    is_ctx = row < L_CTX
    y = yf_ref[0] + yb_ref[0] + d_ref[...] * u_ref[0]
    g = jax.nn.gelu(y, approximate=True)
    s = g * jax.nn.sigmoid(_dot(g.astype(BF), gw_ref[...]) + gb_ref[...])
    s = _rmsnorm(s, onw_ref[...]).astype(BF)
    mix = (_dot(at_ref[0], wo_ref[0:ATTN_W, :])
           + _dot(s, wo_ref[ATTN_W:ATTN_W + SSM_W, :])
           + _dot(pl_ref[0], wo_ref[ATTN_W + SSM_W:D, :]))
    x1 = x_ref[0] + _mod_rows(mod_ref, b, is_ctx, 2) * mix
    x1_ref[0] = x1
    h2 = _rmsnorm(x1, n2_ref[...])
    h2 = h2 * (1.0 + _mod_rows(mod_ref, b, is_ctx, 4)) + _mod_rows(mod_ref, b, is_ctx, 3)
    for j in range(D // LANES):
        h2_ref[pl.ds(j, TM_TOK, stride=SUBLANES), :] = h2[:, j * LANES:(j + 1) * LANES]
    logits = _dot3(h2, rw_ref[...]) + rb_ref[...]
    lane = lax.broadcasted_iota(jnp.int32, (TM_TOK, LANES), 1)
    vals, idxs = [], []
    cur = logits
    for _ in range(TOP_K):
        m = jnp.max(cur, axis=-1, keepdims=True)
        idx = jnp.min(jnp.where(cur == m, lane, LANES), axis=-1, keepdims=True)
        vals.append(m)
        idxs.append(idx)
        cur = jnp.where(lane == idx, -jnp.inf, cur)
    ex = [jnp.exp(v - vals[0]) for v in vals]
    den = ex[0] + ex[1] + ex[2] + ex[3]
    onehot = jnp.zeros((TM_TOK, LANES), F32)
    for idx in idxs:
        onehot = onehot + (lane == idx).astype(F32)
    before = _dot(tri_ref[...], onehot.astype(BF)) + run_ref[...]
    route = jnp.zeros((TM_TOK, LANES), F32)
    for k in range(TOP_K):
        rank = jnp.sum(jnp.where(lane == idxs[k], before, 0.0), axis=-1, keepdims=True)
        route = jnp.where(lane == k, idxs[k].astype(F32), route)
        route = jnp.where(lane == TOP_K + k, ex[k] / den, route)
        route = jnp.where(lane == 2 * TOP_K + k, rank, route)
    rt_ref[0] = route
    run_ref[...] = run_ref[...] + jnp.sum(onehot, axis=0, keepdims=True)
    cnt_ref[...] = jnp.broadcast_to(run_ref[...], (SUBLANES, LANES))


def _mix(xc, attn, yf, yb, u, pool, mod_l, ssm_d, glu_w, glu_b, onw_ssm, w_out, n2, rw, rb, tri):
    tok = lambda w: pl.BlockSpec((1, TM_TOK, w), lambda b, i: (b, i, 0))
    full = lambda s: pl.BlockSpec(s, lambda b, i: (0,) * len(s))
    return pl.pallas_call(
        _mix_kernel,
        out_shape=(jax.ShapeDtypeStruct((NB, LT, D), F32),
                   jax.ShapeDtypeStruct((N_TOK * SUBLANES, LANES), F32),
                   jax.ShapeDtypeStruct((NB, LT, LANES), F32),
                   jax.ShapeDtypeStruct((SUBLANES, LANES), F32)),
        grid=(NB, LT // TM_TOK),
        in_specs=[tok(D), tok(ATTN_W), tok(SSM_W), tok(SSM_W), tok(SSM_W), tok(POOL_W),
                  full((16, N_MOD * D)), full((1, SSM_W)), full((SSM_W, SSM_W)), full((1, SSM_W)),
                  full((1, SSM_W)), full((D, D)), full((1, D)), full((D, LANES)), full((1, LANES)),
                  full((TM_TOK, TM_TOK))],
        out_specs=(tok(D),
                   pl.BlockSpec((TM_TOK * SUBLANES, LANES), lambda b, i: (b * (LT // TM_TOK) + i, 0)),
                   tok(LANES), full((SUBLANES, LANES))),
        scratch_shapes=[pltpu.VMEM((1, LANES), F32)],
        compiler_params=pltpu.CompilerParams(
            dimension_semantics=("arbitrary", "arbitrary"), vmem_limit_bytes=VMEM_LIMIT),
        name="mix_router",
    )(xc, attn, yf, yb, u, pool, mod_l, ssm_d, glu_w, glu_b, onw_ssm, w_out, n2, rw, rb, tri)


def _rows_from_tiles(ref, rows):
    return jnp.concatenate(
        [ref[pl.ds(j, rows, stride=SUBLANES), :] for j in range(D // LANES)], axis=1)


def _rows_to_tiles(ref, val, rows):
    for j in range(D // LANES):
        ref[pl.ds(j, rows, stride=SUBLANES), :] = val[:, j * LANES:(j + 1) * LANES]


def _moe_kernel(te_ref, nu_ref, x_ref, wgu_ref, bgu_ref, wd_ref, bd_ref, o_ref, wgu_s, wd_s, acc_s):
    i = pl.program_id(0)

    @pl.when(i < nu_ref[0])
    def _():
        e = te_ref[i]
        prev = te_ref[jnp.maximum(i - 1, 0)]

        @pl.when((i == 0) | (e != prev))
        def _():
            def cast_gu(r, c):
                rs = pl.ds(pl.multiple_of(r * 128, 128), 128)
                wgu_s[rs, :] = wgu_ref[rs, :].astype(BF)
                return c

            def cast_d(r, c):
                rs = pl.ds(pl.multiple_of(r * 128, 128), 128)
                wd_s[rs, :] = wd_ref[rs, :].astype(BF)
                return c

            lax.fori_loop(0, D // 128, cast_gu, 0)
            lax.fori_loop(0, D_FF // 128, cast_d, 0)

        x = _rows_from_tiles(x_ref, TM_MOE).astype(BF)
        for c in range(D_FF // FF_CHUNK):
            lo = c * FF_CHUNK
            gate = _dot(x, wgu_s[:, lo:lo + FF_CHUNK]) + bgu_ref[:, lo:lo + FF_CHUNK]
            up = (_dot(x, wgu_s[:, D_FF + lo:D_FF + lo + FF_CHUNK])
                  + bgu_ref[:, D_FF + lo:D_FF + lo + FF_CHUNK])
            gate = jnp.minimum(gate, SWIGLU_LIMIT)
            up = jnp.clip(up, -SWIGLU_LIMIT, SWIGLU_LIMIT)
            act = (up + 1.0) * (gate * jax.nn.sigmoid(SWIGLU_ALPHA * gate))
            part = _dot(act.astype(BF), wd_s[lo:lo + FF_CHUNK, :])
            if c == 0:
                acc_s[...] = part + bd_ref[...]
            else:
                acc_s[...] += part
        _rows_to_tiles(o_ref, acc_s[...], TM_MOE)


def _moe(l, tile_e, n_used, xs, w_gu, b_gu, w_down, b_down):
    def tile(i, te, nu):
        return (jnp.minimum(i, nu[0] - 1), 0)

    def expert(i, te, nu):
        return te[jnp.minimum(i, nu[0] - 1)]

    return pl.pallas_call(
        _moe_kernel,
        out_shape=jax.ShapeDtypeStruct((R_MOE * SUBLANES, LANES), F32),
        grid_spec=pltpu.PrefetchScalarGridSpec(
            num_scalar_prefetch=2,
            grid=(NT_MOE,),
            in_specs=[
                pl.BlockSpec((TM_MOE * SUBLANES, LANES), tile),
                pl.BlockSpec((None, None, D, 2 * D_FF), lambda i, te, nu: (l, expert(i, te, nu), 0, 0)),
                pl.BlockSpec((None, None, 1, 2 * D_FF), lambda i, te, nu: (l, expert(i, te, nu), 0, 0)),
                pl.BlockSpec((None, None, D_FF, D), lambda i, te, nu: (l, expert(i, te, nu), 0, 0)),
                pl.BlockSpec((None, None, 1, D), lambda i, te, nu: (l, expert(i, te, nu), 0, 0)),
            ],
            out_specs=pl.BlockSpec((TM_MOE * SUBLANES, LANES), tile),
            scratch_shapes=[pltpu.VMEM((D, 2 * D_FF), BF), pltpu.VMEM((D_FF, D), BF),
                            pltpu.VMEM((TM_MOE, D), F32)],
        ),
        compiler_params=pltpu.CompilerParams(
            dimension_semantics=("arbitrary",), vmem_limit_bytes=VMEM_LIMIT),
        name="moe_experts",
    )(tile_e, n_used, xs, w_gu, b_gu.reshape(DEPTH, N_EXP, 1, 2 * D_FF), w_down,
      b_down.reshape(DEPTH, N_EXP, 1, D))


def _row_gather(table, idx):
    n_rows = idx.shape[0]
    per_worker = n_rows // SC_WORKERS
    n_chunks = per_worker // SC_CHUNK
    assert per_worker * SC_WORKERS == n_rows and n_chunks * SC_CHUNK == per_worker and n_chunks % 2 == 0
    mesh = plsc.VectorSubcoreMesh(core_axis_name="c", subcore_axis_name="s")
    row_tile = (SC_CHUNK, SUBLANES, LANES)

    @functools.partial(
        pl.kernel, mesh=mesh,
        out_type=jax.ShapeDtypeStruct((n_rows, SUBLANES, LANES), F32),
        scratch_types=[pltpu.VMEM((n_chunks, SC_CHUNK), jnp.int32),
                       pltpu.VMEM(row_tile, F32), pltpu.VMEM(row_tile, F32),
                       pltpu.SemaphoreType.DMA, pltpu.SemaphoreType.DMA],
        name="sc_row_gather",
    )
    def gather(table_hbm, idx_hbm, out_hbm, idx_v, buf0, buf1, sem0, sem1):
        wid = lax.axis_index("s") * SC_CORES + lax.axis_index("c")
        pltpu.sync_copy(idx_hbm.at[pl.ds(wid * n_chunks, n_chunks)], idx_v)
        base = wid * per_worker

        def fetch(chunk, buf, sem):
            return pltpu.make_async_copy(table_hbm.at[idx_v.at[chunk]], buf, sem)

        def emit(chunk, buf):
            pltpu.sync_copy(buf, out_hbm.at[pl.ds(base + chunk * SC_CHUNK, SC_CHUNK)])

        fetch(0, buf0, sem0).start()

        @pl.loop(0, n_chunks, step=2)
        def _(c):
            fetch(c + 1, buf1, sem1).start()
            fetch(c, buf0, sem0).wait()
            emit(c, buf0)

            @pl.when(c + 2 < n_chunks)
            def _():
                fetch(c + 2, buf0, sem0).start()

            fetch(c + 1, buf1, sem1).wait()
            emit(c + 1, buf1)

    out = gather(table.reshape(-1, SUBLANES, LANES), idx.reshape(n_rows // SC_CHUNK, SC_CHUNK))
    return out.reshape(n_rows * SUBLANES, LANES)


def _combine_kernel(x1_ref, rt_ref, mod_ref, g0_ref, g1_ref, g2_ref, g3_ref, o_ref):
    t = pl.program_id(0)
    b = t // (LT // TQ)
    is_ctx = (t % (LT // TQ)) == 0
    route = rt_ref[0]
    y = jnp.zeros((TQ, D), F32)
    for k, g_ref in enumerate((g0_ref, g1_ref, g2_ref, g3_ref)):
        y = y + route[:, TOP_K + k:TOP_K + k + 1] * _rows_from_tiles(g_ref, TQ)
    gate = jnp.where(is_ctx, mod_ref[NB:NB + 1, 5 * D:6 * D], mod_ref[pl.ds(b, 1), 5 * D:6 * D])
    o_ref[0] = x1_ref[0] + gate * y


def _combine(x1, route, mod_l, gathered):
    n_t = N_TOK // TQ
    tok = lambda w: pl.BlockSpec((1, TQ, w), lambda t: (t // (LT // TQ), t % (LT // TQ), 0))
    slot = lambda k: pl.BlockSpec((TQ * SUBLANES, LANES), lambda t: (k * n_t + t, 0))
    return pl.pallas_call(
        _combine_kernel,
        out_shape=jax.ShapeDtypeStruct((NB, LT, D), F32),
        grid=(n_t,),
        in_specs=[tok(D), tok(LANES), pl.BlockSpec((16, N_MOD * D), lambda t: (0, 0)),
                  slot(0), slot(1), slot(2), slot(3)],
        out_specs=tok(D),
        compiler_params=pltpu.CompilerParams(
            dimension_semantics=("arbitrary",), vmem_limit_bytes=VMEM_LIMIT),
        name="moe_combine",
    )(x1, route, mod_l, gathered, gathered, gathered, gathered)


def _rope_tables():
    pos = jnp.arange(L_LAT)
    row = (pos // GRID_W).astype(F32)
    col = (pos % GRID_W).astype(F32)
    inv = ROPE_BASE ** (-jnp.arange(0, 32, 2, dtype=F32) / 32)
    ang_r = row[:, None] * inv
    ang_c = col[:, None] * inv
    ang = jnp.concatenate([ang_r, ang_r, ang_c, ang_c], axis=-1)
    cos = jnp.concatenate([jnp.ones((L_CTX, HEAD_DIM), F32), jnp.cos(ang)], axis=0)
    sin = jnp.concatenate([jnp.zeros((L_CTX, HEAD_DIM), F32), jnp.sin(ang)], axis=0)
    sign = jnp.where((jnp.arange(HEAD_DIM) % 32) < 16, -1.0, 1.0).astype(F32)
    return jnp.tile(cos, (1, 2)), jnp.tile(sin * sign, (1, 2))


def _s5_tables(a_re, a_im, log_dt, b_re, b_im, c_re, c_im):
    dt = jnp.exp(log_dt)[..., None]
    mag = jnp.exp(a_re * dt)
    ar = mag * jnp.cos(a_im * dt)
    ai = mag * jnp.sin(a_im * dt)
    den = a_re * a_re + a_im * a_im
    qr = ((ar - 1) * a_re + ai * a_im) / den
    qi = (ai * a_re - (ar - 1) * a_im) / den
    bbr = qr[..., None] * b_re - qi[..., None] * b_im
    bbi = qr[..., None] * b_im + qi[..., None] * b_re
    eye = jnp.eye(SSM_GROUPS, dtype=F32)
    to_in = lambda m: jnp.einsum('dgnp,gh->dgphn', m, eye).reshape(2, SSM_W, NS)
    bm = jnp.concatenate([to_in(bbr), to_in(bbi)], axis=-1).astype(BF)
    to_out = lambda m: jnp.einsum('dgpn,gh->dgnhp', m, eye).reshape(2, NS, SSM_W)
    cm = jnp.concatenate([to_out(c_re), -to_out(c_im)], axis=1).astype(BF)
    a_tab = jnp.stack([ar.reshape(2, NS), ai.reshape(2, NS)], axis=1)
    return a_tab, bm, cm


def _interleave_perms():
    rows = NB * T_SCAN
    r = jnp.arange(rows)
    src = (r % NB) * T_SCAN + r // NB
    pin = (src[:, None] == jnp.arange(rows)[None, :]).astype(BF)
    return pin, pin.T


def kernel(x, c, ctx, c_ctx, w_mod, b_mod, norm1_w, norm2_w, w_in, q_norm_w, k_norm_w, attn_sink,
           ssm_a_re, ssm_a_im, ssm_log_dt, ssm_b_re, ssm_b_im, ssm_c_re, ssm_c_im, ssm_d, glu_w, glu_b,
           pool_w, pool_scale, out_norm_w, w_out, router_w, router_b, exp_w_gu, exp_b_gu, exp_w_down,
           exp_b_down):
    xc = jnp.concatenate([ctx, x], axis=1)
    cc = jnp.concatenate([c, c_ctx[None, :], jnp.zeros((16 - NB - 1, D), F32)], axis=0)
    mod = _adaln(cc, w_mod, b_mod)
    cos2, sin2 = _rope_tables()
    pin, pout = _interleave_perms()
    tri = (jnp.arange(TM_TOK)[:, None] > jnp.arange(TM_TOK)[None, :]).astype(BF)

    for l in range(DEPTH):
        wl = w_in[l]
        kcols = wl[:, 512:640]
        vcols = wl[:, 640:768]
        dup = lambda m: jnp.concatenate([m[:, :64], m[:, :64], m[:, 64:], m[:, 64:]], axis=1)
        w_ext = jnp.concatenate([wl[:, :512], dup(kcols), dup(vcols), wl[:, 768:]], axis=1).astype(BF)
        qw2 = jnp.tile(q_norm_w[l], 2)[None, :]
        kw2 = jnp.tile(k_norm_w[l], 2)[None, :]
        q, kd, vd, u, p = _inproj(xc, mod[l], norm1_w[l][None, :], w_ext, qw2, kw2, cos2, sin2)

        attn = _attention(attn_sink[l], q, kd, vd, out_norm_w[l][None, :ATTN_W])

        a_tab, bm, cm = _s5_tables(ssm_a_re[l], ssm_a_im[l], ssm_log_dt[l], ssm_b_re[l], ssm_b_im[l],
                                   ssm_c_re[l], ssm_c_im[l])
        yf, yb = _s5_scan(u, pin, pout, a_tab, bm, cm)

        pw_bd = jnp.einsum('gcd,gh->gchd', pool_w[l], jnp.eye(4, dtype=F32)).reshape(POOL_W, POOL_W)
        pool = _pool(p, pw_bd.astype(BF), pool_scale[l][None, :])

        rw = jnp.concatenate([router_w[l], jnp.zeros((D, LANES - N_EXP), F32)], axis=1)
        rb = jnp.concatenate([router_b[l], jnp.full((LANES - N_EXP,), -1e30, F32)])[None, :]
        x1, h2, route, cnt = _mix(xc, attn, yf, yb, u, pool, mod[l], ssm_d[l][None, :],
                                  glu_w[l].astype(BF), glu_b[l][None, :], out_norm_w[l][None, ATTN_W:],
                                  w_out[l].astype(BF), norm2_w[l][None, :], rw, rb, tri)

        route = route.reshape(N_TOK, LANES)
        top_i = route[:, 0:TOP_K].astype(jnp.int32)
        rank = route[:, 2 * TOP_K:3 * TOP_K].astype(jnp.int32)
        counts = cnt[0, :N_EXP].astype(jnp.int32)
        padded = (counts + TM_MOE - 1) // TM_MOE * TM_MOE
        pend = jnp.cumsum(padded)
        pstart = pend - padded
        dest = pstart[top_i] + rank
        tok_of = jnp.repeat(jnp.arange(N_TOK, dtype=jnp.int32), TOP_K)
        row_tok = (jnp.arange(R_MOE, dtype=jnp.int32) % N_TOK).at[dest.reshape(-1)].set(tok_of)
        n_used = (pend[-1] // TM_MOE).astype(jnp.int32)[None]
        tile_lo = jnp.arange(NT_MOE, dtype=jnp.int32) * TM_MOE
        tile_e = jnp.minimum(jnp.sum(pend[None, :] <= tile_lo[:, None], axis=1), N_EXP - 1).astype(jnp.int32)
        xs = _row_gather(h2, row_tok)

        yb_rows = _moe(l, tile_e, n_used, xs, exp_w_gu, exp_b_gu, exp_w_down, exp_b_down)
        gathered = _row_gather(yb_rows, dest.T.reshape(-1))
        xc = _combine(x1, route.reshape(NB, LT, LANES), mod[l], gathered)
    return xc[:, L_CTX:, :]
```

```python
import functools
import math

import jax
import jax.numpy as jnp
from jax import lax
from jax.experimental import pallas as pl
from jax.experimental.pallas import tpu as pltpu
from jax.experimental.pallas import tpu_sc as plsc

D = 1024
NB = 8
L_LAT = 2048
L_CTX = 256
LT = L_CTX + L_LAT
DEPTH = 4
N_MOD = 6
EPS = 1e-6
N_HEADS = 8
HEAD_DIM = 64
ATTN_W = 512
WINDOW = 128
ATTN_SCALE = HEAD_DIM ** -0.5
ROPE_BASE = 10000.0
GRID_W = 64
SSM_W = 256
SSM_GROUP = 16
SSM_GROUPS = 16
SSM_STATE = 64
NS = SSM_GROUPS * SSM_STATE
POOL_W = 256
POOL_GROUP = 64
N_EXP = 32
TOP_K = 4
D_FF = 1024
SWIGLU_LIMIT = 7.0
SWIGLU_ALPHA = 1.702

LANES = 128
SUBLANES = 8
VMEM_LIMIT = 56 * 1024 * 1024

TM_TOK = 768
TQ = 256
KSPAN = TQ + 2 * WINDOW
T_SCAN = 32
N_CHUNK = LT // T_SCAN
N_CHUNK_CTX = L_CTX // T_SCAN
TM_MOE = 512
N_TOK = NB * LT
N_ASSIGN = N_TOK * TOP_K
NT_MOE = N_ASSIGN // TM_MOE + N_EXP
R_MOE = NT_MOE * TM_MOE
FF_CHUNK = 512

SC_CORES = 2
SC_SUBCORES = 16
SC_WORKERS = SC_CORES * SC_SUBCORES
SC_CHUNK = 32

C_Q = 0
C_K = 512
C_V = 768
C_U = 1024
C_P = 1280
NW_IN = 1536

BF = jnp.bfloat16
F32 = jnp.float32


def _split_bf16(a):
    hi = a.astype(BF)
    lo = (a - hi.astype(F32)).astype(BF)
    return hi, lo


def _dot(a, b):
    return jnp.dot(a, b, preferred_element_type=F32)


def _dot3(a, b):
    ah, al = _split_bf16(a)
    bh, bl = _split_bf16(b)
    return _dot(ah, bh) + (_dot(ah, bl) + _dot(al, bh))


def _mod_kernel(c_ref, w_ref, b_ref, o_ref):
    c = c_ref[...]
    s = c * jax.nn.sigmoid(c)
    o_ref[0] = _dot3(s, w_ref[0]) + b_ref[0]


def _adaln(cc, w_mod, b_mod):
    tn = 1536
    return pl.pallas_call(
        _mod_kernel,
        out_shape=jax.ShapeDtypeStruct((DEPTH, 16, N_MOD * D), F32),
        grid=(DEPTH, N_MOD * D // tn),
        in_specs=[
            pl.BlockSpec((16, D), lambda l, j: (0, 0)),
            pl.BlockSpec((1, D, tn), lambda l, j: (l, 0, j)),
            pl.BlockSpec((1, 1, tn), lambda l, j: (l, 0, j)),
        ],
        out_specs=pl.BlockSpec((1, 16, tn), lambda l, j: (l, 0, j)),
        compiler_params=pltpu.CompilerParams(
            dimension_semantics=("arbitrary", "arbitrary"), vmem_limit_bytes=VMEM_LIMIT),
        name="adaln_mod",
    )(cc, w_mod, b_mod.reshape(DEPTH, 1, N_MOD * D))


def _mod_rows(mod_ref, b, is_ctx, idx):
    lat = mod_ref[pl.ds(b, 1), idx * D:(idx + 1) * D]
    ctx = mod_ref[NB:NB + 1, idx * D:(idx + 1) * D]
    return jnp.where(is_ctx, ctx, lat)


def _rmsnorm(x, w):
    ms = jnp.mean(x * x, axis=-1, keepdims=True)
    return x * lax.rsqrt(ms + EPS) * w


def _headnorm_pair(t, w2, lane):
    sq = t * t
    first = lane < HEAD_DIM
    a = jnp.sum(jnp.where(first, sq, 0.0), axis=-1, keepdims=True)
    b = jnp.sum(jnp.where(first, 0.0, sq), axis=-1, keepdims=True)
    ms = jnp.where(first, a, b) * (1.0 / HEAD_DIM)
    return t * lax.rsqrt(ms + EPS) * w2


def _rope_pair(t, cos, sin_signed, lane):
    lower = (lane % 32) < 16
    partner = jnp.where(lower, pltpu.roll(t, LANES - 16, 1), pltpu.roll(t, 16, 1))
    return t * cos + partner * sin_signed


def _inproj_kernel(x_ref, mod_ref, n1_ref, w_ref, qw_ref, kw_ref, cos_ref, sin_ref,
                   q_ref, k_ref, v_ref, u_ref, p_ref):
    b = pl.program_id(0)
    i = pl.program_id(1)
    x = x_ref[0]
    row = i * TM_TOK + lax.broadcasted_iota(jnp.int32, (TM_TOK, 1), 0)
    is_ctx = row < L_CTX
    h = _rmsnorm(x, n1_ref[...])
    h = h * (1.0 + _mod_rows(mod_ref, b, is_ctx, 1)) + _mod_rows(mod_ref, b, is_ctx, 0)
    px = _dot(h.astype(BF), w_ref[...])
    lane = lax.broadcasted_iota(jnp.int32, (TM_TOK, LANES), 1)
    cos = cos_ref[...]
    sin = sin_ref[...]
    for j in range(ATTN_W // LANES):
        t = px[:, C_Q + j * LANES:C_Q + (j + 1) * LANES]
        t = _rope_pair(_headnorm_pair(t, qw_ref[...], lane), cos, sin, lane) * ATTN_SCALE
        q_ref[0, :, j * LANES:(j + 1) * LANES] = t.astype(BF)
    for g in range(2):
        t = px[:, C_K + g * LANES:C_K + (g + 1) * LANES]
        t = _rope_pair(_headnorm_pair(t, kw_ref[...], lane), cos, sin, lane)
        k_ref[0, :, g * LANES:(g + 1) * LANES] = t.astype(BF)
    v_ref[0] = px[:, C_V:C_U].astype(BF)
    u_ref[0] = px[:, C_U:C_P]
    p_ref[0] = px[:, C_P:NW_IN]


def _inproj(xc, mod_l, n1, w_ext, qw2, kw2, cos2, sin2):
    tok = lambda w: pl.BlockSpec((1, TM_TOK, w), lambda b, i: (b, i, 0))
    full = lambda s: pl.BlockSpec(s, lambda b, i: (0,) * len(s))
    return pl.pallas_call(
        _inproj_kernel,
        out_shape=(
            jax.ShapeDtypeStruct((NB, LT, ATTN_W), BF),
            jax.ShapeDtypeStruct((NB, LT, 256), BF),
            jax.ShapeDtypeStruct((NB, LT, 256), BF),
            jax.ShapeDtypeStruct((NB, LT, SSM_W), F32),
            jax.ShapeDtypeStruct((NB, LT, POOL_W), F32),
        ),
        grid=(NB, LT // TM_TOK),
        in_specs=[
            tok(D), full((16, N_MOD * D)), full((1, D)), full((D, NW_IN)),
            full((1, LANES)), full((1, LANES)),
            pl.BlockSpec((TM_TOK, LANES), lambda b, i: (i, 0)),
            pl.BlockSpec((TM_TOK, LANES), lambda b, i: (i, 0)),
        ],
        out_specs=(tok(ATTN_W), tok(256), tok(256), tok(SSM_W), tok(POOL_W)),
        compiler_params=pltpu.CompilerParams(
            dimension_semantics=("arbitrary", "arbitrary"), vmem_limit_bytes=VMEM_LIMIT),
        name="inproj",
    )(xc, mod_l, n1, w_ext, qw2, kw2, cos2, sin2)


def _attn_kernel(sink_ref, q_ref, k_ref, v_ref, onw_ref, o_ref, acc_ref):
    j = pl.program_id(1)
    start = pl.multiple_of(jnp.clip(j * TQ - WINDOW, LANES, LT - KSPAN), LANES)
    lane = lax.broadcasted_iota(jnp.int32, (TQ, LANES), 1)
    first = lane < HEAD_DIM
    row2 = lax.broadcasted_iota(jnp.int32, (2 * TQ, 1), 0)
    qpos = j * TQ + jnp.where(row2 < TQ, row2, row2 - TQ)
    kpos = start + lax.broadcasted_iota(jnp.int32, (1, KSPAN), 1)
    valid = (jnp.abs(qpos - kpos) <= WINDOW) & (kpos >= L_CTX) & (j >= 1)
    for g in range(2):
        kc = k_ref[0, 0:L_CTX, g * LANES:(g + 1) * LANES]
        vc = v_ref[0, 0:L_CTX, g * LANES:(g + 1) * LANES]
        kl = k_ref[0, pl.ds(start, KSPAN), g * LANES:(g + 1) * LANES]
        vl = v_ref[0, pl.ds(start, KSPAN), g * LANES:(g + 1) * LANES]
        for hp in (2 * g, 2 * g + 1):
            q2 = q_ref[0, :, hp * LANES:(hp + 1) * LANES]
            zero = jnp.zeros_like(q2)
            qs = jnp.concatenate([jnp.where(first, q2, zero), jnp.where(first, zero, q2)], axis=0)
            dn = (((1,), (1,)), ((), ()))
            s_ctx = lax.dot_general(qs, kc, dn, preferred_element_type=F32)
            s_loc = lax.dot_general(qs, kl, dn, preferred_element_type=F32)
            s_loc = jnp.where(valid, s_loc, -1e30)
            sink = jnp.where(row2 < TQ, sink_ref[2 * hp], sink_ref[2 * hp + 1])
            m = jnp.maximum(jnp.maximum(jnp.max(s_loc, axis=-1, keepdims=True),
                                        jnp.max(s_ctx, axis=-1, keepdims=True)), sink)
            e_loc = jnp.exp(s_loc - m)
            e_ctx = jnp.exp(s_ctx - m)
            den = (jnp.sum(e_loc, axis=-1, keepdims=True) + jnp.sum(e_ctx, axis=-1, keepdims=True)
                   + jnp.exp(sink - m))
            o = _dot(e_loc.astype(BF), vl) + _dot(e_ctx.astype(BF), vc)
            o = o / den
            acc_ref[:, hp * LANES:(hp + 1) * LANES] = jnp.where(first, o[0:TQ], o[TQ:2 * TQ])
    o_ref[0] = _rmsnorm(acc_ref[...], onw_ref[...]).astype(BF)


def _attention(sink, q, kd, vd, onw_attn):
    return pl.pallas_call(
        _attn_kernel,
        out_shape=jax.ShapeDtypeStruct((NB, LT, ATTN_W), BF),
        grid_spec=pltpu.PrefetchScalarGridSpec(
            num_scalar_prefetch=1,
            grid=(NB, LT // TQ),
            in_specs=[
                pl.BlockSpec((1, TQ, ATTN_W), lambda b, j, s: (b, j, 0)),
                pl.BlockSpec((1, LT, 256), lambda b, j, s: (b, 0, 0)),
                pl.BlockSpec((1, LT, 256), lambda b, j, s: (b, 0, 0)),
                pl.BlockSpec((1, ATTN_W), lambda b, j, s: (0, 0)),
            ],
            out_specs=pl.BlockSpec((1, TQ, ATTN_W), lambda b, j, s: (b, j, 0)),
            scratch_shapes=[pltpu.VMEM((TQ, ATTN_W), F32)],
        ),
        compiler_params=pltpu.CompilerParams(
            dimension_semantics=("arbitrary", "arbitrary"), vmem_limit_bytes=VMEM_LIMIT),
        name="window_attn",
    )(sink, q, kd, vd, onw_attn)


def _bwd_chunk(i):
    return jnp.where(i < N_CHUNK_CTX, N_CHUNK_CTX - 1 - i, N_CHUNK - 1 - (i - N_CHUNK_CTX))


def _scan_kernel(uf_ref, ub_ref, pin_ref, pout_ref, a_ref, bm_ref, cm_ref, yf_ref, yb_ref,
                 xs_ref, st_ref):
    i = pl.program_id(0)
    rows = NB * T_SCAN

    @pl.when(i == 0)
    def _():
        st_ref[...] = jnp.zeros_like(st_ref)

    def one_direction(d, u_ref, y_ref, reverse):
        u = u_ref[...].reshape(rows, SSM_W).astype(BF)
        ui = _dot(pin_ref[...], u).astype(BF)
        xs_ref[...] = _dot(ui, bm_ref[d])
        ar = jnp.broadcast_to(a_ref[d, 0:1, :], (NB, NS))
        ai = jnp.broadcast_to(a_ref[d, 1:2, :], (NB, NS))
        sr = st_ref[d, 0]
        si = st_ref[d, 1]
        for t in (range(T_SCAN - 1, -1, -1) if reverse else range(T_SCAN)):
            r = pl.ds(t * NB, NB)
            nr = ar * sr - ai * si + xs_ref[r, 0:NS]
            ni = ar * si + ai * sr + xs_ref[r, NS:2 * NS]
            sr, si = nr, ni
            xs_ref[r, 0:NS] = sr
            xs_ref[r, NS:2 * NS] = si
        st_ref[d, 0] = sr
        st_ref[d, 1] = si
        y = _dot(xs_ref[...].astype(BF), cm_ref[d])
        yh, yl = _split_bf16(y)
        y = _dot(pout_ref[...], yh) + _dot(pout_ref[...], yl)
        y_ref[...] = y.reshape(NB, T_SCAN, SSM_W)

    one_direction(0, uf_ref, yf_ref, False)
    one_direction(1, ub_ref, yb_ref, True)


def _s5_scan(u, pin, pout, a_tab, bm, cm):
    rows = NB * T_SCAN
    full = lambda s: pl.BlockSpec(s, lambda i: (0,) * len(s))
    chunk_f = pl.BlockSpec((NB, T_SCAN, SSM_W), lambda i: (0, i, 0))
    chunk_b = pl.BlockSpec((NB, T_SCAN, SSM_W), lambda i: (0, _bwd_chunk(i), 0))
    return pl.pallas_call(
        _scan_kernel,
        out_shape=(jax.ShapeDtypeStruct((NB, LT, SSM_W), F32),
                   jax.ShapeDtypeStruct((NB, LT, SSM_W), F32)),
        grid=(N_CHUNK,),
        in_specs=[chunk_f, chunk_b, full((rows, rows)), full((rows, rows)),
                  full((2, 2, NS)), full((2, SSM_W, 2 * NS)), full((2, 2 * NS, SSM_W))],
        out_specs=(chunk_f, chunk_b),
        scratch_shapes=[pltpu.VMEM((rows, 2 * NS), F32), pltpu.VMEM((2, 2, NB, NS), F32)],
        compiler_params=pltpu.CompilerParams(
            dimension_semantics=("arbitrary",), vmem_limit_bytes=VMEM_LIMIT),
        name="s5_scan",
    )(u, u, pin, pout, a_tab, bm, cm)


def _pool_segment(ps, length):
    n = length + 2 * SUBLANES
    z = jnp.zeros((SUBLANES, POOL_W), F32)
    pe = jnp.concatenate([z, ps, z], axis=0)
    a1 = pe + pltpu.roll(pe, 1, 0)
    a2 = a1 + pltpu.roll(a1, 2, 0)
    a3 = a2 + pltpu.roll(a2, 4, 0)
    a4 = a3 + pltpu.roll(a3, 8, 0)
    lane = lax.broadcasted_iota(jnp.int32, (1, POOL_W), 1)
    half = jnp.where(lane < 64, 1, jnp.where(lane < 128, 2, jnp.where(lane < 192, 4, 8)))
    s = jnp.where(lane < 64, a1,
                  jnp.where(lane < 128, pltpu.roll(a2, n - 1, 0),
                            jnp.where(lane < 192, pltpu.roll(a3, n - 3, 0), pltpu.roll(a4, n - 7, 0))))
    s = s[SUBLANES:SUBLANES + length]
    t = lax.broadcasted_iota(jnp.int32, (length, 1), 0)
    cnt = jnp.minimum(t + half, length) - jnp.maximum(t - half, 0)
    return s / cnt.astype(F32) - ps


def _pool_kernel(p_ref, w_ref, sc_ref, o_ref):
    for lo, length in ((0, L_CTX), (L_CTX, L_LAT)):
        dlt = _pool_segment(p_ref[0, lo:lo + length, :], length)
        y = _dot(dlt.astype(BF), w_ref[...]) * sc_ref[...]
        o_ref[0, lo:lo + length, :] = y.astype(BF)


def _pool(p, w_bd, scale):
    return pl.pallas_call(
        _pool_kernel,
        out_shape=jax.ShapeDtypeStruct((NB, LT, POOL_W), BF),
        grid=(NB,),
        in_specs=[pl.BlockSpec((1, LT, POOL_W), lambda b: (b, 0, 0)),
                  pl.BlockSpec((POOL_W, POOL_W), lambda b: (0, 0)),
                  pl.BlockSpec((1, POOL_W), lambda b: (0, 0))],
        out_specs=pl.BlockSpec((1, LT, POOL_W), lambda b: (b, 0, 0)),
        compiler_params=pltpu.CompilerParams(
            dimension_semantics=("arbitrary",), vmem_limit_bytes=VMEM_LIMIT),
        name="pool",
    )(p, w_bd, scale)


def _mix_kernel(x_ref, at_ref, yf_ref, yb_ref, u_ref, pl_ref, mod_ref, d_ref, gw_ref, gb_ref, onw_ref,
                wo_ref, n2_ref, rw_ref, rb_ref, tri_ref,
                x1_ref, h2_ref, rt_ref, cnt_ref, run_ref):
    b = pl.program_id(0)
    i = pl.program_id(1)

    @pl.when((b == 0) & (i == 0))
    def _():
        run_ref[...] = jnp.zeros_like(run_ref)

    row = i * TM_TOK + lax.broadcasted_iota(jnp.int32, (TM_TOK, 1), 0)
    is_ctx = row < L_CTX
    y = yf_ref[0] + yb_ref[0] + d_ref[...] * u_ref[0]
    g = jax.nn.gelu(y, approximate=True)
    s = g * jax.nn.sigmoid(_dot(g.astype(BF), gw_ref[...]) + gb_ref[...])
    s = _rmsnorm(s, onw_ref[...]).astype(BF)
    mix = (_dot(at_ref[0], wo_ref[0:ATTN_W, :])
           + _dot(s, wo_ref[ATTN_W:ATTN_W + SSM_W, :])
           + _dot(pl_ref[0], wo_ref[ATTN_W + SSM_W:D, :]))
    x1 = x_ref[0] + _mod_rows(mod_ref, b, is_ctx, 2) * mix
    x1_ref[0] = x1
    h2 = _rmsnorm(x1, n2_ref[...])
    h2 = h2 * (1.0 + _mod_rows(mod_ref, b, is_ctx, 4)) + _mod_rows(mod_ref, b, is_ctx, 3)
    for j in range(D // LANES):
        h2_ref[pl.ds(j, TM_TOK, stride=SUBLANES), :] = h2[:, j * LANES:(j + 1) * LANES]
    logits = _dot3(h2, rw_ref[...]) + rb_ref[...]
    lane = lax.broadcasted_iota(jnp.int32, (TM_TOK, LANES), 1)
    vals, idxs = [], []
    cur = logits
    for _ in range(TOP_K):
        m = jnp.max(cur, axis=-1, keepdims=True)
        idx = jnp.min(jnp.where(cur == m, lane, LANES), axis=-1, keepdims=True)
        vals.append(m)
        idxs.append(idx)
        cur = jnp.where(lane == idx, -jnp.inf, cur)
    ex = [jnp.exp(v - vals[0]) for v in vals]
    den = ex[0] + ex[1] + ex[2] + ex[3]
    onehot = jnp.zeros((TM_TOK, LANES), F32)
    for idx in idxs:
        onehot = onehot + (lane == idx).astype(F32)
    before = _dot(tri_ref[...], onehot.astype(BF)) + run_ref[...]
    route = jnp.zeros((TM_TOK, LANES), F32)
    for k in range(TOP_K):
        rank = jnp.sum(jnp.where(lane == idxs[k], before, 0.0), axis=-1, keepdims=True)
        route = jnp.where(lane == k, idxs[k].astype(F32), route)
        route = jnp.where(lane == TOP_K + k, ex[k] / den, route)
        route = jnp.where(lane == 2 * TOP_K + k, rank, route)
    rt_ref[0] = route
    run_ref[...] = run_ref[...] + jnp.sum(onehot, axis=0, keepdims=True)
    cnt_ref[...] = jnp.broadcast_to(run_ref[...], (SUBLANES, LANES))


def _mix(xc, attn, yf, yb, u, pool, mod_l, ssm_d, glu_w, glu_b, onw_ssm, w_out, n2, rw, rb, tri):
    tok = lambda w: pl.BlockSpec((1, TM_TOK, w), lambda b, i: (b, i, 0))
    full = lambda s: pl.BlockSpec(s, lambda b, i: (0,) * len(s))
    return pl.pallas_call(
        _mix_kernel,
        out_shape=(jax.ShapeDtypeStruct((NB, LT, D), F32),
                   jax.ShapeDtypeStruct((N_TOK * SUBLANES, LANES), F32),
                   jax.ShapeDtypeStruct((NB, LT, LANES), F32),
                   jax.ShapeDtypeStruct((SUBLANES, LANES), F32)),
        grid=(NB, LT // TM_TOK),
        in_specs=[tok(D), tok(ATTN_W), tok(SSM_W), tok(SSM_W), tok(SSM_W), tok(POOL_W),
                  full((16, N_MOD * D)), full((1, SSM_W)), full((SSM_W, SSM_W)), full((1, SSM_W)),
                  full((1, SSM_W)), full((D, D)), full((1, D)), full((D, LANES)), full((1, LANES)),
                  full((TM_TOK, TM_TOK))],
        out_specs=(tok(D),
                   pl.BlockSpec((TM_TOK * SUBLANES, LANES), lambda b, i: (b * (LT // TM_TOK) + i, 0)),
                   tok(LANES), full((SUBLANES, LANES))),
        scratch_shapes=[pltpu.VMEM((1, LANES), F32)],
        compiler_params=pltpu.CompilerParams(
            dimension_semantics=("arbitrary", "arbitrary"), vmem_limit_bytes=VMEM_LIMIT),
        name="mix_router",
    )(xc, attn, yf, yb, u, pool, mod_l, ssm_d, glu_w, glu_b, onw_ssm, w_out, n2, rw, rb, tri)


def _rows_from_tiles(ref, rows):
    return jnp.concatenate(
        [ref[pl.ds(j, rows, stride=SUBLANES), :] for j in range(D // LANES)], axis=1)


def _rows_to_tiles(ref, val, rows):
    for j in range(D // LANES):
        ref[pl.ds(j, rows, stride=SUBLANES), :] = val[:, j * LANES:(j + 1) * LANES]


def _moe_kernel(te_ref, nu_ref, x_ref, wgu_ref, bgu_ref, wd_ref, bd_ref, o_ref, wgu_s, wd_s, acc_s):
    i = pl.program_id(0)

    @pl.when(i < nu_ref[0])
    def _():
        e = te_ref[i]
        prev = te_ref[jnp.maximum(i - 1, 0)]

        @pl.when((i == 0) | (e != prev))
        def _():
            def cast_gu(r, c):
                rs = pl.ds(pl.multiple_of(r * 128, 128), 128)
                wgu_s[rs, :] = wgu_ref[rs, :].astype(BF)
                return c

            def cast_d(r, c):
                rs = pl.ds(pl.multiple_of(r * 128, 128), 128)
                wd_s[rs, :] = wd_ref[rs, :].astype(BF)
                return c

            lax.fori_loop(0, D // 128, cast_gu, 0)
            lax.fori_loop(0, D_FF // 128, cast_d, 0)

        x = _rows_from_tiles(x_ref, TM_MOE).astype(BF)
        for c in range(D_FF // FF_CHUNK):
            lo = c * FF_CHUNK
            gate = _dot(x, wgu_s[:, lo:lo + FF_CHUNK]) + bgu_ref[:, lo:lo + FF_CHUNK]
            up = (_dot(x, wgu_s[:, D_FF + lo:D_FF + lo + FF_CHUNK])
                  + bgu_ref[:, D_FF + lo:D_FF + lo + FF_CHUNK])
            gate = jnp.minimum(gate, SWIGLU_LIMIT)
            up = jnp.clip(up, -SWIGLU_LIMIT, SWIGLU_LIMIT)
            act = (up + 1.0) * (gate * jax.nn.sigmoid(SWIGLU_ALPHA * gate))
            part = _dot(act.astype(BF), wd_s[lo:lo + FF_CHUNK, :])
            if c == 0:
                acc_s[...] = part + bd_ref[...]
            else:
                acc_s[...] += part
        _rows_to_tiles(o_ref, acc_s[...], TM_MOE)


def _moe(l, tile_e, n_used, xs, w_gu, b_gu, w_down, b_down):
    def tile(i, te, nu):
        return (jnp.minimum(i, nu[0] - 1), 0)

    def expert(i, te, nu):
        return te[jnp.minimum(i, nu[0] - 1)]

    return pl.pallas_call(
        _moe_kernel,
        out_shape=jax.ShapeDtypeStruct((R_MOE * SUBLANES, LANES), F32),
        grid_spec=pltpu.PrefetchScalarGridSpec(
            num_scalar_prefetch=2,
            grid=(NT_MOE,),
            in_specs=[
                pl.BlockSpec((TM_MOE * SUBLANES, LANES), tile),
                pl.BlockSpec((None, None, D, 2 * D_FF), lambda i, te, nu: (l, expert(i, te, nu), 0, 0)),
                pl.BlockSpec((None, None, 1, 2 * D_FF), lambda i, te, nu: (l, expert(i, te, nu), 0, 0)),
                pl.BlockSpec((None, None, D_FF, D), lambda i, te, nu: (l, expert(i, te, nu), 0, 0)),
                pl.BlockSpec((None, None, 1, D), lambda i, te, nu: (l, expert(i, te, nu), 0, 0)),
            ],
            out_specs=pl.BlockSpec((TM_MOE * SUBLANES, LANES), tile),
            scratch_shapes=[pltpu.VMEM((D, 2 * D_FF), BF), pltpu.VMEM((D_FF, D), BF),
                            pltpu.VMEM((TM_MOE, D), F32)],
        ),
        compiler_params=pltpu.CompilerParams(
            dimension_semantics=("arbitrary",), vmem_limit_bytes=VMEM_LIMIT),
        name="moe_experts",
    )(tile_e, n_used, xs, w_gu, b_gu.reshape(DEPTH, N_EXP, 1, 2 * D_FF), w_down,
      b_down.reshape(DEPTH, N_EXP, 1, D))


def _row_gather(table, idx):
    n_rows = idx.shape[0]
    per_worker = n_rows // SC_WORKERS
    n_chunks = per_worker // SC_CHUNK
    assert per_worker * SC_WORKERS == n_rows and n_chunks * SC_CHUNK == per_worker and n_chunks % 2 == 0
    mesh = plsc.VectorSubcoreMesh(core_axis_name="c", subcore_axis_name="s")
    row_tile = (SC_CHUNK, SUBLANES, LANES)

    @functools.partial(
        pl.kernel, mesh=mesh,
        out_type=jax.ShapeDtypeStruct((n_rows, SUBLANES, LANES), F32),
        scratch_types=[pltpu.VMEM((n_chunks, SC_CHUNK), jnp.int32),
                       pltpu.VMEM(row_tile, F32), pltpu.VMEM(row_tile, F32),
                       pltpu.SemaphoreType.DMA, pltpu.SemaphoreType.DMA],
        name="sc_row_gather",
    )
    def gather(table_hbm, idx_hbm, out_hbm, idx_v, buf0, buf1, sem0, sem1):
        wid = lax.axis_index("s") * SC_CORES + lax.axis_index("c")
        pltpu.sync_copy(idx_hbm.at[pl.ds(wid * n_chunks, n_chunks)], idx_v)
        base = wid * per_worker

        def fetch(chunk, buf, sem):
            return pltpu.make_async_copy(table_hbm.at[idx_v.at[chunk]], buf, sem)

        def emit(chunk, buf):
            pltpu.sync_copy(buf, out_hbm.at[pl.ds(base + chunk * SC_CHUNK, SC_CHUNK)])

        fetch(0, buf0, sem0).start()

        @pl.loop(0, n_chunks, step=2)
        def _(c):
            fetch(c + 1, buf1, sem1).start()
            fetch(c, buf0, sem0).wait()
            emit(c, buf0)

            @pl.when(c + 2 < n_chunks)
            def _():
                fetch(c + 2, buf0, sem0).start()

            fetch(c + 1, buf1, sem1).wait()
            emit(c + 1, buf1)

    out = gather(table.reshape(-1, SUBLANES, LANES), idx.reshape(n_rows // SC_CHUNK, SC_CHUNK))
    return out.reshape(n_rows * SUBLANES, LANES)


def _row_dispatch(rows, dest_t, n_out):
    per_worker = N_TOK // SC_WORKERS
    n_chunks = per_worker // SC_CHUNK
    assert per_worker * SC_WORKERS == N_TOK and n_chunks * SC_CHUNK == per_worker and n_chunks % 2 == 0
    idx_rows = TOP_K * n_chunks
    assert idx_rows % SUBLANES == 0
    mesh = plsc.VectorSubcoreMesh(core_axis_name="c", subcore_axis_name="s")
    row_tile = (SC_CHUNK, SUBLANES, LANES)

    @functools.partial(
        pl.kernel, mesh=mesh,
        out_type=jax.ShapeDtypeStruct((n_out, SUBLANES, LANES), F32),
        scratch_types=[pltpu.VMEM((idx_rows, SC_CHUNK), jnp.int32),
                       pltpu.VMEM(row_tile, F32), pltpu.VMEM(row_tile, F32),
                       pltpu.SemaphoreType.DMA, pltpu.SemaphoreType.DMA],
        name="sc_row_dispatch",
    )
    def dispatch(rows_hbm, idx_hbm, out_hbm, idx_v, buf0, buf1, sem0, sem1):
        wid = lax.axis_index("s") * SC_CORES + lax.axis_index("c")
        pltpu.sync_copy(idx_hbm.at[pl.ds(wid * idx_rows, idx_rows)], idx_v)
        base = wid * per_worker

        def fetch(chunk, buf, sem):
            return pltpu.make_async_copy(rows_hbm.at[pl.ds(base + chunk * SC_CHUNK, SC_CHUNK)], buf, sem)

        def emit(chunk, buf):
            for k in range(TOP_K):
                pltpu.sync_copy(buf, out_hbm.at[idx_v.at[k * n_chunks + chunk]])

        fetch(0, buf0, sem0).start()

        @pl.loop(0, n_chunks, step=2)
        def _(c):
            fetch(c + 1, buf1, sem1).start()
            fetch(c, buf0, sem0).wait()
            emit(c, buf0)

            @pl.when(c + 2 < n_chunks)
            def _():
                fetch(c + 2, buf0, sem0).start()

            fetch(c + 1, buf1, sem1).wait()
            emit(c + 1, buf1)

    idx = dest_t.reshape(TOP_K, SC_WORKERS, n_chunks, SC_CHUNK).transpose(1, 0, 2, 3)
    out = dispatch(rows.reshape(-1, SUBLANES, LANES), idx.reshape(SC_WORKERS * idx_rows, SC_CHUNK))
    return out.reshape(n_out * SUBLANES, LANES)


def _combine_kernel(x1_ref, rt_ref, mod_ref, g0_ref, g1_ref, g2_ref, g3_ref, o_ref):
    t = pl.program_id(0)
    b = t // (LT // TQ)
    is_ctx = (t % (LT // TQ)) == 0
    route = rt_ref[0]
    y = jnp.zeros((TQ, D), F32)
    for k, g_ref in enumerate((g0_ref, g1_ref, g2_ref, g3_ref)):
        y = y + route[:, TOP_K + k:TOP_K + k + 1] * _rows_from_tiles(g_ref, TQ)
    gate = jnp.where(is_ctx, mod_ref[NB:NB + 1, 5 * D:6 * D], mod_ref[pl.ds(b, 1), 5 * D:6 * D])
    o_ref[0] = x1_ref[0] + gate * y


def _combine(x1, route, mod_l, gathered):
    n_t = N_TOK // TQ
    tok = lambda w: pl.BlockSpec((1, TQ, w), lambda t: (t // (LT // TQ), t % (LT // TQ), 0))
    slot = lambda k: pl.BlockSpec((TQ * SUBLANES, LANES), lambda t: (k * n_t + t, 0))
    return pl.pallas_call(
        _combine_kernel,
        out_shape=jax.ShapeDtypeStruct((NB, LT, D), F32),
        grid=(n_t,),
        in_specs=[tok(D), tok(LANES), pl.BlockSpec((16, N_MOD * D), lambda t: (0, 0)),
                  slot(0), slot(1), slot(2), slot(3)],
        out_specs=tok(D),
        compiler_params=pltpu.CompilerParams(
            dimension_semantics=("arbitrary",), vmem_limit_bytes=VMEM_LIMIT),
        name="moe_combine",
    )(x1, route, mod_l, gathered, gathered, gathered, gathered)


def _rope_tables():
    pos = jnp.arange(L_LAT)
    row = (pos // GRID_W).astype(F32)
    col = (pos % GRID_W).astype(F32)
    inv = ROPE_BASE ** (-jnp.arange(0, 32, 2, dtype=F32) / 32)
    ang_r = row[:, None] * inv
    ang_c = col[:, None] * inv
    ang = jnp.concatenate([ang_r, ang_r, ang_c, ang_c], axis=-1)
    cos = jnp.concatenate([jnp.ones((L_CTX, HEAD_DIM), F32), jnp.cos(ang)], axis=0)
    sin = jnp.concatenate([jnp.zeros((L_CTX, HEAD_DIM), F32), jnp.sin(ang)], axis=0)
    sign = jnp.where((jnp.arange(HEAD_DIM) % 32) < 16, -1.0, 1.0).astype(F32)
    return jnp.tile(cos, (1, 2)), jnp.tile(sin * sign, (1, 2))


def _s5_tables(a_re, a_im, log_dt, b_re, b_im, c_re, c_im):
    dt = jnp.exp(log_dt)[..., None]
    mag = jnp.exp(a_re * dt)
    ar = mag * jnp.cos(a_im * dt)
    ai = mag * jnp.sin(a_im * dt)
    den = a_re * a_re + a_im * a_im
    qr = ((ar - 1) * a_re + ai * a_im) / den
    qi = (ai * a_re - (ar - 1) * a_im) / den
    bbr = qr[..., None] * b_re - qi[..., None] * b_im
    bbi = qr[..., None] * b_im + qi[..., None] * b_re
    eye = jnp.eye(SSM_GROUPS, dtype=F32)
    to_in = lambda m: jnp.einsum('dgnp,gh->dgphn', m, eye).reshape(2, SSM_W, NS)
    bm = jnp.concatenate([to_in(bbr), to_in(bbi)], axis=-1).astype(BF)
    to_out = lambda m: jnp.einsum('dgpn,gh->dgnhp', m, eye).reshape(2, NS, SSM_W)
    cm = jnp.concatenate([to_out(c_re), -to_out(c_im)], axis=1).astype(BF)
    a_tab = jnp.stack([ar.reshape(2, NS), ai.reshape(2, NS)], axis=1)
    return a_tab, bm, cm


def _interleave_perms():
    rows = NB * T_SCAN
    r = jnp.arange(rows)
    src = (r % NB) * T_SCAN + r // NB
    pin = (src[:, None] == jnp.arange(rows)[None, :]).astype(BF)
    return pin, pin.T


def kernel(x, c, ctx, c_ctx, w_mod, b_mod, norm1_w, norm2_w, w_in, q_norm_w, k_norm_w, attn_sink,
           ssm_a_re, ssm_a_im, ssm_log_dt, ssm_b_re, ssm_b_im, ssm_c_re, ssm_c_im, ssm_d, glu_w, glu_b,
           pool_w, pool_scale, out_norm_w, w_out, router_w, router_b, exp_w_gu, exp_b_gu, exp_w_down,
           exp_b_down):
    xc = jnp.concatenate([ctx, x], axis=1)
    cc = jnp.concatenate([c, c_ctx[None, :], jnp.zeros((16 - NB - 1, D), F32)], axis=0)
    mod = _adaln(cc, w_mod, b_mod)
    cos2, sin2 = _rope_tables()
    pin, pout = _interleave_perms()
    tri = (jnp.arange(TM_TOK)[:, None] > jnp.arange(TM_TOK)[None, :]).astype(BF)

    for l in range(DEPTH):
        wl = w_in[l]
        kcols = wl[:, 512:640]
        vcols = wl[:, 640:768]
        dup = lambda m: jnp.concatenate([m[:, :64], m[:, :64], m[:, 64:], m[:, 64:]], axis=1)
        w_ext = jnp.concatenate([wl[:, :512], dup(kcols), dup(vcols), wl[:, 768:]], axis=1).astype(BF)
        qw2 = jnp.tile(q_norm_w[l], 2)[None, :]
        kw2 = jnp.tile(k_norm_w[l], 2)[None, :]
        q, kd, vd, u, p = _inproj(xc, mod[l], norm1_w[l][None, :], w_ext, qw2, kw2, cos2, sin2)

        attn = _attention(attn_sink[l], q, kd, vd, out_norm_w[l][None, :ATTN_W])

        a_tab, bm, cm = _s5_tables(ssm_a_re[l], ssm_a_im[l], ssm_log_dt[l], ssm_b_re[l], ssm_b_im[l],
                                   ssm_c_re[l], ssm_c_im[l])
        yf, yb = _s5_scan(u, pin, pout, a_tab, bm, cm)

        pw_bd = jnp.einsum('gcd,gh->gchd', pool_w[l], jnp.eye(4, dtype=F32)).reshape(POOL_W, POOL_W)
        pool = _pool(p, pw_bd.astype(BF), pool_scale[l][None, :])

        rw = jnp.concatenate([router_w[l], jnp.zeros((D, LANES - N_EXP), F32)], axis=1)
        rb = jnp.concatenate([router_b[l], jnp.full((LANES - N_EXP,), -1e30, F32)])[None, :]
        x1, h2, route, cnt = _mix(xc, attn, yf, yb, u, pool, mod[l], ssm_d[l][None, :],
                                  glu_w[l].astype(BF), glu_b[l][None, :], out_norm_w[l][None, ATTN_W:],
                                  w_out[l].astype(BF), norm2_w[l][None, :], rw, rb, tri)

        route = route.reshape(N_TOK, LANES)
        top_i = route[:, 0:TOP_K].astype(jnp.int32)
        rank = route[:, 2 * TOP_K:3 * TOP_K].astype(jnp.int32)
        counts = cnt[0, :N_EXP].astype(jnp.int32)
        padded = (counts + TM_MOE - 1) // TM_MOE * TM_MOE
        pend = jnp.cumsum(padded)
        pstart = pend - padded
        dest = pstart[top_i] + rank
        dest_t = dest.T
        n_used = (pend[-1] // TM_MOE).astype(jnp.int32)[None]
        tile_lo = jnp.arange(NT_MOE, dtype=jnp.int32) * TM_MOE
        tile_e = jnp.minimum(jnp.sum(pend[None, :] <= tile_lo[:, None], axis=1), N_EXP - 1).astype(jnp.int32)
        xs = _row_dispatch(h2, dest_t, R_MOE)

        yb_rows = _moe(l, tile_e, n_used, xs, exp_w_gu, exp_b_gu, exp_w_down, exp_b_down)
        gathered = _row_gather(yb_rows, dest_t.reshape(-1))
        xc = _combine(x1, route.reshape(NB, LT, LANES), mod[l], gathered)
    return xc[:, L_CTX:, :]
```

```python
import functools
import math

import jax
import jax.numpy as jnp
from jax import lax
from jax.experimental import pallas as pl
from jax.experimental.pallas import tpu as pltpu
from jax.experimental.pallas import tpu_sc as plsc

D = 1024
NB = 8
L_LAT = 2048
L_CTX = 256
LT = L_CTX + L_LAT
DEPTH = 4
N_MOD = 6
EPS = 1e-6
N_HEADS = 8
HEAD_DIM = 64
ATTN_W = 512
WINDOW = 128
ATTN_SCALE = HEAD_DIM ** -0.5
ROPE_BASE = 10000.0
GRID_W = 64
SSM_W = 256
SSM_GROUP = 16
SSM_GROUPS = 16
SSM_STATE = 64
NS = SSM_GROUPS * SSM_STATE
POOL_W = 256
POOL_GROUP = 64
N_EXP = 32
TOP_K = 4
D_FF = 1024
SWIGLU_LIMIT = 7.0
SWIGLU_ALPHA = 1.702

LANES = 128
SUBLANES = 8
VMEM_LIMIT = 56 * 1024 * 1024

TM_TOK = 768
TQ = 256
KSPAN = TQ + 2 * WINDOW
T_SCAN = 128
T_PERM = 32
N_SUB = T_SCAN // T_PERM
N_CHUNK = LT // T_SCAN
N_CHUNK_CTX = L_CTX // T_SCAN
TM_MOE = 512
N_TOK = NB * LT
N_ASSIGN = N_TOK * TOP_K
NT_MOE = N_ASSIGN // TM_MOE + N_EXP
R_MOE = NT_MOE * TM_MOE
FF_CHUNK = 512
OUT_BLOCK = 256

SC_CORES = 2
SC_SUBCORES = 16
SC_WORKERS = SC_CORES * SC_SUBCORES
SC_CHUNK = 32

C_Q = 0
C_K = 512
C_V = 768
C_U = 1024
C_P = 1280
NW_IN = 1536

BF = jnp.bfloat16
F32 = jnp.float32


def _split_bf16(a):
    hi = a.astype(BF)
    lo = (a - hi.astype(F32)).astype(BF)
    return hi, lo


def _dot(a, b):
    return jnp.dot(a, b, preferred_element_type=F32)


def _dot3(a, b):
    ah, al = _split_bf16(a)
    bh, bl = _split_bf16(b)
    return _dot(ah, bh) + (_dot(ah, bl) + _dot(al, bh))


def _mod_kernel(c_ref, w_ref, b_ref, o_ref):
    c = c_ref[...]
    s = c * jax.nn.sigmoid(c)
    o_ref[0] = _dot3(s, w_ref[0]) + b_ref[0]


def _adaln(cc, w_mod, b_mod):
    tn = 1536
    return pl.pallas_call(
        _mod_kernel,
        out_shape=jax.ShapeDtypeStruct((DEPTH, 16, N_MOD * D), F32),
        grid=(DEPTH, N_MOD * D // tn),
        in_specs=[
            pl.BlockSpec((16, D), lambda l, j: (0, 0)),
            pl.BlockSpec((1, D, tn), lambda l, j: (l, 0, j)),
            pl.BlockSpec((1, 1, tn), lambda l, j: (l, 0, j)),
        ],
        out_specs=pl.BlockSpec((1, 16, tn), lambda l, j: (l, 0, j)),
        compiler_params=pltpu.CompilerParams(
            dimension_semantics=("arbitrary", "arbitrary"), vmem_limit_bytes=VMEM_LIMIT),
        name="adaln_mod",
    )(cc, w_mod, b_mod.reshape(DEPTH, 1, N_MOD * D))


def _layer_spec(l, shape, col_block=0):
    shape = tuple(shape)
    return pl.BlockSpec((None,) + shape, lambda *_: (l,) + (0,) * (len(shape) - 1) + (col_block,))


def _mod_rows(mod_ref, b, is_ctx, idx):
    lat = mod_ref[pl.ds(b, 1), idx * D:(idx + 1) * D]
    ctx = mod_ref[NB:NB + 1, idx * D:(idx + 1) * D]
    return jnp.where(is_ctx, ctx, lat)


def _rmsnorm(x, w):
    ms = jnp.mean(x * x, axis=-1, keepdims=True)
    return x * lax.rsqrt(ms + EPS) * w


def _headnorm_pair(t, w2, lane):
    sq = t * t
    first = lane < HEAD_DIM
    a = jnp.sum(jnp.where(first, sq, 0.0), axis=-1, keepdims=True)
    b = jnp.sum(jnp.where(first, 0.0, sq), axis=-1, keepdims=True)
    ms = jnp.where(first, a, b) * (1.0 / HEAD_DIM)
    return t * lax.rsqrt(ms + EPS) * w2


def _rope_pair(t, cos, sin_signed, lane):
    lower = (lane % 32) < 16
    partner = jnp.where(lower, pltpu.roll(t, LANES - 16, 1), pltpu.roll(t, 16, 1))
    return t * cos + partner * sin_signed


def _inproj_kernel(x_ref, mod_ref, n1_ref, w_ref, qw_ref, kw_ref, cos_ref, sin_ref,
                   q_ref, k_ref, v_ref, u_ref, p_ref):
    b = pl.program_id(0)
    i = pl.program_id(1)
    x = x_ref[0]
    row = i * TM_TOK + lax.broadcasted_iota(jnp.int32, (TM_TOK, 1), 0)
    is_ctx = row < L_CTX
    h = _rmsnorm(x, n1_ref[...])
    h = h * (1.0 + _mod_rows(mod_ref, b, is_ctx, 1)) + _mod_rows(mod_ref, b, is_ctx, 0)
    px = _dot(h.astype(BF), w_ref[...])
    lane = lax.broadcasted_iota(jnp.int32, (TM_TOK, LANES), 1)
    cos = cos_ref[...]
    sin = sin_ref[...]
    for j in range(ATTN_W // LANES):
        t = px[:, C_Q + j * LANES:C_Q + (j + 1) * LANES]
        t = _rope_pair(_headnorm_pair(t, qw_ref[...], lane), cos, sin, lane) * ATTN_SCALE
        q_ref[0, :, j * LANES:(j + 1) * LANES] = t.astype(BF)
    for g in range(2):
        t = px[:, C_K + g * LANES:C_K + (g + 1) * LANES]
        t = _rope_pair(_headnorm_pair(t, kw_ref[...], lane), cos, sin, lane)
        k_ref[0, :, g * LANES:(g + 1) * LANES] = t.astype(BF)
    v_ref[0] = px[:, C_V:C_U].astype(BF)
    u_ref[0] = px[:, C_U:C_P]
    p_ref[0] = px[:, C_P:NW_IN]


def _inproj(l, xc, mod, n1, w_ext, qw2, kw2, cos2, sin2):
    tok = lambda w: pl.BlockSpec((1, TM_TOK, w), lambda b, i: (b, i, 0))
    full = lambda s: _layer_spec(l, s)
    return pl.pallas_call(
        _inproj_kernel,
        out_shape=(
            jax.ShapeDtypeStruct((NB, LT, ATTN_W), BF),
            jax.ShapeDtypeStruct((NB, LT, 256), BF),
            jax.ShapeDtypeStruct((NB, LT, 256), BF),
            jax.ShapeDtypeStruct((NB, LT, SSM_W), F32),
            jax.ShapeDtypeStruct((NB, LT, POOL_W), F32),
        ),
        grid=(NB, LT // TM_TOK),
        in_specs=[
            tok(D), full((16, N_MOD * D)), full((1, D)), full((D, NW_IN)),
            full((1, LANES)), full((1, LANES)),
            pl.BlockSpec((TM_TOK, LANES), lambda b, i: (i, 0)),
            pl.BlockSpec((TM_TOK, LANES), lambda b, i: (i, 0)),
        ],
        out_specs=(tok(ATTN_W), tok(256), tok(256), tok(SSM_W), tok(POOL_W)),
        compiler_params=pltpu.CompilerParams(
            dimension_semantics=("arbitrary", "arbitrary"), vmem_limit_bytes=VMEM_LIMIT),
        name="inproj",
    )(xc, mod, n1, w_ext, qw2, kw2, cos2, sin2)


def _attn_kernel(l, sink_ref, q_ref, k_ref, v_ref, onw_ref, o_ref, acc_ref):
    j = pl.program_id(1)
    start = pl.multiple_of(jnp.clip(j * TQ - WINDOW, LANES, LT - KSPAN), LANES)
    lane = lax.broadcasted_iota(jnp.int32, (TQ, LANES), 1)
    first = lane < HEAD_DIM
    row2 = lax.broadcasted_iota(jnp.int32, (2 * TQ, 1), 0)
    qpos = j * TQ + jnp.where(row2 < TQ, row2, row2 - TQ)
    kpos = start + lax.broadcasted_iota(jnp.int32, (1, KSPAN), 1)
    valid = (jnp.abs(qpos - kpos) <= WINDOW) & (kpos >= L_CTX) & (j >= 1)
    for g in range(2):
        kc = k_ref[0, 0:L_CTX, g * LANES:(g + 1) * LANES]
        vc = v_ref[0, 0:L_CTX, g * LANES:(g + 1) * LANES]
        kl = k_ref[0, pl.ds(start, KSPAN), g * LANES:(g + 1) * LANES]
        vl = v_ref[0, pl.ds(start, KSPAN), g * LANES:(g + 1) * LANES]
        for hp in (2 * g, 2 * g + 1):
            q2 = q_ref[0, :, hp * LANES:(hp + 1) * LANES]
            zero = jnp.zeros_like(q2)
            qs = jnp.concatenate([jnp.where(first, q2, zero), jnp.where(first, zero, q2)], axis=0)
            dn = (((1,), (1,)), ((), ()))
            s_ctx = lax.dot_general(qs, kc, dn, preferred_element_type=F32)
            s_loc = lax.dot_general(qs, kl, dn, preferred_element_type=F32)
            s_loc = jnp.where(valid, s_loc, -1e30)
            sink = jnp.where(row2 < TQ, sink_ref[l * N_HEADS + 2 * hp], sink_ref[l * N_HEADS + 2 * hp + 1])
            m = jnp.maximum(jnp.maximum(jnp.max(s_loc, axis=-1, keepdims=True),
                                        jnp.max(s_ctx, axis=-1, keepdims=True)), sink)
            e_loc = jnp.exp(s_loc - m)
            e_ctx = jnp.exp(s_ctx - m)
            den = (jnp.sum(e_loc, axis=-1, keepdims=True) + jnp.sum(e_ctx, axis=-1, keepdims=True)
                   + jnp.exp(sink - m))
            o = _dot(e_loc.astype(BF), vl) + _dot(e_ctx.astype(BF), vc)
            o = o / den
            acc_ref[:, hp * LANES:(hp + 1) * LANES] = jnp.where(first, o[0:TQ], o[TQ:2 * TQ])
    o_ref[0] = _rmsnorm(acc_ref[...], onw_ref[...]).astype(BF)


def _attention(l, sink, q, kd, vd, onw):
    return pl.pallas_call(
        functools.partial(_attn_kernel, l),
        out_shape=jax.ShapeDtypeStruct((NB, LT, ATTN_W), BF),
        grid_spec=pltpu.PrefetchScalarGridSpec(
            num_scalar_prefetch=1,
            grid=(NB, LT // TQ),
            in_specs=[
                pl.BlockSpec((1, TQ, ATTN_W), lambda b, j, s: (b, j, 0)),
                pl.BlockSpec((1, LT, 256), lambda b, j, s: (b, 0, 0)),
                pl.BlockSpec((1, LT, 256), lambda b, j, s: (b, 0, 0)),
                _layer_spec(l, (1, ATTN_W)),
            ],
            out_specs=pl.BlockSpec((1, TQ, ATTN_W), lambda b, j, s: (b, j, 0)),
            scratch_shapes=[pltpu.VMEM((TQ, ATTN_W), F32)],
        ),
        compiler_params=pltpu.CompilerParams(
            dimension_semantics=("arbitrary", "arbitrary"), vmem_limit_bytes=VMEM_LIMIT),
        name="window_attn",
    )(sink, q, kd, vd, onw)


def _bwd_chunk(i):
    return jnp.where(i < N_CHUNK_CTX, N_CHUNK_CTX - 1 - i, N_CHUNK - 1 - (i - N_CHUNK_CTX))


def _scan_kernel(uf_ref, ub_ref, pin_ref, pout_ref, a_ref, bm_ref, cm_ref, yf_ref, yb_ref,
                 xf_ref, xb_ref, st_ref):
    i = pl.program_id(0)
    sub_rows = NB * T_PERM

    @pl.when(i == 0)
    def _():
        st_ref[...] = jnp.zeros_like(st_ref)

    def project(d, u_ref, xs_ref):
        ui = jnp.concatenate(
            [_dot(pin_ref[...], u_ref[:, s * T_PERM:(s + 1) * T_PERM, :].reshape(sub_rows, SSM_W).astype(BF))
             for s in range(N_SUB)], axis=0).astype(BF)
        xs_ref[...] = _dot(ui, bm_ref[d])

    def scan(d, xs_ref, reverse):
        ar = jnp.broadcast_to(a_ref[d, 0:1, :], (NB, NS))
        ai = jnp.broadcast_to(a_ref[d, 1:2, :], (NB, NS))
        sr = st_ref[d, 0]
        si = st_ref[d, 1]
        for t in (range(T_SCAN - 1, -1, -1) if reverse else range(T_SCAN)):
            r = pl.ds(t * NB, NB)
            nr = ar * sr - ai * si + xs_ref[r, 0:NS]
            ni = ar * si + ai * sr + xs_ref[r, NS:2 * NS]
            sr, si = nr, ni
            xs_ref[r, 0:NS] = sr
            xs_ref[r, NS:2 * NS] = si
        st_ref[d, 0] = sr
        st_ref[d, 1] = si

    def readout(d, xs_ref, y_ref):
        y = _dot(xs_ref[...].astype(BF), cm_ref[d])
        yh, yl = _split_bf16(y)
        for s in range(N_SUB):
            rs = slice(s * sub_rows, (s + 1) * sub_rows)
            ys = _dot(pout_ref[...], yh[rs]) + _dot(pout_ref[...], yl[rs])
            y_ref[:, s * T_PERM:(s + 1) * T_PERM, :] = ys.reshape(NB, T_PERM, SSM_W)

    project(0, uf_ref, xf_ref)
    project(1, ub_ref, xb_ref)
    scan(0, xf_ref, False)
    readout(0, xf_ref, yf_ref)
    scan(1, xb_ref, True)
    readout(1, xb_ref, yb_ref)


def _s5_scan(l, u, pin, pout, a_tab, bm, cm):
    rows = NB * T_SCAN
    full = lambda s: pl.BlockSpec(s, lambda i: (0,) * len(s))
    layer = lambda s: _layer_spec(l, s)
    chunk_f = pl.BlockSpec((NB, T_SCAN, SSM_W), lambda i: (0, i, 0))
    chunk_b = pl.BlockSpec((NB, T_SCAN, SSM_W), lambda i: (0, _bwd_chunk(i), 0))
    return pl.pallas_call(
        _scan_kernel,
        out_shape=(jax.ShapeDtypeStruct((NB, LT, SSM_W), F32),
                   jax.ShapeDtypeStruct((NB, LT, SSM_W), F32)),
        grid=(N_CHUNK,),
        in_specs=[chunk_f, chunk_b, full((NB * T_PERM, NB * T_PERM)), full((NB * T_PERM, NB * T_PERM)),
                  layer((2, 2, NS)), layer((2, SSM_W, 2 * NS)), layer((2, 2 * NS, SSM_W))],
        out_specs=(chunk_f, chunk_b),
        scratch_shapes=[pltpu.VMEM((rows, 2 * NS), F32), pltpu.VMEM((rows, 2 * NS), F32),
                        pltpu.VMEM((2, 2, NB, NS), F32)],
        compiler_params=pltpu.CompilerParams(
            dimension_semantics=("arbitrary",), vmem_limit_bytes=VMEM_LIMIT),
        name="s5_scan",
    )(u, u, pin, pout, a_tab, bm, cm)


def _pool_segment(ps, length):
    n = length + 2 * SUBLANES
    z = jnp.zeros((SUBLANES, POOL_W), F32)
    pe = jnp.concatenate([z, ps, z], axis=0)
    a1 = pe + pltpu.roll(pe, 1, 0)
    a2 = a1 + pltpu.roll(a1, 2, 0)
    a3 = a2 + pltpu.roll(a2, 4, 0)
    a4 = a3 + pltpu.roll(a3, 8, 0)
    lane = lax.broadcasted_iota(jnp.int32, (1, POOL_W), 1)
    half = jnp.where(lane < 64, 1, jnp.where(lane < 128, 2, jnp.where(lane < 192, 4, 8)))
    s = jnp.where(lane < 64, a1,
                  jnp.where(lane < 128, pltpu.roll(a2, n - 1, 0),
                            jnp.where(lane < 192, pltpu.roll(a3, n - 3, 0), pltpu.roll(a4, n - 7, 0))))
    s = s[SUBLANES:SUBLANES + length]
    t = lax.broadcasted_iota(jnp.int32, (length, 1), 0)
    cnt = jnp.minimum(t + half, length) - jnp.maximum(t - half, 0)
    return s / cnt.astype(F32) - ps


def _pool_kernel(p_ref, w_ref, sc_ref, o_ref):
    for lo, length in ((0, L_CTX), (L_CTX, L_LAT)):
        dlt = _pool_segment(p_ref[0, lo:lo + length, :], length)
        y = _dot(dlt.astype(BF), w_ref[...]) * sc_ref[...]
        o_ref[0, lo:lo + length, :] = y.astype(BF)


def _pool(l, p, w_bd, scale):
    return pl.pallas_call(
        _pool_kernel,
        out_shape=jax.ShapeDtypeStruct((NB, LT, POOL_W), BF),
        grid=(NB,),
        in_specs=[pl.BlockSpec((1, LT, POOL_W), lambda b: (b, 0, 0)),
                  _layer_spec(l, (POOL_W, POOL_W)),
                  _layer_spec(l, (1, POOL_W))],
        out_specs=pl.BlockSpec((1, LT, POOL_W), lambda b: (b, 0, 0)),
        compiler_params=pltpu.CompilerParams(
            dimension_semantics=("arbitrary",), vmem_limit_bytes=VMEM_LIMIT),
        name="pool",
    )(p, w_bd, scale)


def _mix_kernel(x_ref, at_ref, yf_ref, yb_ref, u_ref, pl_ref, mod_ref, d_ref, gw_ref, gb_ref, onw_ref,
                wo_ref, n2_ref, rw_ref, rb_ref, tri_ref,
                x1_ref, h2_ref, rt_ref, cnt_ref, run_ref):
    b = pl.program_id(0)
    i = pl.program_id(1)

    @pl.when((b == 0) & (i == 0))
    def _():
        run_ref[...] = jnp.zeros_like(run_ref)

    row = i * TM_TOK + lax.broadcasted_iota(jnp.int32, (TM_TOK, 1), 0)
    is_ctx = row < L_CTX
    y = yf_ref[0] + yb_ref[0] + d_ref[...] * u_ref[0]
    g = jax.nn.gelu(y, approximate=True)
    s = g * jax.nn.sigmoid(_dot(g.astype(BF), gw_ref[...]) + gb_ref[...])
    s = _rmsnorm(s, onw_ref[...]).astype(BF)
    mix = (_dot(at_ref[0], wo_ref[0:ATTN_W, :])
           + _dot(s, wo_ref[ATTN_W:ATTN_W + SSM_W, :])
           + _dot(pl_ref[0], wo_ref[ATTN_W + SSM_W:D, :]))
    x1 = x_ref[0] + _mod_rows(mod_ref, b, is_ctx, 2) * mix
    x1_ref[0] = x1
    h2 = _rmsnorm(x1, n2_ref[...])
    h2 = h2 * (1.0 + _mod_rows(mod_ref, b, is_ctx, 4)) + _mod_rows(mod_ref, b, is_ctx, 3)
    for j in range(D // LANES):
        h2_ref[pl.ds(j, TM_TOK, stride=SUBLANES), :] = h2[:, j * LANES:(j + 1) * LANES]
    logits = _dot3(h2, rw_ref[...]) + rb_ref[...]
    lane = lax.broadcasted_iota(jnp.int32, (TM_TOK, LANES), 1).astype(F32)
    vals, idxs, hits = [], [], []
    cur = logits
    for _ in range(TOP_K):
        m = jnp.max(cur, axis=-1, keepdims=True)
        idx = jnp.min(jnp.where(cur == m, lane, float(LANES)), axis=-1, keepdims=True)
        hit = lane == idx
        vals.append(m)
        idxs.append(idx)
        hits.append(hit)
        cur = jnp.where(hit, -jnp.inf, cur)
    ex = [jnp.exp(v - vals[0]) for v in vals]
    den = ex[0] + ex[1] + ex[2] + ex[3]
    onehot = jnp.where(hits[0] | hits[1] | hits[2] | hits[3], 1.0, 0.0)
    before = _dot(tri_ref[...], onehot.astype(BF)) + run_ref[...]
    route = jnp.zeros((TM_TOK, LANES), F32)
    for k in range(TOP_K):
        rank = jnp.sum(jnp.where(hits[k], before, 0.0), axis=-1, keepdims=True)
        route = jnp.where(lane == float(k), idxs[k], route)
        route = jnp.where(lane == float(TOP_K + k), ex[k] / den, route)
        route = jnp.where(lane == float(2 * TOP_K + k), rank, route)
    rt_ref[0] = route
    run_ref[...] = run_ref[...] + jnp.sum(onehot, axis=0, keepdims=True)
    cnt_ref[...] = jnp.broadcast_to(run_ref[...], (SUBLANES, LANES))


def _mix(l, xc, attn, yf, yb, u, pool, mod, ssm_d, glu_w, glu_b, onw, w_out, n2, rw, rb, tri):
    tok = lambda w: pl.BlockSpec((1, TM_TOK, w), lambda b, i: (b, i, 0))
    full = lambda s: pl.BlockSpec(s, lambda b, i: (0,) * len(s))
    layer = lambda s, col=0: _layer_spec(l, s, col)
    return pl.pallas_call(
        _mix_kernel,
        out_shape=(jax.ShapeDtypeStruct((NB, LT, D), F32),
                   jax.ShapeDtypeStruct((N_TOK * SUBLANES, LANES), F32),
                   jax.ShapeDtypeStruct((NB, LT, LANES), F32),
                   jax.ShapeDtypeStruct((SUBLANES, LANES), F32)),
        grid=(NB, LT // TM_TOK),
        in_specs=[tok(D), tok(ATTN_W), tok(SSM_W), tok(SSM_W), tok(SSM_W), tok(POOL_W),
                  layer((16, N_MOD * D)), layer((1, SSM_W)), layer((SSM_W, SSM_W)), layer((1, SSM_W)),
                  layer((1, SSM_W), ATTN_W // SSM_W), layer((D, D)), layer((1, D)), layer((D, LANES)),
                  layer((1, LANES)), full((TM_TOK, TM_TOK))],
        out_specs=(tok(D),
                   pl.BlockSpec((TM_TOK * SUBLANES, LANES), lambda b, i: (b * (LT // TM_TOK) + i, 0)),
                   tok(LANES), full((SUBLANES, LANES))),
        scratch_shapes=[pltpu.VMEM((1, LANES), F32)],
        compiler_params=pltpu.CompilerParams(
            dimension_semantics=("arbitrary", "arbitrary"), vmem_limit_bytes=VMEM_LIMIT),
        name="mix_router",
    )(xc, attn, yf, yb, u, pool, mod, ssm_d, glu_w, glu_b, onw, w_out, n2, rw, rb, tri)


def _rows_from_tiles(ref, rows):
    return jnp.concatenate(
        [ref[pl.ds(j, rows, stride=SUBLANES), :] for j in range(D // LANES)], axis=1)


def _rows_to_tiles(ref, val, rows):
    for j in range(D // LANES):
        ref[pl.ds(j, rows, stride=SUBLANES), :] = val[:, j * LANES:(j + 1) * LANES]


def _moe_kernel(te_ref, nu_ref, x_ref, wgu_ref, bgu_ref, wd_ref, bd_ref, o_ref, wgu_s, wd_s, act_s):
    i = pl.program_id(0)

    @pl.when(i < nu_ref[0])
    def _():
        e = te_ref[i]
        prev = te_ref[jnp.maximum(i - 1, 0)]

        @pl.when((i == 0) | (e != prev))
        def _():
            def cast_gu(r, c):
                rs = pl.ds(pl.multiple_of(r * 128, 128), 128)
                wgu_s[rs, :] = wgu_ref[rs, :].astype(BF)
                return c

            def cast_d(r, c):
                rs = pl.ds(pl.multiple_of(r * 128, 128), 128)
                wd_s[rs, :] = wd_ref[rs, :].astype(BF)
                return c

            lax.fori_loop(0, D // 128, cast_gu, 0)
            lax.fori_loop(0, D_FF // 128, cast_d, 0)

        x = _rows_from_tiles(x_ref, TM_MOE).astype(BF)
        for c in range(D_FF // FF_CHUNK):
            lo = c * FF_CHUNK
            gate = _dot(x, wgu_s[:, lo:lo + FF_CHUNK]) + bgu_ref[:, lo:lo + FF_CHUNK]
            up = (_dot(x, wgu_s[:, D_FF + lo:D_FF + lo + FF_CHUNK])
                  + bgu_ref[:, D_FF + lo:D_FF + lo + FF_CHUNK])
            gate = jnp.minimum(gate, SWIGLU_LIMIT)
            up = jnp.clip(up, -SWIGLU_LIMIT, SWIGLU_LIMIT)
            act = (up + 1.0) * (gate * jax.nn.sigmoid(SWIGLU_ALPHA * gate))
            act_s[:, lo:lo + FF_CHUNK] = act.astype(BF)
        for n in range(D // OUT_BLOCK):
            lo = n * OUT_BLOCK
            y = _dot(act_s[...], wd_s[:, lo:lo + OUT_BLOCK]) + bd_ref[:, lo:lo + OUT_BLOCK]
            for j in range(OUT_BLOCK // LANES):
                o_ref[pl.ds(lo // LANES + j, TM_MOE, stride=SUBLANES), :] = y[:, j * LANES:(j + 1) * LANES]


def _moe(l, tile_e, n_used, xs, w_gu, b_gu, w_down, b_down):
    def tile(i, te, nu):
        return (jnp.minimum(i, nu[0] - 1), 0)

    def expert(i, te, nu):
        return te[jnp.minimum(i, nu[0] - 1)]

    return pl.pallas_call(
        _moe_kernel,
        out_shape=jax.ShapeDtypeStruct((R_MOE * SUBLANES, LANES), F32),
        grid_spec=pltpu.PrefetchScalarGridSpec(
            num_scalar_prefetch=2,
            grid=(NT_MOE,),
            in_specs=[
                pl.BlockSpec((TM_MOE * SUBLANES, LANES), tile),
                pl.BlockSpec((None, None, D, 2 * D_FF), lambda i, te, nu: (l, expert(i, te, nu), 0, 0)),
                pl.BlockSpec((None, None, 1, 2 * D_FF), lambda i, te, nu: (l, expert(i, te, nu), 0, 0)),
                pl.BlockSpec((None, None, D_FF, D), lambda i, te, nu: (l, expert(i, te, nu), 0, 0)),
                pl.BlockSpec((None, None, 1, D), lambda i, te, nu: (l, expert(i, te, nu), 0, 0)),
            ],
            out_specs=pl.BlockSpec((TM_MOE * SUBLANES, LANES), tile),
            scratch_shapes=[pltpu.VMEM((D, 2 * D_FF), BF), pltpu.VMEM((D_FF, D), BF),
                            pltpu.VMEM((TM_MOE, D_FF), BF)],
        ),
        compiler_params=pltpu.CompilerParams(
            dimension_semantics=("arbitrary",), vmem_limit_bytes=VMEM_LIMIT),
        name="moe_experts",
    )(tile_e, n_used, xs, w_gu, b_gu.reshape(DEPTH, N_EXP, 1, 2 * D_FF), w_down,
      b_down.reshape(DEPTH, N_EXP, 1, D))


def _row_gather(table, idx):
    n_rows = idx.shape[0]
    per_worker = n_rows // SC_WORKERS
    n_chunks = per_worker // SC_CHUNK
    assert per_worker * SC_WORKERS == n_rows and n_chunks * SC_CHUNK == per_worker and n_chunks % 2 == 0
    mesh = plsc.VectorSubcoreMesh(core_axis_name="c", subcore_axis_name="s")
    row_tile = (SC_CHUNK, SUBLANES, LANES)

    @functools.partial(
        pl.kernel, mesh=mesh,
        out_type=jax.ShapeDtypeStruct((n_rows, SUBLANES, LANES), F32),
        scratch_types=[pltpu.VMEM((n_chunks, SC_CHUNK), jnp.int32),
                       pltpu.VMEM(row_tile, F32), pltpu.VMEM(row_tile, F32),
                       pltpu.SemaphoreType.DMA, pltpu.SemaphoreType.DMA],
        name="sc_row_gather",
    )
    def gather(table_hbm, idx_hbm, out_hbm, idx_v, buf0, buf1, sem0, sem1):
        wid = lax.axis_index("s") * SC_CORES + lax.axis_index("c")
        pltpu.sync_copy(idx_hbm.at[pl.ds(wid * n_chunks, n_chunks)], idx_v)
        base = wid * per_worker

        def fetch(chunk, buf, sem):
            return pltpu.make_async_copy(table_hbm.at[idx_v.at[chunk]], buf, sem)

        def emit(chunk, buf):
            pltpu.sync_copy(buf, out_hbm.at[pl.ds(base + chunk * SC_CHUNK, SC_CHUNK)])

        fetch(0, buf0, sem0).start()

        @pl.loop(0, n_chunks, step=2)
        def _(c):
            fetch(c + 1, buf1, sem1).start()
            fetch(c, buf0, sem0).wait()
            emit(c, buf0)

            @pl.when(c + 2 < n_chunks)
            def _():
                fetch(c + 2, buf0, sem0).start()

            fetch(c + 1, buf1, sem1).wait()
            emit(c + 1, buf1)

    out = gather(table.reshape(-1, SUBLANES, LANES), idx.reshape(n_rows // SC_CHUNK, SC_CHUNK))
    return out.reshape(n_rows * SUBLANES, LANES)


def _row_dispatch(rows, dest_t, n_out):
    per_worker = N_TOK // SC_WORKERS
    n_chunks = per_worker // SC_CHUNK
    assert per_worker * SC_WORKERS == N_TOK and n_chunks * SC_CHUNK == per_worker and n_chunks % 2 == 0
    idx_rows = TOP_K * n_chunks
    assert idx_rows % SUBLANES == 0
    mesh = plsc.VectorSubcoreMesh(core_axis_name="c", subcore_axis_name="s")
    row_tile = (SC_CHUNK, SUBLANES, LANES)

    @functools.partial(
        pl.kernel, mesh=mesh,
        out_type=jax.ShapeDtypeStruct((n_out, SUBLANES, LANES), F32),
        scratch_types=[pltpu.VMEM((idx_rows, SC_CHUNK), jnp.int32),
                       pltpu.VMEM(row_tile, F32), pltpu.VMEM(row_tile, F32),
                       pltpu.SemaphoreType.DMA, pltpu.SemaphoreType.DMA],
        name="sc_row_dispatch",
    )
    def dispatch(rows_hbm, idx_hbm, out_hbm, idx_v, buf0, buf1, sem0, sem1):
        wid = lax.axis_index("s") * SC_CORES + lax.axis_index("c")
        pltpu.sync_copy(idx_hbm.at[pl.ds(wid * idx_rows, idx_rows)], idx_v)
        base = wid * per_worker

        def fetch(chunk, buf, sem):
            return pltpu.make_async_copy(rows_hbm.at[pl.ds(base + chunk * SC_CHUNK, SC_CHUNK)], buf, sem)

        def emit(chunk, buf):
            for k in range(TOP_K):
                pltpu.sync_copy(buf, out_hbm.at[idx_v.at[k * n_chunks + chunk]])

        fetch(0, buf0, sem0).start()

        @pl.loop(0, n_chunks, step=2)
        def _(c):
            fetch(c + 1, buf1, sem1).start()
            fetch(c, buf0, sem0).wait()
            emit(c, buf0)

            @pl.when(c + 2 < n_chunks)
            def _():
                fetch(c + 2, buf0, sem0).start()

            fetch(c + 1, buf1, sem1).wait()
            emit(c + 1, buf1)

    idx = dest_t.reshape(TOP_K, SC_WORKERS, n_chunks, SC_CHUNK).transpose(1, 0, 2, 3)
    out = dispatch(rows.reshape(-1, SUBLANES, LANES), idx.reshape(SC_WORKERS * idx_rows, SC_CHUNK))
    return out.reshape(n_out * SUBLANES, LANES)


def _combine_kernel(first_tile, x1_ref, rt_ref, mod_ref, g0_ref, g1_ref, g2_ref, g3_ref, o_ref):
    t = pl.program_id(0)
    per_b = LT // TQ - first_tile
    b = t // per_b
    is_ctx = (t % per_b + first_tile) == 0
    route = rt_ref[0]
    y = jnp.zeros((TQ, D), F32)
    for k, g_ref in enumerate((g0_ref, g1_ref, g2_ref, g3_ref)):
        y = y + route[:, TOP_K + k:TOP_K + k + 1] * _rows_from_tiles(g_ref, TQ)
    gate = jnp.where(is_ctx, mod_ref[NB:NB + 1, 5 * D:6 * D], mod_ref[pl.ds(b, 1), 5 * D:6 * D])
    o_ref[0] = x1_ref[0] + gate * y


def _combine(l, x1, route, mod, gathered, latent_only):
    tiles_b = LT // TQ
    first_tile = L_CTX // TQ if latent_only else 0
    per_b = tiles_b - first_tile
    tile_of = lambda t: (t // per_b, t % per_b + first_tile)
    tok = lambda w: pl.BlockSpec((1, TQ, w), lambda t: tile_of(t) + (0,))
    slot = lambda k: pl.BlockSpec(
        (TQ * SUBLANES, LANES), lambda t: (k * NB * tiles_b + tile_of(t)[0] * tiles_b + tile_of(t)[1], 0))
    return pl.pallas_call(
        functools.partial(_combine_kernel, first_tile),
        out_shape=jax.ShapeDtypeStruct((NB, per_b * TQ, D), F32),
        grid=(NB * per_b,),
        in_specs=[tok(D), tok(LANES), _layer_spec(l, (16, N_MOD * D)),
                  slot(0), slot(1), slot(2), slot(3)],
        out_specs=pl.BlockSpec((1, TQ, D), lambda t: (t // per_b, t % per_b, 0)),
        compiler_params=pltpu.CompilerParams(
            dimension_semantics=("arbitrary",), vmem_limit_bytes=VMEM_LIMIT),
        name="moe_combine",
    )(x1, route, mod, gathered, gathered, gathered, gathered)


def _rope_tables():
    pos = jnp.arange(L_LAT)
    row = (pos // GRID_W).astype(F32)
    col = (pos % GRID_W).astype(F32)
    inv = ROPE_BASE ** (-jnp.arange(0, 32, 2, dtype=F32) / 32)
    ang_r = row[:, None] * inv
    ang_c = col[:, None] * inv
    ang = jnp.concatenate([ang_r, ang_r, ang_c, ang_c], axis=-1)
    cos = jnp.concatenate([jnp.ones((L_CTX, HEAD_DIM), F32), jnp.cos(ang)], axis=0)
    sin = jnp.concatenate([jnp.zeros((L_CTX, HEAD_DIM), F32), jnp.sin(ang)], axis=0)
    sign = jnp.where((jnp.arange(HEAD_DIM) % 32) < 16, -1.0, 1.0).astype(F32)
    return jnp.tile(cos, (1, 2)), jnp.tile(sin * sign, (1, 2))


def _s5_tables(a_re, a_im, log_dt, b_re, b_im, c_re, c_im):
    dt = jnp.exp(log_dt)[..., None]
    mag = jnp.exp(a_re * dt)
    ar = mag * jnp.cos(a_im * dt)
    ai = mag * jnp.sin(a_im * dt)
    den = a_re * a_re + a_im * a_im
    qr = ((ar - 1) * a_re + ai * a_im) / den
    qi = (ai * a_re - (ar - 1) * a_im) / den
    bbr = qr[..., None] * b_re - qi[..., None] * b_im
    bbi = qr[..., None] * b_im + qi[..., None] * b_re
    eye = jnp.eye(SSM_GROUPS, dtype=F32)
    to_in = lambda m: jnp.einsum('ldgnp,gh->ldgphn', m, eye).reshape(DEPTH, 2, SSM_W, NS)
    bm = jnp.concatenate([to_in(bbr), to_in(bbi)], axis=-1).astype(BF)
    to_out = lambda m: jnp.einsum('ldgpn,gh->ldgnhp', m, eye).reshape(DEPTH, 2, NS, SSM_W)
    cm = jnp.concatenate([to_out(c_re), -to_out(c_im)], axis=2).astype(BF)
    a_tab = jnp.stack([ar.reshape(DEPTH, 2, NS), ai.reshape(DEPTH, 2, NS)], axis=2)
    return a_tab, bm, cm


def _interleave_perms():
    rows = NB * T_PERM
    r = jnp.arange(rows)
    src = (r % NB) * T_PERM + r // NB
    pin = (src[:, None] == jnp.arange(rows)[None, :]).astype(BF)
    return pin, pin.T


def kernel(x, c, ctx, c_ctx, w_mod, b_mod, norm1_w, norm2_w, w_in, q_norm_w, k_norm_w, attn_sink,
           ssm_a_re, ssm_a_im, ssm_log_dt, ssm_b_re, ssm_b_im, ssm_c_re, ssm_c_im, ssm_d, glu_w, glu_b,
           pool_w, pool_scale, out_norm_w, w_out, router_w, router_b, exp_w_gu, exp_b_gu, exp_w_down,
           exp_b_down):
    xc = jnp.concatenate([ctx, x], axis=1)
    cc = jnp.concatenate([c, c_ctx[None, :], jnp.zeros((16 - NB - 1, D), F32)], axis=0)
    mod = _adaln(cc, w_mod, b_mod)
    cos2, sin2 = _rope_tables()
    pin, pout = _interleave_perms()
    tri = (jnp.arange(TM_TOK)[:, None] > jnp.arange(TM_TOK)[None, :]).astype(BF)

    row = lambda a: a.reshape(DEPTH, 1, -1)
    dup = lambda m: jnp.concatenate([m[..., :64], m[..., :64], m[..., 64:], m[..., 64:]], axis=-1)
    w_ext = jnp.concatenate([w_in[..., :512], dup(w_in[..., 512:640]), dup(w_in[..., 640:768]),
                             w_in[..., 768:]], axis=-1).astype(BF)
    qw2 = row(jnp.tile(q_norm_w, (1, 2)))
    kw2 = row(jnp.tile(k_norm_w, (1, 2)))
    a_tab, bm, cm = _s5_tables(ssm_a_re, ssm_a_im, ssm_log_dt, ssm_b_re, ssm_b_im, ssm_c_re, ssm_c_im)
    pw_bd = jnp.einsum('lgcd,gh->lgchd', pool_w, jnp.eye(4, dtype=F32)).reshape(DEPTH, POOL_W, POOL_W).astype(BF)
    rw = jnp.concatenate([router_w, jnp.zeros((DEPTH, D, LANES - N_EXP), F32)], axis=-1)
    rb = row(jnp.concatenate([router_b, jnp.full((DEPTH, LANES - N_EXP), -1e30, F32)], axis=-1))
    glu_w_bf = glu_w.astype(BF)
    w_out_bf = w_out.astype(BF)
    n1, n2, onw = row(norm1_w), row(norm2_w), row(out_norm_w)
    ssm_d3, glu_b3, pool_sc3 = row(ssm_d), row(glu_b), row(pool_scale)
    sink = attn_sink.reshape(-1)

    for l in range(DEPTH):
        q, kd, vd, u, p = _inproj(l, xc, mod, n1, w_ext, qw2, kw2, cos2, sin2)
        attn = _attention(l, sink, q, kd, vd, onw)
        yf, yb = _s5_scan(l, u, pin, pout, a_tab, bm, cm)
        pool = _pool(l, p, pw_bd, pool_sc3)
        x1, h2, route, cnt = _mix(l, xc, attn, yf, yb, u, pool, mod, ssm_d3, glu_w_bf, glu_b3, onw,
                                  w_out_bf, n2, rw, rb, tri)

        route = route.reshape(N_TOK, LANES)
        top_i = route[:, 0:TOP_K].astype(jnp.int32)
        rank = route[:, 2 * TOP_K:3 * TOP_K].astype(jnp.int32)
        counts = cnt[0, :N_EXP].astype(jnp.int32)
        padded = (counts + TM_MOE - 1) // TM_MOE * TM_MOE
        pend = jnp.cumsum(padded)
        pstart = pend - padded
        dest = pstart[top_i] + rank
        dest_t = dest.T
        n_used = (pend[-1] // TM_MOE).astype(jnp.int32)[None]
        tile_lo = jnp.arange(NT_MOE, dtype=jnp.int32) * TM_MOE
        tile_e = jnp.minimum(jnp.sum(pend[None, :] <= tile_lo[:, None], axis=1), N_EXP - 1).astype(jnp.int32)
        xs = _row_dispatch(h2, dest_t, R_MOE)

        yb_rows = _moe(l, tile_e, n_used, xs, exp_w_gu, exp_b_gu, exp_w_down, exp_b_down)
        gathered = _row_gather(yb_rows, dest_t.reshape(-1))
        xc = _combine(l, x1, route.reshape(NB, LT, LANES), mod, gathered, latent_only=(l == DEPTH - 1))
    return xc
```

```python
import functools
import math

import jax
import jax.numpy as jnp
from jax import lax
from jax.experimental import pallas as pl
from jax.experimental.pallas import tpu as pltpu
from jax.experimental.pallas import tpu_sc as plsc

D = 1024
NB = 8
L_LAT = 2048
L_CTX = 256
LT = L_CTX + L_LAT
DEPTH = 4
N_MOD = 6
EPS = 1e-6
N_HEADS = 8
HEAD_DIM = 64
ATTN_W = 512
WINDOW = 128
ATTN_SCALE = HEAD_DIM ** -0.5
LOG2E = math.log2(math.e)
ROPE_BASE = 10000.0
GRID_W = 64
SSM_W = 256
SSM_GROUP = 16
SSM_GROUPS = 16
SSM_STATE = 64
NS = SSM_GROUPS * SSM_STATE
POOL_W = 256
POOL_GROUP = 64
N_EXP = 32
TOP_K = 4
D_FF = 1024
SWIGLU_LIMIT = 7.0
SWIGLU_ALPHA = 1.702

LANES = 128
SUBLANES = 8
VMEM_LIMIT = 56 * 1024 * 1024

TM_TOK = 768
TQ = 256
KSPAN = TQ + 2 * WINDOW
T_SCAN = 128
T_PERM = 32
N_SUB = T_SCAN // T_PERM
N_CHUNK = LT // T_SCAN
N_CHUNK_CTX = L_CTX // T_SCAN
TM_MOE = 512
N_TOK = NB * LT
N_ASSIGN = N_TOK * TOP_K
NT_MOE = N_ASSIGN // TM_MOE + N_EXP
R_MOE = NT_MOE * TM_MOE
FF_CHUNK = 512
OUT_BLOCK = 256

SC_CORES = 2
SC_SUBCORES = 16
SC_WORKERS = SC_CORES * SC_SUBCORES
SC_CHUNK = 32

C_Q = 0
C_K = 512
C_V = 768
C_U = 1024
C_P = 1280
NW_IN = 1536

BF = jnp.bfloat16
F32 = jnp.float32


def _split_bf16(a):
    hi = a.astype(BF)
    lo = (a - hi.astype(F32)).astype(BF)
    return hi, lo


def _dot(a, b):
    return jnp.dot(a, b, preferred_element_type=F32)


def _dot3(a, b):
    ah, al = _split_bf16(a)
    bh, bl = _split_bf16(b)
    return _dot(ah, bh) + (_dot(ah, bl) + _dot(al, bh))


def _mod_kernel(c_ref, w_ref, b_ref, o_ref):
    c = c_ref[...]
    s = c * jax.nn.sigmoid(c)
    o_ref[0] = _dot3(s, w_ref[0]) + b_ref[0]


def _adaln(cc, w_mod, b_mod):
    tn = 1536
    return pl.pallas_call(
        _mod_kernel,
        out_shape=jax.ShapeDtypeStruct((DEPTH, 16, N_MOD * D), F32),
        grid=(DEPTH, N_MOD * D // tn),
        in_specs=[
            pl.BlockSpec((16, D), lambda l, j: (0, 0)),
            pl.BlockSpec((1, D, tn), lambda l, j: (l, 0, j)),
            pl.BlockSpec((1, 1, tn), lambda l, j: (l, 0, j)),
        ],
        out_specs=pl.BlockSpec((1, 16, tn), lambda l, j: (l, 0, j)),
        compiler_params=pltpu.CompilerParams(
            dimension_semantics=("arbitrary", "arbitrary"), vmem_limit_bytes=VMEM_LIMIT),
        name="adaln_mod",
    )(cc, w_mod, b_mod.reshape(DEPTH, 1, N_MOD * D))


def _layer_spec(l, shape, col_block=0):
    shape = tuple(shape)
    return pl.BlockSpec((None,) + shape, lambda *_: (l,) + (0,) * (len(shape) - 1) + (col_block,))


def _mod_rows(mod_ref, b, is_ctx, idx):
    lat = mod_ref[pl.ds(b, 1), idx * D:(idx + 1) * D]
    ctx = mod_ref[NB:NB + 1, idx * D:(idx + 1) * D]
    return jnp.where(is_ctx, ctx, lat)


def _rmsnorm(x, w):
    ms = jnp.mean(x * x, axis=-1, keepdims=True)
    return x * lax.rsqrt(ms + EPS) * w


def _headnorm_pair(t, w2, lane):
    sq = t * t
    first = lane < HEAD_DIM
    a = jnp.sum(jnp.where(first, sq, 0.0), axis=-1, keepdims=True)
    b = jnp.sum(jnp.where(first, 0.0, sq), axis=-1, keepdims=True)
    ms = jnp.where(first, a, b) * (1.0 / HEAD_DIM)
    return t * lax.rsqrt(ms + EPS) * w2


def _rope_pair(t, cos, sin_signed, lane):
    lower = (lane % 32) < 16
    partner = jnp.where(lower, pltpu.roll(t, LANES - 16, 1), pltpu.roll(t, 16, 1))
    return t * cos + partner * sin_signed


def _inproj_kernel(x_ref, mod_ref, n1_ref, w_ref, qw_ref, kw_ref, cos_ref, sin_ref,
                   q_ref, k_ref, v_ref, u_ref, p_ref):
    b = pl.program_id(0)
    i = pl.program_id(1)
    x = x_ref[0]
    row = i * TM_TOK + lax.broadcasted_iota(jnp.int32, (TM_TOK, 1), 0)
    is_ctx = row < L_CTX
    h = _rmsnorm(x, n1_ref[...])
    h = h * (1.0 + _mod_rows(mod_ref, b, is_ctx, 1)) + _mod_rows(mod_ref, b, is_ctx, 0)
    px = _dot(h.astype(BF), w_ref[...])
    lane = lax.broadcasted_iota(jnp.int32, (TM_TOK, LANES), 1)
    cos = cos_ref[...]
    sin = sin_ref[...]
    for j in range(ATTN_W // LANES):
        t = px[:, C_Q + j * LANES:C_Q + (j + 1) * LANES]
        t = _rope_pair(_headnorm_pair(t, qw_ref[...], lane), cos, sin, lane) * (ATTN_SCALE * LOG2E)
        q_ref[0, :, j * LANES:(j + 1) * LANES] = t.astype(BF)
    for g in range(2):
        t = px[:, C_K + g * LANES:C_K + (g + 1) * LANES]
        t = _rope_pair(_headnorm_pair(t, kw_ref[...], lane), cos, sin, lane)
        k_ref[0, :, g * LANES:(g + 1) * LANES] = t.astype(BF)
        t = px[:, C_V + g * LANES:C_V + (g + 1) * LANES]
        v_ref[0, :, g * LANES:(g + 1) * LANES] = jnp.where(lane < HEAD_DIM, t, 1.0).astype(BF)
    u_ref[0] = px[:, C_U:C_P]
    p_ref[0] = px[:, C_P:NW_IN]


def _inproj(l, xc, mod, n1, w_ext, qw2, kw2, cos2, sin2):
    tok = lambda w: pl.BlockSpec((1, TM_TOK, w), lambda b, i: (b, i, 0))
    full = lambda s: _layer_spec(l, s)
    return pl.pallas_call(
        _inproj_kernel,
        out_shape=(
            jax.ShapeDtypeStruct((NB, LT, ATTN_W), BF),
            jax.ShapeDtypeStruct((NB, LT, 256), BF),
            jax.ShapeDtypeStruct((NB, LT, 256), BF),
            jax.ShapeDtypeStruct((NB, LT, SSM_W), F32),
            jax.ShapeDtypeStruct((NB, LT, POOL_W), F32),
        ),
        grid=(NB, LT // TM_TOK),
        in_specs=[
            tok(D), full((16, N_MOD * D)), full((1, D)), full((D, NW_IN)),
            full((1, LANES)), full((1, LANES)),
            pl.BlockSpec((TM_TOK, LANES), lambda b, i: (i, 0)),
            pl.BlockSpec((TM_TOK, LANES), lambda b, i: (i, 0)),
        ],
        out_specs=(tok(ATTN_W), tok(256), tok(256), tok(SSM_W), tok(POOL_W)),
        compiler_params=pltpu.CompilerParams(
            dimension_semantics=("arbitrary", "arbitrary"), vmem_limit_bytes=VMEM_LIMIT),
        name="inproj",
    )(xc, mod, n1, w_ext, qw2, kw2, cos2, sin2)


def _attn_kernel(l, sink_ref, q_ref, k_ref, v_ref, onw_ref, o_ref, acc_ref, bias_ref):
    j = pl.program_id(1)
    start = pl.multiple_of(jnp.clip(j * TQ - WINDOW, LANES, LT - KSPAN), LANES)
    lane = lax.broadcasted_iota(jnp.int32, (TQ, LANES), 1)
    first = lane < HEAD_DIM
    row2 = lax.broadcasted_iota(jnp.int32, (2 * TQ, 1), 0)
    qpos = j * TQ + jnp.where(row2 < TQ, row2, row2 - TQ)
    kpos = start + lax.broadcasted_iota(jnp.int32, (1, KSPAN), 1)
    valid = (jnp.abs(qpos - kpos) <= WINDOW) & (kpos >= L_CTX) & (j >= 1)
    bias_ref[...] = jnp.where(valid, 0.0, -1e30)
    for g in range(2):
        kc = k_ref[0, 0:L_CTX, g * LANES:(g + 1) * LANES]
        vc = v_ref[0, 0:L_CTX, g * LANES:(g + 1) * LANES]
        kl = k_ref[0, pl.ds(start, KSPAN), g * LANES:(g + 1) * LANES]
        vl = v_ref[0, pl.ds(start, KSPAN), g * LANES:(g + 1) * LANES]
        for hp in (2 * g, 2 * g + 1):
            q2 = q_ref[0, :, hp * LANES:(hp + 1) * LANES]
            zero = jnp.zeros_like(q2)
            qs = jnp.concatenate([jnp.where(first, q2, zero), jnp.where(first, zero, q2)], axis=0)
            dn = (((1,), (1,)), ((), ()))
            s_ctx = lax.dot_general(qs, kc, dn, preferred_element_type=F32)
            s_loc = lax.dot_general(qs, kl, dn, preferred_element_type=F32) + bias_ref[...]
            sink = jnp.where(row2 < TQ, sink_ref[l * N_HEADS + 2 * hp],
                             sink_ref[l * N_HEADS + 2 * hp + 1]) * LOG2E
            m = jnp.maximum(jnp.maximum(jnp.max(s_loc, axis=-1, keepdims=True),
                                        jnp.max(s_ctx, axis=-1, keepdims=True)), sink)
            e_loc = jnp.exp2(s_loc - m)
            e_ctx = jnp.exp2(s_ctx - m)
            o = _dot(e_loc.astype(BF), vl) + _dot(e_ctx.astype(BF), vc)
            den = pltpu.roll(o + jnp.exp2(sink - m), HEAD_DIM, 1)
            o = o / den
            acc_ref[:, hp * LANES:(hp + 1) * LANES] = jnp.where(
                first, o[0:TQ], pltpu.roll(o[TQ:2 * TQ], HEAD_DIM, 1))
    o_ref[0] = _rmsnorm(acc_ref[...], onw_ref[...]).astype(BF)


def _attention(l, sink, q, kd, vd, onw):
    return pl.pallas_call(
        functools.partial(_attn_kernel, l),
        out_shape=jax.ShapeDtypeStruct((NB, LT, ATTN_W), BF),
        grid_spec=pltpu.PrefetchScalarGridSpec(
            num_scalar_prefetch=1,
            grid=(NB, LT // TQ),
            in_specs=[
                pl.BlockSpec((1, TQ, ATTN_W), lambda b, j, s: (b, j, 0)),
                pl.BlockSpec((1, LT, 256), lambda b, j, s: (b, 0, 0)),
                pl.BlockSpec((1, LT, 256), lambda b, j, s: (b, 0, 0)),
                _layer_spec(l, (1, ATTN_W)),
            ],
            out_specs=pl.BlockSpec((1, TQ, ATTN_W), lambda b, j, s: (b, j, 0)),
            scratch_shapes=[pltpu.VMEM((TQ, ATTN_W), F32), pltpu.VMEM((2 * TQ, KSPAN), F32)],
        ),
        compiler_params=pltpu.CompilerParams(
            dimension_semantics=("arbitrary", "arbitrary"), vmem_limit_bytes=VMEM_LIMIT),
        name="window_attn",
    )(sink, q, kd, vd, onw)


def _bwd_chunk(i):
    return jnp.where(i < N_CHUNK_CTX, N_CHUNK_CTX - 1 - i, N_CHUNK - 1 - (i - N_CHUNK_CTX))


def _scan_kernel(uf_ref, ub_ref, pin_ref, pout_ref, a_ref, bm_ref, cm_ref, yf_ref, yb_ref,
                 xf_ref, xb_ref, st_ref):
    i = pl.program_id(0)
    sub_rows = NB * T_PERM

    @pl.when(i == 0)
    def _():
        st_ref[...] = jnp.zeros_like(st_ref)

    def project(d, u_ref, xs_ref):
        ui = jnp.concatenate(
            [_dot(pin_ref[...], u_ref[:, s * T_PERM:(s + 1) * T_PERM, :].reshape(sub_rows, SSM_W).astype(BF))
             for s in range(N_SUB)], axis=0).astype(BF)
        xs_ref[...] = _dot(ui, bm_ref[d])

    def scan(d, xs_ref, reverse):
        ar = jnp.broadcast_to(a_ref[d, 0:1, :], (NB, NS))
        ai = jnp.broadcast_to(a_ref[d, 1:2, :], (NB, NS))
        sr = st_ref[d, 0]
        si = st_ref[d, 1]
        for t in (range(T_SCAN - 1, -1, -1) if reverse else range(T_SCAN)):
            r = pl.ds(t * NB, NB)
            nr = ar * sr - ai * si + xs_ref[r, 0:NS]
            ni = ar * si + ai * sr + xs_ref[r, NS:2 * NS]
            sr, si = nr, ni
            xs_ref[r, 0:NS] = sr
            xs_ref[r, NS:2 * NS] = si
        st_ref[d, 0] = sr
        st_ref[d, 1] = si

    def readout(d, xs_ref, y_ref):
        y = _dot(xs_ref[...].astype(BF), cm_ref[d])
        yh, yl = _split_bf16(y)
        for s in range(N_SUB):
            rs = slice(s * sub_rows, (s + 1) * sub_rows)
            ys = _dot(pout_ref[...], yh[rs]) + _dot(pout_ref[...], yl[rs])
            y_ref[:, s * T_PERM:(s + 1) * T_PERM, :] = ys.reshape(NB, T_PERM, SSM_W)

    project(0, uf_ref, xf_ref)
    project(1, ub_ref, xb_ref)
    scan(0, xf_ref, False)
    readout(0, xf_ref, yf_ref)
    scan(1, xb_ref, True)
    readout(1, xb_ref, yb_ref)


def _s5_scan(l, u, pin, pout, a_tab, bm, cm):
    rows = NB * T_SCAN
    full = lambda s: pl.BlockSpec(s, lambda i: (0,) * len(s))
    layer = lambda s: _layer_spec(l, s)
    chunk_f = pl.BlockSpec((NB, T_SCAN, SSM_W), lambda i: (0, i, 0))
    chunk_b = pl.BlockSpec((NB, T_SCAN, SSM_W), lambda i: (0, _bwd_chunk(i), 0))
    return pl.pallas_call(
        _scan_kernel,
        out_shape=(jax.ShapeDtypeStruct((NB, LT, SSM_W), F32),
                   jax.ShapeDtypeStruct((NB, LT, SSM_W), F32)),
        grid=(N_CHUNK,),
        in_specs=[chunk_f, chunk_b, full((NB * T_PERM, NB * T_PERM)), full((NB * T_PERM, NB * T_PERM)),
                  layer((2, 2, NS)), layer((2, SSM_W, 2 * NS)), layer((2, 2 * NS, SSM_W))],
        out_specs=(chunk_f, chunk_b),
        scratch_shapes=[pltpu.VMEM((rows, 2 * NS), F32), pltpu.VMEM((rows, 2 * NS), F32),
                        pltpu.VMEM((2, 2, NB, NS), F32)],
        compiler_params=pltpu.CompilerParams(
            dimension_semantics=("arbitrary",), vmem_limit_bytes=VMEM_LIMIT),
        name="s5_scan",
    )(u, u, pin, pout, a_tab, bm, cm)


def _pool_segment(ps, length):
    n = length + 2 * SUBLANES
    z = jnp.zeros((SUBLANES, POOL_W), F32)
    pe = jnp.concatenate([z, ps, z], axis=0)
    a1 = pe + pltpu.roll(pe, 1, 0)
    a2 = a1 + pltpu.roll(a1, 2, 0)
    a3 = a2 + pltpu.roll(a2, 4, 0)
    a4 = a3 + pltpu.roll(a3, 8, 0)
    lane = lax.broadcasted_iota(jnp.int32, (1, POOL_W), 1)
    half = jnp.where(lane < 64, 1, jnp.where(lane < 128, 2, jnp.where(lane < 192, 4, 8)))
    s = jnp.where(lane < 64, a1,
                  jnp.where(lane < 128, pltpu.roll(a2, n - 1, 0),
                            jnp.where(lane < 192, pltpu.roll(a3, n - 3, 0), pltpu.roll(a4, n - 7, 0))))
    s = s[SUBLANES:SUBLANES + length]
    t = lax.broadcasted_iota(jnp.int32, (length, 1), 0)
    cnt = jnp.minimum(t + half, length) - jnp.maximum(t - half, 0)
    return s / cnt.astype(F32) - ps


def _pool_kernel(p_ref, w_ref, sc_ref, o_ref):
    for lo, length in ((0, L_CTX), (L_CTX, L_LAT)):
        dlt = _pool_segment(p_ref[0, lo:lo + length, :], length)
        y = _dot(dlt.astype(BF), w_ref[...]) * sc_ref[...]
        o_ref[0, lo:lo + length, :] = y.astype(BF)


def _pool(l, p, w_bd, scale):
    return pl.pallas_call(
        _pool_kernel,
        out_shape=jax.ShapeDtypeStruct((NB, LT, POOL_W), BF),
        grid=(NB,),
        in_specs=[pl.BlockSpec((1, LT, POOL_W), lambda b: (b, 0, 0)),
                  _layer_spec(l, (POOL_W, POOL_W)),
                  _layer_spec(l, (1, POOL_W))],
        out_specs=pl.BlockSpec((1, LT, POOL_W), lambda b: (b, 0, 0)),
        compiler_params=pltpu.CompilerParams(
            dimension_semantics=("arbitrary",), vmem_limit_bytes=VMEM_LIMIT),
        name="pool",
    )(p, w_bd, scale)


def _mix_kernel(x_ref, at_ref, yf_ref, yb_ref, u_ref, pl_ref, mod_ref, d_ref, gw_ref, gb_ref, onw_ref,
                wo_ref, n2_ref, rw_ref, rb_ref, tri_ref,
                x1_ref, h2_ref, rt_ref, cnt_ref, run_ref):
    b = pl.program_id(0)
    i = pl.program_id(1)

    @pl.when((b == 0) & (i == 0))
    def _():
        run_ref[...] = jnp.zeros_like(run_ref)

    row = i * TM_TOK + lax.broadcasted_iota(jnp.int32, (TM_TOK, 1), 0)
    is_ctx = row < L_CTX
    y = yf_ref[0] + yb_ref[0] + d_ref[...] * u_ref[0]
    g = jax.nn.gelu(y, approximate=True)
    s = g * jax.nn.sigmoid(_dot(g.astype(BF), gw_ref[...]) + gb_ref[...])
    s = _rmsnorm(s, onw_ref[...]).astype(BF)
    mix = (_dot(at_ref[0], wo_ref[0:ATTN_W, :])
           + _dot(s, wo_ref[ATTN_W:ATTN_W + SSM_W, :])
           + _dot(pl_ref[0], wo_ref[ATTN_W + SSM_W:D, :]))
    x1 = x_ref[0] + _mod_rows(mod_ref, b, is_ctx, 2) * mix
    x1_ref[0] = x1
    h2 = _rmsnorm(x1, n2_ref[...])
    h2 = h2 * (1.0 + _mod_rows(mod_ref, b, is_ctx, 4)) + _mod_rows(mod_ref, b, is_ctx, 3)
    for j in range(D // LANES):
        h2_ref[pl.ds(j, TM_TOK, stride=SUBLANES), :] = h2[:, j * LANES:(j + 1) * LANES]
    logits = _dot3(h2, rw_ref[...]) + rb_ref[...]
    lane = lax.broadcasted_iota(jnp.int32, (TM_TOK, LANES), 1).astype(F32)
    vals, idxs, hits = [], [], []
    cur = logits
    for _ in range(TOP_K):
        m = jnp.max(cur, axis=-1, keepdims=True)
        idx = jnp.min(jnp.where(cur == m, lane, float(LANES)), axis=-1, keepdims=True)
        hit = lane == idx
        vals.append(m)
        idxs.append(idx)
        hits.append(hit)
        cur = jnp.where(hit, -jnp.inf, cur)
    ex = [jnp.exp(v - vals[0]) for v in vals]
    den = ex[0] + ex[1] + ex[2] + ex[3]
    onehot = jnp.where(hits[0] | hits[1] | hits[2] | hits[3], 1.0, 0.0)
    before = _dot(tri_ref[...], onehot.astype(BF)) + run_ref[...]
    route = jnp.zeros((TM_TOK, LANES), F32)
    for k in range(TOP_K):
        rank = jnp.sum(jnp.where(hits[k], before, 0.0), axis=-1, keepdims=True)
        route = jnp.where(lane == float(k), idxs[k], route)
        route = jnp.where(lane == float(TOP_K + k), ex[k] / den, route)
        route = jnp.where(lane == float(2 * TOP_K + k), rank, route)
    rt_ref[0] = route
    run_ref[...] = run_ref[...] + jnp.sum(onehot, axis=0, keepdims=True)
    cnt_ref[...] = jnp.broadcast_to(run_ref[...], (SUBLANES, LANES))


def _mix(l, xc, attn, yf, yb, u, pool, mod, ssm_d, glu_w, glu_b, onw, w_out, n2, rw, rb, tri):
    tok = lambda w: pl.BlockSpec((1, TM_TOK, w), lambda b, i: (b, i, 0))
    full = lambda s: pl.BlockSpec(s, lambda b, i: (0,) * len(s))
    layer = lambda s, col=0: _layer_spec(l, s, col)
    return pl.pallas_call(
        _mix_kernel,
        out_shape=(jax.ShapeDtypeStruct((NB, LT, D), F32),
                   jax.ShapeDtypeStruct((N_TOK * SUBLANES, LANES), F32),
                   jax.ShapeDtypeStruct((NB, LT, LANES), F32),
                   jax.ShapeDtypeStruct((SUBLANES, LANES), F32)),
        grid=(NB, LT // TM_TOK),
        in_specs=[tok(D), tok(ATTN_W), tok(SSM_W), tok(SSM_W), tok(SSM_W), tok(POOL_W),
                  layer((16, N_MOD * D)), layer((1, SSM_W)), layer((SSM_W, SSM_W)), layer((1, SSM_W)),
                  layer((1, SSM_W), ATTN_W // SSM_W), layer((D, D)), layer((1, D)), layer((D, LANES)),
                  layer((1, LANES)), full((TM_TOK, TM_TOK))],
        out_specs=(tok(D),
                   pl.BlockSpec((TM_TOK * SUBLANES, LANES), lambda b, i: (b * (LT // TM_TOK) + i, 0)),
                   tok(LANES), full((SUBLANES, LANES))),
        scratch_shapes=[pltpu.VMEM((1, LANES), F32)],
        compiler_params=pltpu.CompilerParams(
            dimension_semantics=("arbitrary", "arbitrary"), vmem_limit_bytes=VMEM_LIMIT),
        name="mix_router",
    )(xc, attn, yf, yb, u, pool, mod, ssm_d, glu_w, glu_b, onw, w_out, n2, rw, rb, tri)


def _rows_from_tiles(ref, rows):
    return jnp.concatenate(
        [ref[pl.ds(j, rows, stride=SUBLANES), :] for j in range(D // LANES)], axis=1)


def _rows_to_tiles(ref, val, rows):
    for j in range(D // LANES):
        ref[pl.ds(j, rows, stride=SUBLANES), :] = val[:, j * LANES:(j + 1) * LANES]


def _moe_kernel(l, te_ref, nu_ref, slot_ref, nxt_ref, x_ref, wgu_hbm, bgu_ref, wd_hbm, bd_ref, o_ref,
                wgu_f, wd_f, wgu_s, wd_s, act_s, sem):
    i = pl.program_id(0)

    @pl.when(i >= nu_ref[0])
    def _():
        o_ref[...] = jnp.zeros_like(o_ref)

    def fetch(e, slot):
        return (pltpu.make_async_copy(wgu_hbm.at[l, e], wgu_f.at[slot], sem.at[0, slot]),
                pltpu.make_async_copy(wd_hbm.at[l, e], wd_f.at[slot], sem.at[1, slot]))

    @pl.when(i < nu_ref[0])
    def _():
        e = te_ref[i]
        prev = te_ref[jnp.maximum(i - 1, 0)]
        slot = slot_ref[i]

        @pl.when(i == 0)
        def _():
            for cp in fetch(e, slot):
                cp.start()

        @pl.when((i == 0) | (e != prev))
        def _():
            for cp in fetch(e, slot):
                cp.wait()
            nxt = nxt_ref[i]

            @pl.when(nxt >= 0)
            def _():
                for cp in fetch(nxt, 1 - slot):
                    cp.start()

            def cast_gu(r, c):
                rs = pl.ds(pl.multiple_of(r * 128, 128), 128)
                wgu_s[rs, :] = wgu_f[slot, rs, :].astype(BF)
                return c

            def cast_d(r, c):
                rs = pl.ds(pl.multiple_of(r * 128, 128), 128)
                wd_s[rs, :] = wd_f[slot, rs, :].astype(BF)
                return c

            lax.fori_loop(0, D // 128, cast_gu, 0)
            lax.fori_loop(0, D_FF // 128, cast_d, 0)

        x = _rows_from_tiles(x_ref, TM_MOE).astype(BF)
        for c in range(D_FF // FF_CHUNK):
            lo = c * FF_CHUNK
            gate = _dot(x, wgu_s[:, lo:lo + FF_CHUNK]) + bgu_ref[:, lo:lo + FF_CHUNK]
            up = (_dot(x, wgu_s[:, D_FF + lo:D_FF + lo + FF_CHUNK])
                  + bgu_ref[:, D_FF + lo:D_FF + lo + FF_CHUNK])
            gate = jnp.minimum(gate, SWIGLU_LIMIT)
            up = jnp.clip(up, -SWIGLU_LIMIT, SWIGLU_LIMIT)
            act = (up + 1.0) * (gate * jax.nn.sigmoid(SWIGLU_ALPHA * gate))
            act_s[:, lo:lo + FF_CHUNK] = act.astype(BF)
        for n in range(D // OUT_BLOCK):
            lo = n * OUT_BLOCK
            y = _dot(act_s[...], wd_s[:, lo:lo + OUT_BLOCK]) + bd_ref[:, lo:lo + OUT_BLOCK]
            for j in range(OUT_BLOCK // LANES):
                o_ref[pl.ds(lo // LANES + j, TM_MOE, stride=SUBLANES), :] = y[:, j * LANES:(j + 1) * LANES]


def _moe(l, tile_e, n_used, slot, nxt, xs, w_gu, b_gu, w_down, b_down):
    def tile(i, te, nu, *_):
        return (jnp.minimum(i, nu[0] - 1), 0)

    def bias(i, te, nu, *_):
        return (l, te[jnp.minimum(i, nu[0] - 1)], 0, 0)

    return pl.pallas_call(
        functools.partial(_moe_kernel, l),
        out_shape=jax.ShapeDtypeStruct((R_MOE * SUBLANES, LANES), F32),
        grid_spec=pltpu.PrefetchScalarGridSpec(
            num_scalar_prefetch=4,
            grid=(NT_MOE,),
            in_specs=[
                pl.BlockSpec((TM_MOE * SUBLANES, LANES), tile),
                pl.BlockSpec(memory_space=pl.ANY),
                pl.BlockSpec((None, None, 1, 2 * D_FF), bias),
                pl.BlockSpec(memory_space=pl.ANY),
                pl.BlockSpec((None, None, 1, D), bias),
            ],
            out_specs=pl.BlockSpec((TM_MOE * SUBLANES, LANES), lambda i, *_: (i, 0)),
            scratch_shapes=[pltpu.VMEM((2, D, 2 * D_FF), F32), pltpu.VMEM((2, D_FF, D), F32),
                            pltpu.VMEM((D, 2 * D_FF), BF), pltpu.VMEM((D_FF, D), BF),
                            pltpu.VMEM((TM_MOE, D_FF), BF), pltpu.SemaphoreType.DMA((2, 2))],
        ),
        compiler_params=pltpu.CompilerParams(
            dimension_semantics=("arbitrary",), vmem_limit_bytes=VMEM_LIMIT),
        name="moe_experts",
    )(tile_e, n_used, slot, nxt, xs, w_gu, b_gu.reshape(DEPTH, N_EXP, 1, 2 * D_FF), w_down,
      b_down.reshape(DEPTH, N_EXP, 1, D))


def _row_gather(table, idx):
    n_rows = idx.shape[0]
    per_worker = n_rows // SC_WORKERS
    n_chunks = per_worker // SC_CHUNK
    assert per_worker * SC_WORKERS == n_rows and n_chunks * SC_CHUNK == per_worker and n_chunks % 2 == 0
    mesh = plsc.VectorSubcoreMesh(core_axis_name="c", subcore_axis_name="s")
    row_tile = (SC_CHUNK, SUBLANES, LANES)

    @functools.partial(
        pl.kernel, mesh=mesh,
        out_type=jax.ShapeDtypeStruct((n_rows, SUBLANES, LANES), F32),
        scratch_types=[pltpu.VMEM((n_chunks, SC_CHUNK), jnp.int32),
                       pltpu.VMEM(row_tile, F32), pltpu.VMEM(row_tile, F32),
                       pltpu.SemaphoreType.DMA, pltpu.SemaphoreType.DMA],
        name="sc_row_gather",
    )
    def gather(table_hbm, idx_hbm, out_hbm, idx_v, buf0, buf1, sem0, sem1):
        wid = lax.axis_index("s") * SC_CORES + lax.axis_index("c")
        pltpu.sync_copy(idx_hbm.at[pl.ds(wid * n_chunks, n_chunks)], idx_v)
        base = wid * per_worker

        def fetch(chunk, buf, sem):
            return pltpu.make_async_copy(table_hbm.at[idx_v.at[chunk]], buf, sem)

        def emit(chunk, buf):
            pltpu.sync_copy(buf, out_hbm.at[pl.ds(base + chunk * SC_CHUNK, SC_CHUNK)])

        fetch(0, buf0, sem0).start()

        @pl.loop(0, n_chunks, step=2)
        def _(c):
            fetch(c + 1, buf1, sem1).start()
            fetch(c, buf0, sem0).wait()
            emit(c, buf0)

            @pl.when(c + 2 < n_chunks)
            def _():
                fetch(c + 2, buf0, sem0).start()

            fetch(c + 1, buf1, sem1).wait()
            emit(c + 1, buf1)

    out = gather(table.reshape(-1, SUBLANES, LANES), idx.reshape(n_rows // SC_CHUNK, SC_CHUNK))
    return out.reshape(n_rows * SUBLANES, LANES)


def _row_dispatch(rows, dest_t, n_out):
    per_worker = N_TOK // SC_WORKERS
    n_chunks = per_worker // SC_CHUNK
    assert per_worker * SC_WORKERS == N_TOK and n_chunks * SC_CHUNK == per_worker and n_chunks % 2 == 0
    idx_rows = TOP_K * n_chunks
    assert idx_rows % SUBLANES == 0
    mesh = plsc.VectorSubcoreMesh(core_axis_name="c", subcore_axis_name="s")
    row_tile = (SC_CHUNK, SUBLANES, LANES)

    @functools.partial(
        pl.kernel, mesh=mesh,
        out_type=jax.ShapeDtypeStruct((n_out, SUBLANES, LANES), F32),
        scratch_types=[pltpu.VMEM((idx_rows, SC_CHUNK), jnp.int32),
                       pltpu.VMEM(row_tile, F32), pltpu.VMEM(row_tile, F32),
                       pltpu.SemaphoreType.DMA, pltpu.SemaphoreType.DMA],
        name="sc_row_dispatch",
    )
    def dispatch(rows_hbm, idx_hbm, out_hbm, idx_v, buf0, buf1, sem0, sem1):
        wid = lax.axis_index("s") * SC_CORES + lax.axis_index("c")
        pltpu.sync_copy(idx_hbm.at[pl.ds(wid * idx_rows, idx_rows)], idx_v)
        base = wid * per_worker

        def fetch(chunk, buf, sem):
            return pltpu.make_async_copy(rows_hbm.at[pl.ds(base + chunk * SC_CHUNK, SC_CHUNK)], buf, sem)

        def emit(chunk, buf):
            for k in range(TOP_K):
                pltpu.sync_copy(buf, out_hbm.at[idx_v.at[k * n_chunks + chunk]])

        fetch(0, buf0, sem0).start()

        @pl.loop(0, n_chunks, step=2)
        def _(c):
            fetch(c + 1, buf1, sem1).start()
            fetch(c, buf0, sem0).wait()
            emit(c, buf0)

            @pl.when(c + 2 < n_chunks)
            def _():
                fetch(c + 2, buf0, sem0).start()

            fetch(c + 1, buf1, sem1).wait()
            emit(c + 1, buf1)

    idx = dest_t.reshape(TOP_K, SC_WORKERS, n_chunks, SC_CHUNK).transpose(1, 0, 2, 3)
    out = dispatch(rows.reshape(-1, SUBLANES, LANES), idx.reshape(SC_WORKERS * idx_rows, SC_CHUNK))
    return out.reshape(n_out * SUBLANES, LANES)


def _combine_kernel(first_tile, x1_ref, rt_ref, mod_ref, g0_ref, g1_ref, g2_ref, g3_ref, o_ref):
    t = pl.program_id(0)
    per_b = LT // TQ - first_tile
    b = t // per_b
    is_ctx = (t % per_b + first_tile) == 0
    route = rt_ref[0]
    y = jnp.zeros((TQ, D), F32)
    for k, g_ref in enumerate((g0_ref, g1_ref, g2_ref, g3_ref)):
        y = y + route[:, TOP_K + k:TOP_K + k + 1] * _rows_from_tiles(g_ref, TQ)
    gate = jnp.where(is_ctx, mod_ref[NB:NB + 1, 5 * D:6 * D], mod_ref[pl.ds(b, 1), 5 * D:6 * D])
    o_ref[0] = x1_ref[0] + gate * y


def _combine(l, x1, route, mod, gathered, latent_only):
    tiles_b = LT // TQ
    first_tile = L_CTX // TQ if latent_only else 0
    per_b = tiles_b - first_tile
    tile_of = lambda t: (t // per_b, t % per_b + first_tile)
    tok = lambda w: pl.BlockSpec((1, TQ, w), lambda t: tile_of(t) + (0,))
    slot = lambda k: pl.BlockSpec(
        (TQ * SUBLANES, LANES), lambda t: (k * NB * tiles_b + tile_of(t)[0] * tiles_b + tile_of(t)[1], 0))
    return pl.pallas_call(
        functools.partial(_combine_kernel, first_tile),
        out_shape=jax.ShapeDtypeStruct((NB, per_b * TQ, D), F32),
        grid=(NB * per_b,),
        in_specs=[tok(D), tok(LANES), _layer_spec(l, (16, N_MOD * D)),
                  slot(0), slot(1), slot(2), slot(3)],
        out_specs=pl.BlockSpec((1, TQ, D), lambda t: (t // per_b, t % per_b, 0)),
        compiler_params=pltpu.CompilerParams(
            dimension_semantics=("arbitrary",), vmem_limit_bytes=VMEM_LIMIT),
        name="moe_combine",
    )(x1, route, mod, gathered, gathered, gathered, gathered)


def _rope_tables():
    pos = jnp.arange(L_LAT)
    row = (pos // GRID_W).astype(F32)
    col = (pos % GRID_W).astype(F32)
    inv = ROPE_BASE ** (-jnp.arange(0, 32, 2, dtype=F32) / 32)
    ang_r = row[:, None] * inv
    ang_c = col[:, None] * inv
    ang = jnp.concatenate([ang_r, ang_r, ang_c, ang_c], axis=-1)
    cos = jnp.concatenate([jnp.ones((L_CTX, HEAD_DIM), F32), jnp.cos(ang)], axis=0)
    sin = jnp.concatenate([jnp.zeros((L_CTX, HEAD_DIM), F32), jnp.sin(ang)], axis=0)
    sign = jnp.where((jnp.arange(HEAD_DIM) % 32) < 16, -1.0, 1.0).astype(F32)
    return jnp.tile(cos, (1, 2)), jnp.tile(sin * sign, (1, 2))


def _s5_tables(a_re, a_im, log_dt, b_re, b_im, c_re, c_im):
    dt = jnp.exp(log_dt)[..., None]
    mag = jnp.exp(a_re * dt)
    ar = mag * jnp.cos(a_im * dt)
    ai = mag * jnp.sin(a_im * dt)
    den = a_re * a_re + a_im * a_im
    qr = ((ar - 1) * a_re + ai * a_im) / den
    qi = (ai * a_re - (ar - 1) * a_im) / den
    bbr = qr[..., None] * b_re - qi[..., None] * b_im
    bbi = qr[..., None] * b_im + qi[..., None] * b_re
    eye = jnp.eye(SSM_GROUPS, dtype=F32)
    to_in = lambda m: jnp.einsum('ldgnp,gh->ldgphn', m, eye).reshape(DEPTH, 2, SSM_W, NS)
    bm = jnp.concatenate([to_in(bbr), to_in(bbi)], axis=-1).astype(BF)
    to_out = lambda m: jnp.einsum('ldgpn,gh->ldgnhp', m, eye).reshape(DEPTH, 2, NS, SSM_W)
    cm = jnp.concatenate([to_out(c_re), -to_out(c_im)], axis=2).astype(BF)
    a_tab = jnp.stack([ar.reshape(DEPTH, 2, NS), ai.reshape(DEPTH, 2, NS)], axis=2)
    return a_tab, bm, cm


def _interleave_perms():
    rows = NB * T_PERM
    r = jnp.arange(rows)
    src = (r % NB) * T_PERM + r // NB
    pin = (src[:, None] == jnp.arange(rows)[None, :]).astype(BF)
    return pin, pin.T


def kernel(x, c, ctx, c_ctx, w_mod, b_mod, norm1_w, norm2_w, w_in, q_norm_w, k_norm_w, attn_sink,
           ssm_a_re, ssm_a_im, ssm_log_dt, ssm_b_re, ssm_b_im, ssm_c_re, ssm_c_im, ssm_d, glu_w, glu_b,
           pool_w, pool_scale, out_norm_w, w_out, router_w, router_b, exp_w_gu, exp_b_gu, exp_w_down,
           exp_b_down):
    xc = jnp.concatenate([ctx, x], axis=1)
    cc = jnp.concatenate([c, c_ctx[None, :], jnp.zeros((16 - NB - 1, D), F32)], axis=0)
    mod = _adaln(cc, w_mod, b_mod)
    cos2, sin2 = _rope_tables()
    pin, pout = _interleave_perms()
    tri = (jnp.arange(TM_TOK)[:, None] > jnp.arange(TM_TOK)[None, :]).astype(BF)

    row = lambda a: a.reshape(DEPTH, 1, -1)
    dup = lambda m: jnp.concatenate([m[..., :64], m[..., :64], m[..., 64:], m[..., 64:]], axis=-1)
    w_ext = jnp.concatenate([w_in[..., :512], dup(w_in[..., 512:640]), dup(w_in[..., 640:768]),
                             w_in[..., 768:]], axis=-1).astype(BF)
    qw2 = row(jnp.tile(q_norm_w, (1, 2)))
    kw2 = row(jnp.tile(k_norm_w, (1, 2)))
    a_tab, bm, cm = _s5_tables(ssm_a_re, ssm_a_im, ssm_log_dt, ssm_b_re, ssm_b_im, ssm_c_re, ssm_c_im)
    pw_bd = jnp.einsum('lgcd,gh->lgchd', pool_w, jnp.eye(4, dtype=F32)).reshape(DEPTH, POOL_W, POOL_W).astype(BF)
    rw = jnp.concatenate([router_w, jnp.zeros((DEPTH, D, LANES - N_EXP), F32)], axis=-1)
    rb = row(jnp.concatenate([router_b, jnp.full((DEPTH, LANES - N_EXP), -1e30, F32)], axis=-1))
    glu_w_bf = glu_w.astype(BF)
    w_out_bf = w_out.astype(BF)
    n1, n2, onw = row(norm1_w), row(norm2_w), row(out_norm_w)
    ssm_d3, glu_b3, pool_sc3 = row(ssm_d), row(glu_b), row(pool_scale)
    sink = attn_sink.reshape(-1)

    for l in range(DEPTH):
        q, kd, vd, u, p = _inproj(l, xc, mod, n1, w_ext, qw2, kw2, cos2, sin2)
        attn = _attention(l, sink, q, kd, vd, onw)
        yf, yb = _s5_scan(l, u, pin, pout, a_tab, bm, cm)
        pool = _pool(l, p, pw_bd, pool_sc3)
        x1, h2, route, cnt = _mix(l, xc, attn, yf, yb, u, pool, mod, ssm_d3, glu_w_bf, glu_b3, onw,
                                  w_out_bf, n2, rw, rb, tri)

        route = route.reshape(N_TOK, LANES)
        top_i = route[:, 0:TOP_K].astype(jnp.int32)
        rank = route[:, 2 * TOP_K:3 * TOP_K].astype(jnp.int32)
        counts = cnt[0, :N_EXP].astype(jnp.int32)
        padded = (counts + TM_MOE - 1) // TM_MOE * TM_MOE
        pend = jnp.cumsum(padded)
        pstart = pend - padded
        dest = pstart[top_i] + rank
        dest_t = dest.T
        n_used = (pend[-1] // TM_MOE).astype(jnp.int32)[None]
        tile_lo = jnp.arange(NT_MOE, dtype=jnp.int32) * TM_MOE
        tile_e = jnp.minimum(jnp.sum(pend[None, :] <= tile_lo[:, None], axis=1), N_EXP - 1).astype(jnp.int32)
        change = jnp.concatenate([jnp.ones((1,), jnp.int32), (tile_e[1:] != tile_e[:-1]).astype(jnp.int32)])
        slot = (jnp.cumsum(change) - 1) % 2
        next_first = (pend // TM_MOE)[tile_e]
        nxt = jnp.where(next_first < n_used[0], tile_e[jnp.minimum(next_first, NT_MOE - 1)], -1)
        xs = _row_dispatch(h2, dest_t, R_MOE)

        yb_rows = _moe(l, tile_e, n_used, slot.astype(jnp.int32), nxt.astype(jnp.int32), xs,
                       exp_w_gu, exp_b_gu, exp_w_down, exp_b_down)
        gathered = _row_gather(yb_rows, dest_t.reshape(-1))
        xc = _combine(l, x1, route.reshape(NB, LT, LANES), mod, gathered, latent_only=(l == DEPTH - 1))
    return xc
```

```python
import functools
import math

import jax
import jax.numpy as jnp
from jax import lax
from jax.experimental import pallas as pl
from jax.experimental.pallas import tpu as pltpu
from jax.experimental.pallas import tpu_sc as plsc

D = 1024
NB = 8
L_LAT = 2048
L_CTX = 256
LT = L_CTX + L_LAT
DEPTH = 4
N_MOD = 6
EPS = 1e-6
N_HEADS = 8
HEAD_DIM = 64
ATTN_W = 512
WINDOW = 128
ATTN_SCALE = HEAD_DIM ** -0.5
LOG2E = math.log2(math.e)
ROPE_BASE = 10000.0
GRID_W = 64
SSM_W = 256
SSM_GROUP = 16
SSM_GROUPS = 16
SSM_STATE = 64
NS = SSM_GROUPS * SSM_STATE
POOL_W = 256
POOL_GROUP = 64
N_EXP = 32
TOP_K = 4
D_FF = 1024
SWIGLU_LIMIT = 7.0
SWIGLU_ALPHA = 1.702

LANES = 128
SUBLANES = 8
VMEM_LIMIT = 56 * 1024 * 1024

TM_TOK = 768
TQ = 256
KSPAN = TQ + 2 * WINDOW
T_SCAN = 128
T_PERM = 32
N_SUB = T_SCAN // T_PERM
N_CHUNK = LT // T_SCAN
N_CHUNK_CTX = L_CTX // T_SCAN
TM_MOE = 512
N_TOK = NB * LT
N_ASSIGN = N_TOK * TOP_K
NT_MOE = N_ASSIGN // TM_MOE + N_EXP
R_MOE = NT_MOE * TM_MOE
FF_CHUNK = 512
OUT_BLOCK = 256

SC_CORES = 2
SC_SUBCORES = 16
SC_WORKERS = SC_CORES * SC_SUBCORES
SC_CHUNK = 32

C_Q = 0
C_K = 512
C_V = 768
C_U = 1024
C_P = 1280
NW_IN = 1536

BF = jnp.bfloat16
F32 = jnp.float32


def _split_bf16(a):
    hi = a.astype(BF)
    lo = (a - hi.astype(F32)).astype(BF)
    return hi, lo


def _dot(a, b):
    return jnp.dot(a, b, preferred_element_type=F32)


def _dot3(a, b):
    ah, al = _split_bf16(a)
    bh, bl = _split_bf16(b)
    return _dot(ah, bh) + (_dot(ah, bl) + _dot(al, bh))


def _mod_kernel(c_ref, w_ref, b_ref, o_ref):
    c = c_ref[...]
    s = c * jax.nn.sigmoid(c)
    o_ref[0] = _dot3(s, w_ref[0]) + b_ref[0]


def _adaln(cc, w_mod, b_mod):
    tn = 1536
    return pl.pallas_call(
        _mod_kernel,
        out_shape=jax.ShapeDtypeStruct((DEPTH, 16, N_MOD * D), F32),
        grid=(DEPTH, N_MOD * D // tn),
        in_specs=[
            pl.BlockSpec((16, D), lambda l, j: (0, 0)),
            pl.BlockSpec((1, D, tn), lambda l, j: (l, 0, j)),
            pl.BlockSpec((1, 1, tn), lambda l, j: (l, 0, j)),
        ],
        out_specs=pl.BlockSpec((1, 16, tn), lambda l, j: (l, 0, j)),
        compiler_params=pltpu.CompilerParams(
            dimension_semantics=("arbitrary", "arbitrary"), vmem_limit_bytes=VMEM_LIMIT),
        name="adaln_mod",
    )(cc, w_mod, b_mod.reshape(DEPTH, 1, N_MOD * D))


def _layer_spec(l, shape, col_block=0):
    shape = tuple(shape)
    return pl.BlockSpec((None,) + shape, lambda *_: (l,) + (0,) * (len(shape) - 1) + (col_block,))


def _with_mod(fn, mod_ref, b, i, idxs, *arrays):
    lat = [mod_ref[pl.ds(b, 1), k * D:(k + 1) * D] for k in idxs]
    top = [jnp.where(i == 0, mod_ref[NB:NB + 1, k * D:(k + 1) * D], v) for k, v in zip(idxs, lat)]
    return jnp.concatenate([fn(*[a[:L_CTX] for a in arrays], *top),
                            fn(*[a[L_CTX:] for a in arrays], *lat)], axis=0)


def _rmsnorm(x, w):
    ms = jnp.mean(x * x, axis=-1, keepdims=True)
    return x * lax.rsqrt(ms + EPS) * w


def _headnorm_pair(t, w2, lane):
    sq = t * t
    first = lane < HEAD_DIM
    a = jnp.sum(jnp.where(first, sq, 0.0), axis=-1, keepdims=True)
    b = jnp.sum(jnp.where(first, 0.0, sq), axis=-1, keepdims=True)
    ms = jnp.where(first, a, b) * (1.0 / HEAD_DIM)
    return t * lax.rsqrt(ms + EPS) * w2


def _rope_pair(t, cos, sin_signed, lane):
    lower = (lane % 32) < 16
    partner = jnp.where(lower, pltpu.roll(t, LANES - 16, 1), pltpu.roll(t, 16, 1))
    return t * cos + partner * sin_signed


def _inproj_kernel(x_ref, mod_ref, n1_ref, w_ref, qw_ref, kw_ref, cos_ref, sin_ref,
                   q_ref, k_ref, v_ref, u_ref, p_ref):
    b = pl.program_id(0)
    i = pl.program_id(1)
    h = _rmsnorm(x_ref[0], n1_ref[...])
    h = _with_mod(lambda hr, shift, scale: (hr * (1.0 + scale) + shift).astype(BF), mod_ref, b, i, (0, 1), h)
    px = _dot(h, w_ref[...])
    lane = lax.broadcasted_iota(jnp.int32, (TM_TOK, LANES), 1)
    cos = cos_ref[...]
    sin = sin_ref[...]
    for j in range(ATTN_W // LANES):
        t = px[:, C_Q + j * LANES:C_Q + (j + 1) * LANES]
        t = _rope_pair(_headnorm_pair(t, qw_ref[...], lane), cos, sin, lane) * (ATTN_SCALE * LOG2E)
        q_ref[0, :, j * LANES:(j + 1) * LANES] = t.astype(BF)
    for g in range(2):
        t = px[:, C_K + g * LANES:C_K + (g + 1) * LANES]
        t = _rope_pair(_headnorm_pair(t, kw_ref[...], lane), cos, sin, lane)
        k_ref[0, :, g * LANES:(g + 1) * LANES] = t.astype(BF)
        t = px[:, C_V + g * LANES:C_V + (g + 1) * LANES]
        v_ref[0, :, g * LANES:(g + 1) * LANES] = jnp.where(lane < HEAD_DIM, t, 1.0).astype(BF)
    u_ref[0] = px[:, C_U:C_P]
    p_ref[0] = px[:, C_P:NW_IN]


def _inproj(l, xc, mod, n1, w_ext, qw2, kw2, cos2, sin2):
    tok = lambda w: pl.BlockSpec((1, TM_TOK, w), lambda b, i: (b, i, 0))
    full = lambda s: _layer_spec(l, s)
    return pl.pallas_call(
        _inproj_kernel,
        out_shape=(
            jax.ShapeDtypeStruct((NB, LT, ATTN_W), BF),
            jax.ShapeDtypeStruct((NB, LT, 256), BF),
            jax.ShapeDtypeStruct((NB, LT, 256), BF),
            jax.ShapeDtypeStruct((NB, LT, SSM_W), F32),
            jax.ShapeDtypeStruct((NB, LT, POOL_W), F32),
        ),
        grid=(NB, LT // TM_TOK),
        in_specs=[
            tok(D), full((16, N_MOD * D)), full((1, D)), full((D, NW_IN)),
            full((1, LANES)), full((1, LANES)),
            pl.BlockSpec((TM_TOK, LANES), lambda b, i: (i, 0)),
            pl.BlockSpec((TM_TOK, LANES), lambda b, i: (i, 0)),
        ],
        out_specs=(tok(ATTN_W), tok(256), tok(256), tok(SSM_W), tok(POOL_W)),
        compiler_params=pltpu.CompilerParams(
            dimension_semantics=("arbitrary", "arbitrary"), vmem_limit_bytes=VMEM_LIMIT),
        name="inproj",
    )(xc, mod, n1, w_ext, qw2, kw2, cos2, sin2)


def _attn_kernel(l, sink_ref, q_ref, k_ref, v_ref, onw_ref, o_ref, bias_ref):
    j = pl.program_id(1)
    start = pl.multiple_of(jnp.clip(j * TQ - WINDOW, LANES, LT - KSPAN), LANES)
    lane = lax.broadcasted_iota(jnp.int32, (TQ, LANES), 1)
    first = lane < HEAD_DIM
    row2 = lax.broadcasted_iota(jnp.int32, (2 * TQ, 1), 0)
    qpos = j * TQ + jnp.where(row2 < TQ, row2, row2 - TQ)
    kpos = start + lax.broadcasted_iota(jnp.int32, (1, KSPAN), 1)
    valid = (jnp.abs(qpos - kpos) <= WINDOW) & (kpos >= L_CTX) & (j >= 1)
    bias_ref[...] = jnp.where(valid, 0.0, -1e30)
    n_pairs = N_HEADS // 2

    def kv(ref, hp, rows):
        g = hp // 2
        return ref[0, rows, g * LANES:(g + 1) * LANES]

    def scores(hp):
        q2 = q_ref[0, :, hp * LANES:(hp + 1) * LANES]
        zero = jnp.zeros_like(q2)
        qs = jnp.concatenate([jnp.where(first, q2, zero), jnp.where(first, zero, q2)], axis=0)
        dn = (((1,), (1,)), ((), ()))
        s_ctx = lax.dot_general(qs, kv(k_ref, hp, slice(0, L_CTX)), dn, preferred_element_type=F32)
        s_loc = lax.dot_general(qs, kv(k_ref, hp, pl.ds(start, KSPAN)), dn, preferred_element_type=F32)
        return s_loc, s_ctx

    nxt = scores(0)
    outs = []
    for hp in range(n_pairs):
        s_loc, s_ctx = nxt
        if hp + 1 < n_pairs:
            nxt = scores(hp + 1)
        s_loc = s_loc + bias_ref[...]
        sink = jnp.where(row2 < TQ, sink_ref[l * N_HEADS + 2 * hp],
                         sink_ref[l * N_HEADS + 2 * hp + 1]) * LOG2E
        m = jnp.maximum(jnp.maximum(jnp.max(s_loc, axis=-1, keepdims=True),
                                    jnp.max(s_ctx, axis=-1, keepdims=True)), sink)
        e_loc = jnp.exp2(s_loc - m)
        e_ctx = jnp.exp2(s_ctx - m)
        o = (_dot(e_loc.astype(BF), kv(v_ref, hp, pl.ds(start, KSPAN)))
             + _dot(e_ctx.astype(BF), kv(v_ref, hp, slice(0, L_CTX))))
        den = pltpu.roll(o + jnp.exp2(sink - m), HEAD_DIM, 1)
        o = o / den
        outs.append(jnp.where(first, o[0:TQ], pltpu.roll(o[TQ:2 * TQ], HEAD_DIM, 1)))
    o_ref[0] = _rmsnorm(jnp.concatenate(outs, axis=1), onw_ref[...]).astype(BF)


def _attention(l, sink, q, kd, vd, onw):
    return pl.pallas_call(
        functools.partial(_attn_kernel, l),
        out_shape=jax.ShapeDtypeStruct((NB, LT, ATTN_W), BF),
        grid_spec=pltpu.PrefetchScalarGridSpec(
            num_scalar_prefetch=1,
            grid=(NB, LT // TQ),
            in_specs=[
                pl.BlockSpec((1, TQ, ATTN_W), lambda b, j, s: (b, j, 0)),
                pl.BlockSpec((1, LT, 256), lambda b, j, s: (b, 0, 0)),
                pl.BlockSpec((1, LT, 256), lambda b, j, s: (b, 0, 0)),
                _layer_spec(l, (1, ATTN_W)),
            ],
            out_specs=pl.BlockSpec((1, TQ, ATTN_W), lambda b, j, s: (b, j, 0)),
            scratch_shapes=[pltpu.VMEM((2 * TQ, KSPAN), F32)],
        ),
        compiler_params=pltpu.CompilerParams(
            dimension_semantics=("arbitrary", "arbitrary"), vmem_limit_bytes=VMEM_LIMIT),
        name="window_attn",
    )(sink, q, kd, vd, onw)


def _bwd_chunk(i):
    return jnp.where(i < N_CHUNK_CTX, N_CHUNK_CTX - 1 - i, N_CHUNK - 1 - (i - N_CHUNK_CTX))


def _scan_kernel(uf_ref, ub_ref, pin_ref, pout_ref, a_ref, bm_ref, cm_ref, yf_ref, yb_ref,
                 xf_ref, xb_ref, st_ref):
    i = pl.program_id(0)
    sub_rows = NB * T_PERM

    @pl.when(i == 0)
    def _():
        st_ref[...] = jnp.zeros_like(st_ref)

    def project(d, u_ref, xs_ref):
        ui = jnp.concatenate(
            [_dot(pin_ref[...], u_ref[:, s * T_PERM:(s + 1) * T_PERM, :].reshape(sub_rows, SSM_W).astype(BF))
             for s in range(N_SUB)], axis=0).astype(BF)
        xs_ref[...] = _dot(ui, bm_ref[d])

    def scan(d, xs_ref, reverse):
        ar = jnp.broadcast_to(a_ref[d, 0:1, :], (NB, NS))
        ai = jnp.broadcast_to(a_ref[d, 1:2, :], (NB, NS))
        sr = st_ref[d, 0]
        si = st_ref[d, 1]
        for t in (range(T_SCAN - 1, -1, -1) if reverse else range(T_SCAN)):
            r = pl.ds(t * NB, NB)
            nr = ar * sr - ai * si + xs_ref[r, 0:NS]
            ni = ar * si + ai * sr + xs_ref[r, NS:2 * NS]
            sr, si = nr, ni
            xs_ref[r, 0:NS] = sr
            xs_ref[r, NS:2 * NS] = si
        st_ref[d, 0] = sr
        st_ref[d, 1] = si

    def readout(d, xs_ref, y_ref):
        y = _dot(xs_ref[...].astype(BF), cm_ref[d])
        yh, yl = _split_bf16(y)
        for s in range(N_SUB):
            rs = slice(s * sub_rows, (s + 1) * sub_rows)
            ys = _dot(pout_ref[...], yh[rs]) + _dot(pout_ref[...], yl[rs])
            y_ref[:, s * T_PERM:(s + 1) * T_PERM, :] = ys.reshape(NB, T_PERM, SSM_W)

    project(0, uf_ref, xf_ref)
    project(1, ub_ref, xb_ref)
    scan(0, xf_ref, False)
    readout(0, xf_ref, yf_ref)
    scan(1, xb_ref, True)
    readout(1, xb_ref, yb_ref)


def _s5_scan(l, u, pin, pout, a_tab, bm, cm):
    rows = NB * T_SCAN
    full = lambda s: pl.BlockSpec(s, lambda i: (0,) * len(s))
    layer = lambda s: _layer_spec(l, s)
    chunk_f = pl.BlockSpec((NB, T_SCAN, SSM_W), lambda i: (0, i, 0))
    chunk_b = pl.BlockSpec((NB, T_SCAN, SSM_W), lambda i: (0, _bwd_chunk(i), 0))
    return pl.pallas_call(
        _scan_kernel,
        out_shape=(jax.ShapeDtypeStruct((NB, LT, SSM_W), F32),
                   jax.ShapeDtypeStruct((NB, LT, SSM_W), F32)),
        grid=(N_CHUNK,),
        in_specs=[chunk_f, chunk_b, full((NB * T_PERM, NB * T_PERM)), full((NB * T_PERM, NB * T_PERM)),
                  layer((2, 2, NS)), layer((2, SSM_W, 2 * NS)), layer((2, 2 * NS, SSM_W))],
        out_specs=(chunk_f, chunk_b),
        scratch_shapes=[pltpu.VMEM((rows, 2 * NS), F32), pltpu.VMEM((rows, 2 * NS), F32),
                        pltpu.VMEM((2, 2, NB, NS), F32)],
        compiler_params=pltpu.CompilerParams(
            dimension_semantics=("arbitrary",), vmem_limit_bytes=VMEM_LIMIT),
        name="s5_scan",
    )(u, u, pin, pout, a_tab, bm, cm)


def _pool_segment(ps, length):
    n = length + 2 * SUBLANES
    z = jnp.zeros((SUBLANES, POOL_W), F32)
    pe = jnp.concatenate([z, ps, z], axis=0)
    a1 = pe + pltpu.roll(pe, 1, 0)
    a2 = a1 + pltpu.roll(a1, 2, 0)
    a3 = a2 + pltpu.roll(a2, 4, 0)
    a4 = a3 + pltpu.roll(a3, 8, 0)
    lane = lax.broadcasted_iota(jnp.int32, (1, POOL_W), 1)
    half = jnp.where(lane < 64, 1, jnp.where(lane < 128, 2, jnp.where(lane < 192, 4, 8)))
    s = jnp.where(lane < 64, a1,
                  jnp.where(lane < 128, pltpu.roll(a2, n - 1, 0),
                            jnp.where(lane < 192, pltpu.roll(a3, n - 3, 0), pltpu.roll(a4, n - 7, 0))))
    s = s[SUBLANES:SUBLANES + length]
    t = lax.broadcasted_iota(jnp.int32, (length, 1), 0)
    cnt = jnp.minimum(t + half, length) - jnp.maximum(t - half, 0)
    return s / cnt.astype(F32) - ps


def _pool_kernel(p_ref, w_ref, sc_ref, o_ref):
    for lo, length in ((0, L_CTX), (L_CTX, L_LAT)):
        dlt = _pool_segment(p_ref[0, lo:lo + length, :], length)
        y = _dot(dlt.astype(BF), w_ref[...]) * sc_ref[...]
        o_ref[0, lo:lo + length, :] = y.astype(BF)


def _pool(l, p, w_bd, scale):
    return pl.pallas_call(
        _pool_kernel,
        out_shape=jax.ShapeDtypeStruct((NB, LT, POOL_W), BF),
        grid=(NB,),
        in_specs=[pl.BlockSpec((1, LT, POOL_W), lambda b: (b, 0, 0)),
                  _layer_spec(l, (POOL_W, POOL_W)),
                  _layer_spec(l, (1, POOL_W))],
        out_specs=pl.BlockSpec((1, LT, POOL_W), lambda b: (b, 0, 0)),
        compiler_params=pltpu.CompilerParams(
            dimension_semantics=("arbitrary",), vmem_limit_bytes=VMEM_LIMIT),
        name="pool",
    )(p, w_bd, scale)


def _mix_kernel(x_ref, at_ref, yf_ref, yb_ref, u_ref, pl_ref, mod_ref, d_ref, gw_ref, gb_ref, onw_ref,
                wo_ref, n2_ref, rw_ref, rb_ref, tri_ref,
                x1_ref, h2_ref, rt_ref, cnt_ref, run_ref):
    b = pl.program_id(0)
    i = pl.program_id(1)

    @pl.when((b == 0) & (i == 0))
    def _():
        run_ref[...] = jnp.zeros_like(run_ref)

    y = yf_ref[0] + yb_ref[0] + d_ref[...] * u_ref[0]
    g = jax.nn.gelu(y, approximate=True)
    s = g * jax.nn.sigmoid(_dot(g.astype(BF), gw_ref[...]) + gb_ref[...])
    s = _rmsnorm(s, onw_ref[...]).astype(BF)
    mix = (_dot(at_ref[0], wo_ref[0:ATTN_W, :])
           + _dot(s, wo_ref[ATTN_W:ATTN_W + SSM_W, :])
           + _dot(pl_ref[0], wo_ref[ATTN_W + SSM_W:D, :]))
    x1 = _with_mod(lambda xr, mr, gate: xr + gate * mr, mod_ref, b, i, (2,), x_ref[0], mix)
    x1_ref[0] = x1
    h2 = _rmsnorm(x1, n2_ref[...])
    h2 = _with_mod(lambda hr, shift, scale: hr * (1.0 + scale) + shift, mod_ref, b, i, (3, 4), h2)
    for j in range(D // LANES):
        h2_ref[pl.ds(j, TM_TOK, stride=SUBLANES), :] = h2[:, j * LANES:(j + 1) * LANES]
    logits = _dot3(h2, rw_ref[...]) + rb_ref[...]
    lane = lax.broadcasted_iota(jnp.int32, (TM_TOK, LANES), 1).astype(F32)
    vals, idxs, hits = [], [], []
    cur = logits
    for _ in range(TOP_K):
        m = jnp.max(cur, axis=-1, keepdims=True)
        idx = jnp.min(jnp.where(cur == m, lane, float(LANES)), axis=-1, keepdims=True)
        hit = lane == idx
        vals.append(m)
        idxs.append(idx)
        hits.append(hit)
        cur = jnp.where(hit, -jnp.inf, cur)
    ex = [jnp.exp(v - vals[0]) for v in vals]
    den = ex[0] + ex[1] + ex[2] + ex[3]
    onehot = jnp.where(hits[0] | hits[1] | hits[2] | hits[3], 1.0, 0.0)
    before = _dot(tri_ref[...], onehot.astype(BF)) + run_ref[...]
    route = jnp.zeros((TM_TOK, LANES), F32)
    for k in range(TOP_K):
        rank = jnp.sum(jnp.where(hits[k], before, 0.0), axis=-1, keepdims=True)
        route = jnp.where(lane == float(k), idxs[k], route)
        route = jnp.where(lane == float(TOP_K + k), ex[k] / den, route)
        route = jnp.where(lane == float(2 * TOP_K + k), rank, route)
    rt_ref[0] = route
    run_ref[...] = run_ref[...] + jnp.sum(onehot, axis=0, keepdims=True)
    cnt_ref[...] = jnp.broadcast_to(run_ref[...], (SUBLANES, LANES))


def _mix(l, xc, attn, yf, yb, u, pool, mod, ssm_d, glu_w, glu_b, onw, w_out, n2, rw, rb, tri):
    tok = lambda w: pl.BlockSpec((1, TM_TOK, w), lambda b, i: (b, i, 0))
    full = lambda s: pl.BlockSpec(s, lambda b, i: (0,) * len(s))
    layer = lambda s, col=0: _layer_spec(l, s, col)
    return pl.pallas_call(
        _mix_kernel,
        out_shape=(jax.ShapeDtypeStruct((NB, LT, D), F32),
                   jax.ShapeDtypeStruct((N_TOK * SUBLANES, LANES), F32),
                   jax.ShapeDtypeStruct((NB, LT, LANES), F32),
                   jax.ShapeDtypeStruct((SUBLANES, LANES), F32)),
        grid=(NB, LT // TM_TOK),
        in_specs=[tok(D), tok(ATTN_W), tok(SSM_W), tok(SSM_W), tok(SSM_W), tok(POOL_W),
                  layer((16, N_MOD * D)), layer((1, SSM_W)), layer((SSM_W, SSM_W)), layer((1, SSM_W)),
                  layer((1, SSM_W), ATTN_W // SSM_W), layer((D, D)), layer((1, D)), layer((D, LANES)),
                  layer((1, LANES)), full((TM_TOK, TM_TOK))],
        out_specs=(tok(D),
                   pl.BlockSpec((TM_TOK * SUBLANES, LANES), lambda b, i: (b * (LT // TM_TOK) + i, 0)),
                   tok(LANES), full((SUBLANES, LANES))),
        scratch_shapes=[pltpu.VMEM((1, LANES), F32)],
        compiler_params=pltpu.CompilerParams(
            dimension_semantics=("arbitrary", "arbitrary"), vmem_limit_bytes=VMEM_LIMIT),
        name="mix_router",
    )(xc, attn, yf, yb, u, pool, mod, ssm_d, glu_w, glu_b, onw, w_out, n2, rw, rb, tri)


def _rows_from_tiles(ref, rows):
    return jnp.concatenate(
        [ref[pl.ds(j, rows, stride=SUBLANES), :] for j in range(D // LANES)], axis=1)


def _rows_to_tiles(ref, val, rows):
    for j in range(D // LANES):
        ref[pl.ds(j, rows, stride=SUBLANES), :] = val[:, j * LANES:(j + 1) * LANES]


def _moe_kernel(l, te_ref, nu_ref, slot_ref, nxt_ref, x_ref, wgu_hbm, bgu_ref, wd_hbm, bd_ref, o_ref,
                wgu_f, wd_f, wgu_s, wd_s, act_s, sem):
    i = pl.program_id(0)

    @pl.when(i >= nu_ref[0])
    def _():
        o_ref[...] = jnp.zeros_like(o_ref)

    def fetch(e, slot):
        return (pltpu.make_async_copy(wgu_hbm.at[l, e], wgu_f.at[slot], sem.at[0, slot]),
                pltpu.make_async_copy(wd_hbm.at[l, e], wd_f.at[slot], sem.at[1, slot]))

    @pl.when(i < nu_ref[0])
    def _():
        e = te_ref[i]
        prev = te_ref[jnp.maximum(i - 1, 0)]
        slot = slot_ref[i]

        @pl.when(i == 0)
        def _():
            for cp in fetch(e, slot):
                cp.start()

        @pl.when((i == 0) | (e != prev))
        def _():
            for cp in fetch(e, slot):
                cp.wait()
            nxt = nxt_ref[i]

            @pl.when(nxt >= 0)
            def _():
                for cp in fetch(nxt, 1 - slot):
                    cp.start()

            def cast_gu(r, c):
                rs = pl.ds(pl.multiple_of(r * 128, 128), 128)
                wgu_s[rs, :] = wgu_f[slot, rs, :].astype(BF)
                return c

            def cast_d(r, c):
                rs = pl.ds(pl.multiple_of(r * 128, 128), 128)
                wd_s[rs, :] = wd_f[slot, rs, :].astype(BF)
                return c

            lax.fori_loop(0, D // 128, cast_gu, 0)
            lax.fori_loop(0, D_FF // 128, cast_d, 0)

        x = _rows_from_tiles(x_ref, TM_MOE).astype(BF)
        for c in range(D_FF // FF_CHUNK):
            lo = c * FF_CHUNK
            gate = _dot(x, wgu_s[:, lo:lo + FF_CHUNK]) + bgu_ref[:, lo:lo + FF_CHUNK]
            up = (_dot(x, wgu_s[:, D_FF + lo:D_FF + lo + FF_CHUNK])
                  + bgu_ref[:, D_FF + lo:D_FF + lo + FF_CHUNK])
            gate = jnp.minimum(gate, SWIGLU_LIMIT)
            up = jnp.clip(up, -SWIGLU_LIMIT, SWIGLU_LIMIT)
            act = (up + 1.0) * (gate * jax.nn.sigmoid(SWIGLU_ALPHA * gate))
            act_s[:, lo:lo + FF_CHUNK] = act.astype(BF)
        for n in range(D // OUT_BLOCK):
            lo = n * OUT_BLOCK
            y = _dot(act_s[...], wd_s[:, lo:lo + OUT_BLOCK]) + bd_ref[:, lo:lo + OUT_BLOCK]
            for j in range(OUT_BLOCK // LANES):
                o_ref[pl.ds(lo // LANES + j, TM_MOE, stride=SUBLANES), :] = y[:, j * LANES:(j + 1) * LANES]


def _moe(l, tile_e, n_used, slot, nxt, xs, w_gu, b_gu, w_down, b_down):
    def tile(i, te, nu, *_):
        return (jnp.minimum(i, nu[0] - 1), 0)

    def bias(i, te, nu, *_):
        return (l, te[jnp.minimum(i, nu[0] - 1)], 0, 0)

    return pl.pallas_call(
        functools.partial(_moe_kernel, l),
        out_shape=jax.ShapeDtypeStruct((R_MOE * SUBLANES, LANES), F32),
        grid_spec=pltpu.PrefetchScalarGridSpec(
            num_scalar_prefetch=4,
            grid=(NT_MOE,),
            in_specs=[
                pl.BlockSpec((TM_MOE * SUBLANES, LANES), tile),
                pl.BlockSpec(memory_space=pl.ANY),
                pl.BlockSpec((None, None, 1, 2 * D_FF), bias),
                pl.BlockSpec(memory_space=pl.ANY),
                pl.BlockSpec((None, None, 1, D), bias),
            ],
            out_specs=pl.BlockSpec((TM_MOE * SUBLANES, LANES), lambda i, *_: (i, 0)),
            scratch_shapes=[pltpu.VMEM((2, D, 2 * D_FF), F32), pltpu.VMEM((2, D_FF, D), F32),
                            pltpu.VMEM((D, 2 * D_FF), BF), pltpu.VMEM((D_FF, D), BF),
                            pltpu.VMEM((TM_MOE, D_FF), BF), pltpu.SemaphoreType.DMA((2, 2))],
        ),
        compiler_params=pltpu.CompilerParams(
            dimension_semantics=("arbitrary",), vmem_limit_bytes=VMEM_LIMIT),
        name="moe_experts",
    )(tile_e, n_used, slot, nxt, xs, w_gu, b_gu.reshape(DEPTH, N_EXP, 1, 2 * D_FF), w_down,
      b_down.reshape(DEPTH, N_EXP, 1, D))


def _row_gather(table, idx):
    n_rows = idx.shape[0]
    per_worker = n_rows // SC_WORKERS
    n_chunks = per_worker // SC_CHUNK
    assert per_worker * SC_WORKERS == n_rows and n_chunks * SC_CHUNK == per_worker and n_chunks % 2 == 0
    mesh = plsc.VectorSubcoreMesh(core_axis_name="c", subcore_axis_name="s")
    row_tile = (SC_CHUNK, SUBLANES, LANES)

    @functools.partial(
        pl.kernel, mesh=mesh,
        out_type=jax.ShapeDtypeStruct((n_rows, SUBLANES, LANES), F32),
        scratch_types=[pltpu.VMEM((n_chunks, SC_CHUNK), jnp.int32),
                       pltpu.VMEM(row_tile, F32), pltpu.VMEM(row_tile, F32),
                       pltpu.SemaphoreType.DMA, pltpu.SemaphoreType.DMA],
        name="sc_row_gather",
    )
    def gather(table_hbm, idx_hbm, out_hbm, idx_v, buf0, buf1, sem0, sem1):
        wid = lax.axis_index("s") * SC_CORES + lax.axis_index("c")
        pltpu.sync_copy(idx_hbm.at[pl.ds(wid * n_chunks, n_chunks)], idx_v)
        base = wid * per_worker

        def fetch(chunk, buf, sem):
            return pltpu.make_async_copy(table_hbm.at[idx_v.at[chunk]], buf, sem)

        def emit(chunk, buf):
            pltpu.sync_copy(buf, out_hbm.at[pl.ds(base + chunk * SC_CHUNK, SC_CHUNK)])

        fetch(0, buf0, sem0).start()

        @pl.loop(0, n_chunks, step=2)
        def _(c):
            fetch(c + 1, buf1, sem1).start()
            fetch(c, buf0, sem0).wait()
            emit(c, buf0)

            @pl.when(c + 2 < n_chunks)
            def _():
                fetch(c + 2, buf0, sem0).start()

            fetch(c + 1, buf1, sem1).wait()
            emit(c + 1, buf1)

    out = gather(table.reshape(-1, SUBLANES, LANES), idx.reshape(n_rows // SC_CHUNK, SC_CHUNK))
    return out.reshape(n_rows * SUBLANES, LANES)


def _row_dispatch(rows, dest_t, n_out):
    per_worker = N_TOK // SC_WORKERS
    n_chunks = per_worker // SC_CHUNK
    assert per_worker * SC_WORKERS == N_TOK and n_chunks * SC_CHUNK == per_worker and n_chunks % 2 == 0
    idx_rows = TOP_K * n_chunks
    assert idx_rows % SUBLANES == 0
    mesh = plsc.VectorSubcoreMesh(core_axis_name="c", subcore_axis_name="s")
    row_tile = (SC_CHUNK, SUBLANES, LANES)

    @functools.partial(
        pl.kernel, mesh=mesh,
        out_type=jax.ShapeDtypeStruct((n_out, SUBLANES, LANES), F32),
        scratch_types=[pltpu.VMEM((idx_rows, SC_CHUNK), jnp.int32),
                       pltpu.VMEM(row_tile, F32), pltpu.VMEM(row_tile, F32),
                       pltpu.SemaphoreType.DMA, pltpu.SemaphoreType.DMA],
        name="sc_row_dispatch",
    )
    def dispatch(rows_hbm, idx_hbm, out_hbm, idx_v, buf0, buf1, sem0, sem1):
        wid = lax.axis_index("s") * SC_CORES + lax.axis_index("c")
        pltpu.sync_copy(idx_hbm.at[pl.ds(wid * idx_rows, idx_rows)], idx_v)
        base = wid * per_worker

        def fetch(chunk, buf, sem):
            return pltpu.make_async_copy(rows_hbm.at[pl.ds(base + chunk * SC_CHUNK, SC_CHUNK)], buf, sem)

        def emit(chunk, buf):
            for k in range(TOP_K):
                pltpu.sync_copy(buf, out_hbm.at[idx_v.at[k * n_chunks + chunk]])

        fetch(0, buf0, sem0).start()

        @pl.loop(0, n_chunks, step=2)
        def _(c):
            fetch(c + 1, buf1, sem1).start()
            fetch(c, buf0, sem0).wait()
            emit(c, buf0)

            @pl.when(c + 2 < n_chunks)
            def _():
                fetch(c + 2, buf0, sem0).start()

            fetch(c + 1, buf1, sem1).wait()
            emit(c + 1, buf1)

    idx = dest_t.reshape(TOP_K, SC_WORKERS, n_chunks, SC_CHUNK).transpose(1, 0, 2, 3)
    out = dispatch(rows.reshape(-1, SUBLANES, LANES), idx.reshape(SC_WORKERS * idx_rows, SC_CHUNK))
    return out.reshape(n_out * SUBLANES, LANES)


def _combine_kernel(first_tile, x1_ref, rt_ref, mod_ref, g0_ref, g1_ref, g2_ref, g3_ref, o_ref):
    t = pl.program_id(0)
    per_b = LT // TQ - first_tile
    b = t // per_b
    is_ctx = (t % per_b + first_tile) == 0
    route = rt_ref[0]
    y = jnp.zeros((TQ, D), F32)
    for k, g_ref in enumerate((g0_ref, g1_ref, g2_ref, g3_ref)):
        y = y + route[:, TOP_K + k:TOP_K + k + 1] * _rows_from_tiles(g_ref, TQ)
    gate = jnp.where(is_ctx, mod_ref[NB:NB + 1, 5 * D:6 * D], mod_ref[pl.ds(b, 1), 5 * D:6 * D])
    o_ref[0] = x1_ref[0] + gate * y


def _combine(l, x1, route, mod, gathered, latent_only):
    tiles_b = LT // TQ
    first_tile = L_CTX // TQ if latent_only else 0
    per_b = tiles_b - first_tile
    tile_of = lambda t: (t // per_b, t % per_b + first_tile)
    tok = lambda w: pl.BlockSpec((1, TQ, w), lambda t: tile_of(t) + (0,))
    slot = lambda k: pl.BlockSpec(
        (TQ * SUBLANES, LANES), lambda t: (k * NB * tiles_b + tile_of(t)[0] * tiles_b + tile_of(t)[1], 0))
    return pl.pallas_call(
        functools.partial(_combine_kernel, first_tile),
        out_shape=jax.ShapeDtypeStruct((NB, per_b * TQ, D), F32),
        grid=(NB * per_b,),
        in_specs=[tok(D), tok(LANES), _layer_spec(l, (16, N_MOD * D)),
                  slot(0), slot(1), slot(2), slot(3)],
        out_specs=pl.BlockSpec((1, TQ, D), lambda t: (t // per_b, t % per_b, 0)),
        compiler_params=pltpu.CompilerParams(
            dimension_semantics=("arbitrary",), vmem_limit_bytes=VMEM_LIMIT),
        name="moe_combine",
    )(x1, route, mod, gathered, gathered, gathered, gathered)


def _rope_tables():
    pos = jnp.arange(L_LAT)
    row = (pos // GRID_W).astype(F32)
    col = (pos % GRID_W).astype(F32)
    inv = ROPE_BASE ** (-jnp.arange(0, 32, 2, dtype=F32) / 32)
    ang_r = row[:, None] * inv
    ang_c = col[:, None] * inv
    ang = jnp.concatenate([ang_r, ang_r, ang_c, ang_c], axis=-1)
    cos = jnp.concatenate([jnp.ones((L_CTX, HEAD_DIM), F32), jnp.cos(ang)], axis=0)
    sin = jnp.concatenate([jnp.zeros((L_CTX, HEAD_DIM), F32), jnp.sin(ang)], axis=0)
    sign = jnp.where((jnp.arange(HEAD_DIM) % 32) < 16, -1.0, 1.0).astype(F32)
    return jnp.tile(cos, (1, 2)), jnp.tile(sin * sign, (1, 2))


def _s5_tables(a_re, a_im, log_dt, b_re, b_im, c_re, c_im):
    dt = jnp.exp(log_dt)[..., None]
    mag = jnp.exp(a_re * dt)
    ar = mag * jnp.cos(a_im * dt)
    ai = mag * jnp.sin(a_im * dt)
    den = a_re * a_re + a_im * a_im
    qr = ((ar - 1) * a_re + ai * a_im) / den
    qi = (ai * a_re - (ar - 1) * a_im) / den
    bbr = qr[..., None] * b_re - qi[..., None] * b_im
    bbi = qr[..., None] * b_im + qi[..., None] * b_re
    eye = jnp.eye(SSM_GROUPS, dtype=F32)
    to_in = lambda m: jnp.einsum('ldgnp,gh->ldgphn', m, eye).reshape(DEPTH, 2, SSM_W, NS)
    bm = jnp.concatenate([to_in(bbr), to_in(bbi)], axis=-1).astype(BF)
    to_out = lambda m: jnp.einsum('ldgpn,gh->ldgnhp', m, eye).reshape(DEPTH, 2, NS, SSM_W)
    cm = jnp.concatenate([to_out(c_re), -to_out(c_im)], axis=2).astype(BF)
    a_tab = jnp.stack([ar.reshape(DEPTH, 2, NS), ai.reshape(DEPTH, 2, NS)], axis=2)
    return a_tab, bm, cm


def _interleave_perms():
    rows = NB * T_PERM
    r = jnp.arange(rows)
    src = (r % NB) * T_PERM + r // NB
    pin = (src[:, None] == jnp.arange(rows)[None, :]).astype(BF)
    return pin, pin.T


def kernel(x, c, ctx, c_ctx, w_mod, b_mod, norm1_w, norm2_w, w_in, q_norm_w, k_norm_w, attn_sink,
           ssm_a_re, ssm_a_im, ssm_log_dt, ssm_b_re, ssm_b_im, ssm_c_re, ssm_c_im, ssm_d, glu_w, glu_b,
           pool_w, pool_scale, out_norm_w, w_out, router_w, router_b, exp_w_gu, exp_b_gu, exp_w_down,
           exp_b_down):
    xc = jnp.concatenate([ctx, x], axis=1)
    cc = jnp.concatenate([c, c_ctx[None, :], jnp.zeros((16 - NB - 1, D), F32)], axis=0)
    mod = _adaln(cc, w_mod, b_mod)
    cos2, sin2 = _rope_tables()
    pin, pout = _interleave_perms()
    tri = (jnp.arange(TM_TOK)[:, None] > jnp.arange(TM_TOK)[None, :]).astype(BF)

    row = lambda a: a.reshape(DEPTH, 1, -1)
    dup = lambda m: jnp.concatenate([m[..., :64], m[..., :64], m[..., 64:], m[..., 64:]], axis=-1)
    w_ext = jnp.concatenate([w_in[..., :512], dup(w_in[..., 512:640]), dup(w_in[..., 640:768]),
                             w_in[..., 768:]], axis=-1).astype(BF)
    qw2 = row(jnp.tile(q_norm_w, (1, 2)))
    kw2 = row(jnp.tile(k_norm_w, (1, 2)))
    a_tab, bm, cm = _s5_tables(ssm_a_re, ssm_a_im, ssm_log_dt, ssm_b_re, ssm_b_im, ssm_c_re, ssm_c_im)
    pw_bd = jnp.einsum('lgcd,gh->lgchd', pool_w, jnp.eye(4, dtype=F32)).reshape(DEPTH, POOL_W, POOL_W).astype(BF)
    rw = jnp.concatenate([router_w, jnp.zeros((DEPTH, D, LANES - N_EXP), F32)], axis=-1)
    rb = row(jnp.concatenate([router_b, jnp.full((DEPTH, LANES - N_EXP), -1e30, F32)], axis=-1))
    glu_w_bf = glu_w.astype(BF)
    w_out_bf = w_out.astype(BF)
    n1, n2, onw = row(norm1_w), row(norm2_w), row(out_norm_w)
    ssm_d3, glu_b3, pool_sc3 = row(ssm_d), row(glu_b), row(pool_scale)
    sink = attn_sink.reshape(-1)

    for l in range(DEPTH):
        q, kd, vd, u, p = _inproj(l, xc, mod, n1, w_ext, qw2, kw2, cos2, sin2)
        attn = _attention(l, sink, q, kd, vd, onw)
        yf, yb = _s5_scan(l, u, pin, pout, a_tab, bm, cm)
        pool = _pool(l, p, pw_bd, pool_sc3)
        x1, h2, route, cnt = _mix(l, xc, attn, yf, yb, u, pool, mod, ssm_d3, glu_w_bf, glu_b3, onw,
                                  w_out_bf, n2, rw, rb, tri)

        route = route.reshape(N_TOK, LANES)
        top_i = route[:, 0:TOP_K].astype(jnp.int32)
        rank = route[:, 2 * TOP_K:3 * TOP_K].astype(jnp.int32)
        counts = cnt[0, :N_EXP].astype(jnp.int32)
        padded = (counts + TM_MOE - 1) // TM_MOE * TM_MOE
        pend = jnp.cumsum(padded)
        pstart = pend - padded
        experts = jnp.arange(N_EXP, dtype=jnp.int32)
        dest = jnp.sum(jnp.where(top_i[..., None] == experts, pstart, 0), axis=-1) + rank
        dest_t = dest.T
        n_used = (pend[-1] // TM_MOE).astype(jnp.int32)[None]
        tile_lo = jnp.arange(NT_MOE, dtype=jnp.int32) * TM_MOE
        tile_e = jnp.minimum(jnp.sum(pend[None, :] <= tile_lo[:, None], axis=1), N_EXP - 1).astype(jnp.int32)
        change = jnp.concatenate([jnp.ones((1,), jnp.int32), (tile_e[1:] != tile_e[:-1]).astype(jnp.int32)])
        slot = (jnp.cumsum(change) - 1) % 2
        next_first = jnp.sum(jnp.where(tile_e[:, None] == experts, pend // TM_MOE, 0), axis=1)
        tiles = jnp.arange(NT_MOE, dtype=jnp.int32)
        next_e = jnp.sum(jnp.where(next_first[:, None] == tiles, tile_e, 0), axis=1)
        nxt = jnp.where(next_first < n_used[0], next_e, -1)
        xs = _row_dispatch(h2, dest_t, R_MOE)

        yb_rows = _moe(l, tile_e, n_used, slot.astype(jnp.int32), nxt.astype(jnp.int32), xs,
                       exp_w_gu, exp_b_gu, exp_w_down, exp_b_down)
        gathered = _row_gather(yb_rows, dest_t.reshape(-1))
        xc = _combine(l, x1, route.reshape(NB, LT, LANES), mod, gathered, latent_only=(l == DEPTH - 1))
    return xc
```

```python
import functools
import math

import jax
import jax.numpy as jnp
from jax import lax
from jax.experimental import pallas as pl
from jax.experimental.pallas import tpu as pltpu
from jax.experimental.pallas import tpu_sc as plsc

D = 1024
NB = 8
L_LAT = 2048
L_CTX = 256
LT = L_CTX + L_LAT
DEPTH = 4
N_MOD = 6
EPS = 1e-6
N_HEADS = 8
HEAD_DIM = 64
ATTN_W = 512
WINDOW = 128
ATTN_SCALE = HEAD_DIM ** -0.5
LOG2E = math.log2(math.e)
ROPE_BASE = 10000.0
GRID_W = 64
SSM_W = 256
SSM_GROUP = 16
SSM_GROUPS = 16
SSM_STATE = 64
NS = SSM_GROUPS * SSM_STATE
POOL_W = 256
POOL_GROUP = 64
N_EXP = 32
TOP_K = 4
D_FF = 1024
SWIGLU_LIMIT = 7.0
SWIGLU_ALPHA = 1.702

LANES = 128
SUBLANES = 8
VMEM_LIMIT = 56 * 1024 * 1024

TM_TOK = 768
TQ = 256
KSPAN = TQ + 2 * WINDOW
T_SCAN = 128
T_PERM = 32
N_SUB = T_SCAN // T_PERM
N_CHUNK = LT // T_SCAN
N_CHUNK_CTX = L_CTX // T_SCAN
TM_MOE = 512
N_TOK = NB * LT
N_ASSIGN = N_TOK * TOP_K
NT_MOE = N_ASSIGN // TM_MOE + N_EXP
R_MOE = NT_MOE * TM_MOE
FF_CHUNK = 512
OUT_BLOCK = 256

SC_CORES = 2
SC_SUBCORES = 16
SC_WORKERS = SC_CORES * SC_SUBCORES
SC_CHUNK = 32

C_Q = 0
C_K = 512
C_V = 768
C_U = 1024
C_P = 1280
NW_IN = 1536

BF = jnp.bfloat16
F32 = jnp.float32


def _split_bf16(a):
    hi = a.astype(BF)
    lo = (a - hi.astype(F32)).astype(BF)
    return hi, lo


def _dot(a, b):
    return jnp.dot(a, b, preferred_element_type=F32)


def _dot3(a, b):
    ah, al = _split_bf16(a)
    bh, bl = _split_bf16(b)
    return _dot(ah, bh) + (_dot(ah, bl) + _dot(al, bh))


def _mod_kernel(c_ref, w_ref, b_ref, o_ref):
    c = c_ref[...]
    s = c * jax.nn.sigmoid(c)
    o_ref[0] = _dot3(s, w_ref[0]) + b_ref[0]


def _adaln(cc, w_mod, b_mod):
    tn = 1536
    return pl.pallas_call(
        _mod_kernel,
        out_shape=jax.ShapeDtypeStruct((DEPTH, 16, N_MOD * D), F32),
        grid=(DEPTH, N_MOD * D // tn),
        in_specs=[
            pl.BlockSpec((16, D), lambda l, j: (0, 0)),
            pl.BlockSpec((1, D, tn), lambda l, j: (l, 0, j)),
            pl.BlockSpec((1, 1, tn), lambda l, j: (l, 0, j)),
        ],
        out_specs=pl.BlockSpec((1, 16, tn), lambda l, j: (l, 0, j)),
        compiler_params=pltpu.CompilerParams(
            dimension_semantics=("arbitrary", "arbitrary"), vmem_limit_bytes=VMEM_LIMIT),
        name="adaln_mod",
    )(cc, w_mod, b_mod.reshape(DEPTH, 1, N_MOD * D))


def _layer_spec(l, shape, col_block=0):
    shape = tuple(shape)
    return pl.BlockSpec((None,) + shape, lambda *_: (l,) + (0,) * (len(shape) - 1) + (col_block,))


def _with_mod(fn, mod_ref, b, i, idxs, *arrays):
    lat = [mod_ref[pl.ds(b, 1), k * D:(k + 1) * D] for k in idxs]
    top = [jnp.where(i == 0, mod_ref[NB:NB + 1, k * D:(k + 1) * D], v) for k, v in zip(idxs, lat)]
    return jnp.concatenate([fn(*[a[:L_CTX] for a in arrays], *top),
                            fn(*[a[L_CTX:] for a in arrays], *lat)], axis=0)


def _rmsnorm(x, w):
    ms = jnp.mean(x * x, axis=-1, keepdims=True)
    return x * lax.rsqrt(ms + EPS) * w


def _headnorm_pair(t, w2, lane):
    sq = t * t
    first = lane < HEAD_DIM
    a = jnp.sum(jnp.where(first, sq, 0.0), axis=-1, keepdims=True)
    b = jnp.sum(jnp.where(first, 0.0, sq), axis=-1, keepdims=True)
    ms = jnp.where(first, a, b) * (1.0 / HEAD_DIM)
    return t * lax.rsqrt(ms + EPS) * w2


def _rope_pair(t, cos, sin_signed, lane):
    lower = (lane % 32) < 16
    partner = jnp.where(lower, pltpu.roll(t, LANES - 16, 1), pltpu.roll(t, 16, 1))
    return t * cos + partner * sin_signed


def _inproj_kernel(x_ref, mod_ref, n1_ref, w_ref, qw_ref, kw_ref, cos_ref, sin_ref,
                   q_ref, k_ref, v_ref, u_ref, p_ref):
    b = pl.program_id(0)
    i = pl.program_id(1)
    h = _rmsnorm(x_ref[0], n1_ref[...])
    h = _with_mod(lambda hr, shift, scale: (hr * (1.0 + scale) + shift).astype(BF), mod_ref, b, i, (0, 1), h)
    px = _dot(h, w_ref[...])
    lane = lax.broadcasted_iota(jnp.int32, (TM_TOK, LANES), 1)
    cos = cos_ref[...]
    sin = sin_ref[...]
    for j in range(ATTN_W // LANES):
        t = px[:, C_Q + j * LANES:C_Q + (j + 1) * LANES]
        t = _rope_pair(_headnorm_pair(t, qw_ref[...], lane), cos, sin, lane) * (ATTN_SCALE * LOG2E)
        q_ref[0, :, j * LANES:(j + 1) * LANES] = t.astype(BF)
    for g in range(2):
        t = px[:, C_K + g * LANES:C_K + (g + 1) * LANES]
        t = _rope_pair(_headnorm_pair(t, kw_ref[...], lane), cos, sin, lane)
        k_ref[0, :, g * LANES:(g + 1) * LANES] = t.astype(BF)
        t = px[:, C_V + g * LANES:C_V + (g + 1) * LANES]
        v_ref[0, :, g * LANES:(g + 1) * LANES] = jnp.where(lane < HEAD_DIM, t, 1.0).astype(BF)
    u_ref[0] = px[:, C_U:C_P]
    p_ref[0] = px[:, C_P:NW_IN]


def _inproj(l, xc, mod, n1, w_ext, qw2, kw2, cos2, sin2):
    tok = lambda w: pl.BlockSpec((1, TM_TOK, w), lambda b, i: (b, i, 0))
    full = lambda s: _layer_spec(l, s)
    return pl.pallas_call(
        _inproj_kernel,
        out_shape=(
            jax.ShapeDtypeStruct((NB, LT, ATTN_W), BF),
            jax.ShapeDtypeStruct((NB, LT, 256), BF),
            jax.ShapeDtypeStruct((NB, LT, 256), BF),
            jax.ShapeDtypeStruct((NB, LT, SSM_W), F32),
            jax.ShapeDtypeStruct((NB, LT, POOL_W), F32),
        ),
        grid=(NB, LT // TM_TOK),
        in_specs=[
            tok(D), full((16, N_MOD * D)), full((1, D)), full((D, NW_IN)),
            full((1, LANES)), full((1, LANES)),
            pl.BlockSpec((TM_TOK, LANES), lambda b, i: (i, 0)),
            pl.BlockSpec((TM_TOK, LANES), lambda b, i: (i, 0)),
        ],
        out_specs=(tok(ATTN_W), tok(256), tok(256), tok(SSM_W), tok(POOL_W)),
        compiler_params=pltpu.CompilerParams(
            dimension_semantics=("arbitrary", "arbitrary"), vmem_limit_bytes=VMEM_LIMIT),
        name="inproj",
    )(xc, mod, n1, w_ext, qw2, kw2, cos2, sin2)


def _attn_kernel(l, sink_ref, q_ref, k_ref, v_ref, onw_ref, o_ref, bias_ref):
    j = pl.program_id(1)
    start = pl.multiple_of(jnp.clip(j * TQ - WINDOW, LANES, LT - KSPAN), LANES)
    lane = lax.broadcasted_iota(jnp.int32, (TQ, LANES), 1)
    first = lane < HEAD_DIM
    row2 = lax.broadcasted_iota(jnp.int32, (2 * TQ, 1), 0)
    qpos = j * TQ + jnp.where(row2 < TQ, row2, row2 - TQ)
    kpos = start + lax.broadcasted_iota(jnp.int32, (1, KSPAN), 1)
    valid = (jnp.abs(qpos - kpos) <= WINDOW) & (kpos >= L_CTX) & (j >= 1)
    bias_ref[...] = jnp.where(valid, 0.0, -1e30)
    n_pairs = N_HEADS // 2

    def kv(ref, hp, rows):
        g = hp // 2
        return ref[0, rows, g * LANES:(g + 1) * LANES]

    def scores(hp):
        q2 = q_ref[0, :, hp * LANES:(hp + 1) * LANES]
        zero = jnp.zeros_like(q2)
        qs = jnp.concatenate([jnp.where(first, q2, zero), jnp.where(first, zero, q2)], axis=0)
        dn = (((1,), (1,)), ((), ()))
        s_ctx = lax.dot_general(qs, kv(k_ref, hp, slice(0, L_CTX)), dn, preferred_element_type=F32)
        s_loc = lax.dot_general(qs, kv(k_ref, hp, pl.ds(start, KSPAN)), dn, preferred_element_type=F32)
        return s_loc, s_ctx

    nxt = scores(0)
    outs = []
    for hp in range(n_pairs):
        s_loc, s_ctx = nxt
        if hp + 1 < n_pairs:
            nxt = scores(hp + 1)
        s_loc = s_loc + bias_ref[...]
        sink = jnp.where(row2 < TQ, sink_ref[l * N_HEADS + 2 * hp],
                         sink_ref[l * N_HEADS + 2 * hp + 1]) * LOG2E
        m = jnp.maximum(jnp.maximum(jnp.max(s_loc, axis=-1, keepdims=True),
                                    jnp.max(s_ctx, axis=-1, keepdims=True)), sink)
        e_loc = jnp.exp2(s_loc - m)
        e_ctx = jnp.exp2(s_ctx - m)
        o = (_dot(e_loc.astype(BF), kv(v_ref, hp, pl.ds(start, KSPAN)))
             + _dot(e_ctx.astype(BF), kv(v_ref, hp, slice(0, L_CTX))))
        den = pltpu.roll(o + jnp.exp2(sink - m), HEAD_DIM, 1)
        o = o / den
        outs.append(jnp.where(first, o[0:TQ], pltpu.roll(o[TQ:2 * TQ], HEAD_DIM, 1)))
    o_ref[0] = _rmsnorm(jnp.concatenate(outs, axis=1), onw_ref[...]).astype(BF)


def _attention(l, sink, q, kd, vd, onw):
    return pl.pallas_call(
        functools.partial(_attn_kernel, l),
        out_shape=jax.ShapeDtypeStruct((NB, LT, ATTN_W), BF),
        grid_spec=pltpu.PrefetchScalarGridSpec(
            num_scalar_prefetch=1,
            grid=(NB, LT // TQ),
            in_specs=[
                pl.BlockSpec((1, TQ, ATTN_W), lambda b, j, s: (b, j, 0)),
                pl.BlockSpec((1, LT, 256), lambda b, j, s: (b, 0, 0)),
                pl.BlockSpec((1, LT, 256), lambda b, j, s: (b, 0, 0)),
                _layer_spec(l, (1, ATTN_W)),
            ],
            out_specs=pl.BlockSpec((1, TQ, ATTN_W), lambda b, j, s: (b, j, 0)),
            scratch_shapes=[pltpu.VMEM((2 * TQ, KSPAN), F32)],
        ),
        compiler_params=pltpu.CompilerParams(
            dimension_semantics=("arbitrary", "arbitrary"), vmem_limit_bytes=VMEM_LIMIT),
        name="window_attn",
    )(sink, q, kd, vd, onw)


def _bwd_chunk(i):
    return jnp.where(i < N_CHUNK_CTX, N_CHUNK_CTX - 1 - i, N_CHUNK - 1 - (i - N_CHUNK_CTX))


def _scan_kernel(uf_ref, ub_ref, pin_ref, pout_ref, a_ref, bm_ref, cm_ref, yf_ref, yb_ref,
                 xf_ref, xb_ref, st_ref):
    i = pl.program_id(0)
    sub_rows = NB * T_PERM

    @pl.when(i == 0)
    def _():
        st_ref[...] = jnp.zeros_like(st_ref)

    def project(d, u_ref, xs_ref):
        ui = jnp.concatenate(
            [_dot(pin_ref[...], u_ref[:, s * T_PERM:(s + 1) * T_PERM, :].reshape(sub_rows, SSM_W).astype(BF))
             for s in range(N_SUB)], axis=0).astype(BF)
        xs_ref[...] = _dot(ui, bm_ref[d])

    def scan(d, xs_ref, reverse):
        ar = jnp.broadcast_to(a_ref[d, 0:1, :], (NB, NS))
        ai = jnp.broadcast_to(a_ref[d, 1:2, :], (NB, NS))
        sr = st_ref[d, 0]
        si = st_ref[d, 1]
        for t in (range(T_SCAN - 1, -1, -1) if reverse else range(T_SCAN)):
            r = pl.ds(t * NB, NB)
            nr = ar * sr - ai * si + xs_ref[r, 0:NS]
            ni = ar * si + ai * sr + xs_ref[r, NS:2 * NS]
            sr, si = nr, ni
            xs_ref[r, 0:NS] = sr
            xs_ref[r, NS:2 * NS] = si
        st_ref[d, 0] = sr
        st_ref[d, 1] = si

    def readout(d, xs_ref, y_ref):
        y = _dot(xs_ref[...].astype(BF), cm_ref[d])
        yh, yl = _split_bf16(y)
        for s in range(N_SUB):
            rs = slice(s * sub_rows, (s + 1) * sub_rows)
            ys = _dot(pout_ref[...], yh[rs]) + _dot(pout_ref[...], yl[rs])
            y_ref[:, s * T_PERM:(s + 1) * T_PERM, :] = ys.reshape(NB, T_PERM, SSM_W)

    project(0, uf_ref, xf_ref)
    project(1, ub_ref, xb_ref)
    scan(0, xf_ref, False)
    readout(0, xf_ref, yf_ref)
    scan(1, xb_ref, True)
    readout(1, xb_ref, yb_ref)


def _s5_scan(l, u, pin, pout, a_tab, bm, cm):
    rows = NB * T_SCAN
    full = lambda s: pl.BlockSpec(s, lambda i: (0,) * len(s))
    layer = lambda s: _layer_spec(l, s)
    chunk_f = pl.BlockSpec((NB, T_SCAN, SSM_W), lambda i: (0, i, 0))
    chunk_b = pl.BlockSpec((NB, T_SCAN, SSM_W), lambda i: (0, _bwd_chunk(i), 0))
    return pl.pallas_call(
        _scan_kernel,
        out_shape=(jax.ShapeDtypeStruct((NB, LT, SSM_W), F32),
                   jax.ShapeDtypeStruct((NB, LT, SSM_W), F32)),
        grid=(N_CHUNK,),
        in_specs=[chunk_f, chunk_b, full((NB * T_PERM, NB * T_PERM)), full((NB * T_PERM, NB * T_PERM)),
                  layer((2, 2, NS)), layer((2, SSM_W, 2 * NS)), layer((2, 2 * NS, SSM_W))],
        out_specs=(chunk_f, chunk_b),
        scratch_shapes=[pltpu.VMEM((rows, 2 * NS), F32), pltpu.VMEM((rows, 2 * NS), F32),
                        pltpu.VMEM((2, 2, NB, NS), F32)],
        compiler_params=pltpu.CompilerParams(
            dimension_semantics=("arbitrary",), vmem_limit_bytes=VMEM_LIMIT),
        name="s5_scan",
    )(u, u, pin, pout, a_tab, bm, cm)


def _pool_segment(ps, length):
    n = length + 2 * SUBLANES
    z = jnp.zeros((SUBLANES, POOL_W), F32)
    pe = jnp.concatenate([z, ps, z], axis=0)
    a1 = pe + pltpu.roll(pe, 1, 0)
    a2 = a1 + pltpu.roll(a1, 2, 0)
    a3 = a2 + pltpu.roll(a2, 4, 0)
    a4 = a3 + pltpu.roll(a3, 8, 0)
    lane = lax.broadcasted_iota(jnp.int32, (1, POOL_W), 1)
    half = jnp.where(lane < 64, 1, jnp.where(lane < 128, 2, jnp.where(lane < 192, 4, 8)))
    s = jnp.where(lane < 64, a1,
                  jnp.where(lane < 128, pltpu.roll(a2, n - 1, 0),
                            jnp.where(lane < 192, pltpu.roll(a3, n - 3, 0), pltpu.roll(a4, n - 7, 0))))
    s = s[SUBLANES:SUBLANES + length]
    t = lax.broadcasted_iota(jnp.int32, (length, 1), 0)
    cnt = jnp.minimum(t + half, length) - jnp.maximum(t - half, 0)
    return s / cnt.astype(F32) - ps


def _pool_kernel(p_ref, w_ref, sc_ref, o_ref):
    for lo, length in ((0, L_CTX), (L_CTX, L_LAT)):
        dlt = _pool_segment(p_ref[0, lo:lo + length, :], length)
        y = _dot(dlt.astype(BF), w_ref[...]) * sc_ref[...]
        o_ref[0, lo:lo + length, :] = y.astype(BF)


def _pool(l, p, w_bd, scale):
    return pl.pallas_call(
        _pool_kernel,
        out_shape=jax.ShapeDtypeStruct((NB, LT, POOL_W), BF),
        grid=(NB,),
        in_specs=[pl.BlockSpec((1, LT, POOL_W), lambda b: (b, 0, 0)),
                  _layer_spec(l, (POOL_W, POOL_W)),
                  _layer_spec(l, (1, POOL_W))],
        out_specs=pl.BlockSpec((1, LT, POOL_W), lambda b: (b, 0, 0)),
        compiler_params=pltpu.CompilerParams(
            dimension_semantics=("arbitrary",), vmem_limit_bytes=VMEM_LIMIT),
        name="pool",
    )(p, w_bd, scale)


def _mix_kernel(x_ref, at_ref, yf_ref, yb_ref, u_ref, pl_ref, mod_ref, d_ref, gw_ref, gb_ref, onw_ref,
                wo_ref, n2_ref, rw_ref, rb_ref, tri_ref,
                x1_ref, h2_ref, rt_ref, cnt_ref, run_ref):
    b = pl.program_id(0)
    i = pl.program_id(1)

    @pl.when((b == 0) & (i == 0))
    def _():
        run_ref[...] = jnp.zeros_like(run_ref)

    y = yf_ref[0] + yb_ref[0] + d_ref[...] * u_ref[0]
    g = jax.nn.gelu(y, approximate=True)
    s = g * jax.nn.sigmoid(_dot(g.astype(BF), gw_ref[...]) + gb_ref[...])
    s = _rmsnorm(s, onw_ref[...]).astype(BF)
    mix = (_dot(at_ref[0], wo_ref[0:ATTN_W, :])
           + _dot(s, wo_ref[ATTN_W:ATTN_W + SSM_W, :])
           + _dot(pl_ref[0], wo_ref[ATTN_W + SSM_W:D, :]))
    x1 = _with_mod(lambda xr, mr, gate: xr + gate * mr, mod_ref, b, i, (2,), x_ref[0], mix)
    x1_ref[0] = x1
    h2 = _rmsnorm(x1, n2_ref[...])
    h2 = _with_mod(lambda hr, shift, scale: hr * (1.0 + scale) + shift, mod_ref, b, i, (3, 4), h2)
    for j in range(D // LANES):
        h2_ref[pl.ds(j, TM_TOK, stride=SUBLANES), :] = h2[:, j * LANES:(j + 1) * LANES]
    logits = (_dot3(h2, rw_ref[...]) + rb_ref[...]).T[0:N_EXP]
    eidx = lax.broadcasted_iota(jnp.int32, (N_EXP, TM_TOK), 0).astype(F32)
    vals, idxs, hits = [], [], []
    cur = logits
    for _ in range(TOP_K):
        m = jnp.max(cur, axis=0, keepdims=True)
        idx = jnp.min(jnp.where(cur == m, eidx, float(N_EXP)), axis=0, keepdims=True)
        hit = eidx == idx
        vals.append(m)
        idxs.append(idx)
        hits.append(hit)
        cur = jnp.where(hit, -jnp.inf, cur)
    ex = [jnp.exp(v - vals[0]) for v in vals]
    den = ex[0] + ex[1] + ex[2] + ex[3]
    onehot = jnp.where(hits[0] | hits[1] | hits[2] | hits[3], 1.0, 0.0)
    run = run_ref[:, 0:1]
    before = _dot(onehot.astype(BF), tri_ref[...]) + run
    row = lax.broadcasted_iota(jnp.int32, (4 * TOP_K, TM_TOK), 0)
    route = jnp.zeros((4 * TOP_K, TM_TOK), F32)
    for k in range(TOP_K):
        rank = jnp.sum(jnp.where(hits[k], before, 0.0), axis=0, keepdims=True)
        route = jnp.where(row == k, idxs[k], route)
        route = jnp.where(row == TOP_K + k, ex[k] / den, route)
        route = jnp.where(row == 2 * TOP_K + k, rank, route)
    rt_ref[0] = route
    run_ref[...] = jnp.broadcast_to(run + jnp.sum(onehot, axis=1, keepdims=True), (N_EXP, LANES))
    cnt_ref[...] = run_ref[...]


def _mix(l, xc, attn, yf, yb, u, pool, mod, ssm_d, glu_w, glu_b, onw, w_out, n2, rw, rb, tri):
    tok = lambda w: pl.BlockSpec((1, TM_TOK, w), lambda b, i: (b, i, 0))
    full = lambda s: pl.BlockSpec(s, lambda b, i: (0,) * len(s))
    layer = lambda s, col=0: _layer_spec(l, s, col)
    return pl.pallas_call(
        _mix_kernel,
        out_shape=(jax.ShapeDtypeStruct((NB, LT, D), F32),
                   jax.ShapeDtypeStruct((N_TOK * SUBLANES, LANES), F32),
                   jax.ShapeDtypeStruct((N_TOK // TM_TOK, 4 * TOP_K, TM_TOK), F32),
                   jax.ShapeDtypeStruct((N_EXP, LANES), F32)),
        grid=(NB, LT // TM_TOK),
        in_specs=[tok(D), tok(ATTN_W), tok(SSM_W), tok(SSM_W), tok(SSM_W), tok(POOL_W),
                  layer((16, N_MOD * D)), layer((1, SSM_W)), layer((SSM_W, SSM_W)), layer((1, SSM_W)),
                  layer((1, SSM_W), ATTN_W // SSM_W), layer((D, D)), layer((1, D)), layer((D, LANES)),
                  layer((1, LANES)), full((TM_TOK, TM_TOK))],
        out_specs=(tok(D),
                   pl.BlockSpec((TM_TOK * SUBLANES, LANES), lambda b, i: (b * (LT // TM_TOK) + i, 0)),
                   pl.BlockSpec((1, 4 * TOP_K, TM_TOK), lambda b, i: (b * (LT // TM_TOK) + i, 0, 0)),
                   full((N_EXP, LANES))),
        scratch_shapes=[pltpu.VMEM((N_EXP, LANES), F32)],
        compiler_params=pltpu.CompilerParams(
            dimension_semantics=("arbitrary", "arbitrary"), vmem_limit_bytes=VMEM_LIMIT),
        name="mix_router",
    )(xc, attn, yf, yb, u, pool, mod, ssm_d, glu_w, glu_b, onw, w_out, n2, rw, rb, tri)


def _rows_from_tiles(ref, rows):
    return jnp.concatenate(
        [ref[pl.ds(j, rows, stride=SUBLANES), :] for j in range(D // LANES)], axis=1)


def _rows_to_tiles(ref, val, rows):
    for j in range(D // LANES):
        ref[pl.ds(j, rows, stride=SUBLANES), :] = val[:, j * LANES:(j + 1) * LANES]


def _moe_kernel(l, te_ref, nu_ref, slot_ref, nxt_ref, x_ref, wgu_hbm, bgu_ref, wd_hbm, bd_ref, o_ref,
                wgu_f, wd_f, wgu_s, wd_s, act_s, sem):
    i = pl.program_id(0)

    @pl.when(i >= nu_ref[0])
    def _():
        o_ref[...] = jnp.zeros_like(o_ref)

    def fetch(e, slot):
        return (pltpu.make_async_copy(wgu_hbm.at[l, e], wgu_f.at[slot], sem.at[0, slot]),
                pltpu.make_async_copy(wd_hbm.at[l, e], wd_f.at[slot], sem.at[1, slot]))

    @pl.when(i < nu_ref[0])
    def _():
        e = te_ref[i]
        prev = te_ref[jnp.maximum(i - 1, 0)]
        slot = slot_ref[i]

        @pl.when(i == 0)
        def _():
            for cp in fetch(e, slot):
                cp.start()

        @pl.when((i == 0) | (e != prev))
        def _():
            for cp in fetch(e, slot):
                cp.wait()
            nxt = nxt_ref[i]

            @pl.when(nxt >= 0)
            def _():
                for cp in fetch(nxt, 1 - slot):
                    cp.start()

            def cast_gu(r, c):
                rs = pl.ds(pl.multiple_of(r * 128, 128), 128)
                wgu_s[rs, :] = wgu_f[slot, rs, :].astype(BF)
                return c

            def cast_d(r, c):
                rs = pl.ds(pl.multiple_of(r * 128, 128), 128)
                wd_s[rs, :] = wd_f[slot, rs, :].astype(BF)
                return c

            lax.fori_loop(0, D // 128, cast_gu, 0)
            lax.fori_loop(0, D_FF // 128, cast_d, 0)

        x = _rows_from_tiles(x_ref, TM_MOE).astype(BF)
        for c in range(D_FF // FF_CHUNK):
            lo = c * FF_CHUNK
            gate = _dot(x, wgu_s[:, lo:lo + FF_CHUNK]) + bgu_ref[:, lo:lo + FF_CHUNK]
            up = (_dot(x, wgu_s[:, D_FF + lo:D_FF + lo + FF_CHUNK])
                  + bgu_ref[:, D_FF + lo:D_FF + lo + FF_CHUNK])
            gate = jnp.minimum(gate, SWIGLU_LIMIT)
            up = jnp.clip(up, -SWIGLU_LIMIT, SWIGLU_LIMIT)
            act = (up + 1.0) * (gate * jax.nn.sigmoid(SWIGLU_ALPHA * gate))
            act_s[:, lo:lo + FF_CHUNK] = act.astype(BF)
        for n in range(D // OUT_BLOCK):
            lo = n * OUT_BLOCK
            y = _dot(act_s[...], wd_s[:, lo:lo + OUT_BLOCK]) + bd_ref[:, lo:lo + OUT_BLOCK]
            for j in range(OUT_BLOCK // LANES):
                o_ref[pl.ds(lo // LANES + j, TM_MOE, stride=SUBLANES), :] = y[:, j * LANES:(j + 1) * LANES]


def _moe(l, tile_e, n_used, slot, nxt, xs, w_gu, b_gu, w_down, b_down):
    def tile(i, te, nu, *_):
        return (jnp.minimum(i, nu[0] - 1), 0)

    def bias(i, te, nu, *_):
        return (l, te[jnp.minimum(i, nu[0] - 1)], 0, 0)

    return pl.pallas_call(
        functools.partial(_moe_kernel, l),
        out_shape=jax.ShapeDtypeStruct((R_MOE * SUBLANES, LANES), F32),
        grid_spec=pltpu.PrefetchScalarGridSpec(
            num_scalar_prefetch=4,
            grid=(NT_MOE,),
            in_specs=[
                pl.BlockSpec((TM_MOE * SUBLANES, LANES), tile),
                pl.BlockSpec(memory_space=pl.ANY),
                pl.BlockSpec((None, None, 1, 2 * D_FF), bias),
                pl.BlockSpec(memory_space=pl.ANY),
                pl.BlockSpec((None, None, 1, D), bias),
            ],
            out_specs=pl.BlockSpec((TM_MOE * SUBLANES, LANES), lambda i, *_: (i, 0)),
            scratch_shapes=[pltpu.VMEM((2, D, 2 * D_FF), F32), pltpu.VMEM((2, D_FF, D), F32),
                            pltpu.VMEM((D, 2 * D_FF), BF), pltpu.VMEM((D_FF, D), BF),
                            pltpu.VMEM((TM_MOE, D_FF), BF), pltpu.SemaphoreType.DMA((2, 2))],
        ),
        compiler_params=pltpu.CompilerParams(
            dimension_semantics=("arbitrary",), vmem_limit_bytes=VMEM_LIMIT),
        name="moe_experts",
    )(tile_e, n_used, slot, nxt, xs, w_gu, b_gu.reshape(DEPTH, N_EXP, 1, 2 * D_FF), w_down,
      b_down.reshape(DEPTH, N_EXP, 1, D))


def _row_gather(table, idx):
    n_rows = idx.shape[0]
    per_worker = n_rows // SC_WORKERS
    n_chunks = per_worker // SC_CHUNK
    assert per_worker * SC_WORKERS == n_rows and n_chunks * SC_CHUNK == per_worker and n_chunks % 2 == 0
    mesh = plsc.VectorSubcoreMesh(core_axis_name="c", subcore_axis_name="s")
    row_tile = (SC_CHUNK, SUBLANES, LANES)

    @functools.partial(
        pl.kernel, mesh=mesh,
        out_type=jax.ShapeDtypeStruct((n_rows, SUBLANES, LANES), F32),
        scratch_types=[pltpu.VMEM((n_chunks, SC_CHUNK), jnp.int32),
                       pltpu.VMEM(row_tile, F32), pltpu.VMEM(row_tile, F32),
                       pltpu.SemaphoreType.DMA, pltpu.SemaphoreType.DMA],
        name="sc_row_gather",
    )
    def gather(table_hbm, idx_hbm, out_hbm, idx_v, buf0, buf1, sem0, sem1):
        wid = lax.axis_index("s") * SC_CORES + lax.axis_index("c")
        pltpu.sync_copy(idx_hbm.at[pl.ds(wid * n_chunks, n_chunks)], idx_v)
        base = wid * per_worker

        def fetch(chunk, buf, sem):
            return pltpu.make_async_copy(table_hbm.at[idx_v.at[chunk]], buf, sem)

        def emit(chunk, buf):
            pltpu.sync_copy(buf, out_hbm.at[pl.ds(base + chunk * SC_CHUNK, SC_CHUNK)])

        fetch(0, buf0, sem0).start()

        @pl.loop(0, n_chunks, step=2)
        def _(c):
            fetch(c + 1, buf1, sem1).start()
            fetch(c, buf0, sem0).wait()
            emit(c, buf0)

            @pl.when(c + 2 < n_chunks)
            def _():
                fetch(c + 2, buf0, sem0).start()

            fetch(c + 1, buf1, sem1).wait()
            emit(c + 1, buf1)

    out = gather(table.reshape(-1, SUBLANES, LANES), idx.reshape(n_rows // SC_CHUNK, SC_CHUNK))
    return out.reshape(n_rows * SUBLANES, LANES)


def _row_dispatch(rows, dest_t, n_out):
    per_worker = N_TOK // SC_WORKERS
    n_chunks = per_worker // SC_CHUNK
    assert per_worker * SC_WORKERS == N_TOK and n_chunks * SC_CHUNK == per_worker and n_chunks % 2 == 0
    idx_rows = TOP_K * n_chunks
    assert idx_rows % SUBLANES == 0
    mesh = plsc.VectorSubcoreMesh(core_axis_name="c", subcore_axis_name="s")
    row_tile = (SC_CHUNK, SUBLANES, LANES)

    @functools.partial(
        pl.kernel, mesh=mesh,
        out_type=jax.ShapeDtypeStruct((n_out, SUBLANES, LANES), F32),
        scratch_types=[pltpu.VMEM((idx_rows, SC_CHUNK), jnp.int32),
                       pltpu.VMEM(row_tile, F32), pltpu.VMEM(row_tile, F32),
                       pltpu.SemaphoreType.DMA, pltpu.SemaphoreType.DMA],
        name="sc_row_dispatch",
    )
    def dispatch(rows_hbm, idx_hbm, out_hbm, idx_v, buf0, buf1, sem0, sem1):
        wid = lax.axis_index("s") * SC_CORES + lax.axis_index("c")
        pltpu.sync_copy(idx_hbm.at[pl.ds(wid * idx_rows, idx_rows)], idx_v)
        base = wid * per_worker

        def fetch(chunk, buf, sem):
            return pltpu.make_async_copy(rows_hbm.at[pl.ds(base + chunk * SC_CHUNK, SC_CHUNK)], buf, sem)

        def emit(chunk, buf):
            for k in range(TOP_K):
                pltpu.sync_copy(buf, out_hbm.at[idx_v.at[k * n_chunks + chunk]])

        fetch(0, buf0, sem0).start()

        @pl.loop(0, n_chunks, step=2)
        def _(c):
            fetch(c + 1, buf1, sem1).start()
            fetch(c, buf0, sem0).wait()
            emit(c, buf0)

            @pl.when(c + 2 < n_chunks)
            def _():
                fetch(c + 2, buf0, sem0).start()

            fetch(c + 1, buf1, sem1).wait()
            emit(c + 1, buf1)

    idx = dest_t.reshape(TOP_K, SC_WORKERS, n_chunks, SC_CHUNK).transpose(1, 0, 2, 3)
    out = dispatch(rows.reshape(-1, SUBLANES, LANES), idx.reshape(SC_WORKERS * idx_rows, SC_CHUNK))
    return out.reshape(n_out * SUBLANES, LANES)


def _combine_kernel(first_tile, x1_ref, rt_ref, mod_ref, g0_ref, g1_ref, g2_ref, g3_ref, o_ref):
    t = pl.program_id(0)
    per_b = LT // TQ - first_tile
    b = t // per_b
    is_ctx = (t % per_b + first_tile) == 0
    route = rt_ref[0].T
    y = jnp.zeros((TQ, D), F32)
    for k, g_ref in enumerate((g0_ref, g1_ref, g2_ref, g3_ref)):
        y = y + route[:, TOP_K + k:TOP_K + k + 1] * _rows_from_tiles(g_ref, TQ)
    gate = jnp.where(is_ctx, mod_ref[NB:NB + 1, 5 * D:6 * D], mod_ref[pl.ds(b, 1), 5 * D:6 * D])
    o_ref[0] = x1_ref[0] + gate * y


def _combine(l, x1, route, mod, gathered, latent_only):
    tiles_b = LT // TQ
    first_tile = L_CTX // TQ if latent_only else 0
    per_b = tiles_b - first_tile
    tile_of = lambda t: (t // per_b, t % per_b + first_tile)
    tok = lambda w: pl.BlockSpec((1, TQ, w), lambda t: tile_of(t) + (0,))
    slot = lambda k: pl.BlockSpec(
        (TQ * SUBLANES, LANES), lambda t: (k * NB * tiles_b + tile_of(t)[0] * tiles_b + tile_of(t)[1], 0))
    return pl.pallas_call(
        functools.partial(_combine_kernel, first_tile),
        out_shape=jax.ShapeDtypeStruct((NB, per_b * TQ, D), F32),
        grid=(NB * per_b,),
        in_specs=[tok(D),
                  pl.BlockSpec((1, 4 * TOP_K, TQ),
                               lambda t: (tile_of(t)[0] * (LT // TM_TOK) + tile_of(t)[1] // (TM_TOK // TQ), 0,
                                          tile_of(t)[1] % (TM_TOK // TQ))),
                  _layer_spec(l, (16, N_MOD * D)),
                  slot(0), slot(1), slot(2), slot(3)],
        out_specs=pl.BlockSpec((1, TQ, D), lambda t: (t // per_b, t % per_b, 0)),
        compiler_params=pltpu.CompilerParams(
            dimension_semantics=("arbitrary",), vmem_limit_bytes=VMEM_LIMIT),
        name="moe_combine",
    )(x1, route, mod, gathered, gathered, gathered, gathered)


def _rope_tables():
    pos = jnp.arange(L_LAT)
    row = (pos // GRID_W).astype(F32)
    col = (pos % GRID_W).astype(F32)
    inv = ROPE_BASE ** (-jnp.arange(0, 32, 2, dtype=F32) / 32)
    ang_r = row[:, None] * inv
    ang_c = col[:, None] * inv
    ang = jnp.concatenate([ang_r, ang_r, ang_c, ang_c], axis=-1)
    cos = jnp.concatenate([jnp.ones((L_CTX, HEAD_DIM), F32), jnp.cos(ang)], axis=0)
    sin = jnp.concatenate([jnp.zeros((L_CTX, HEAD_DIM), F32), jnp.sin(ang)], axis=0)
    sign = jnp.where((jnp.arange(HEAD_DIM) % 32) < 16, -1.0, 1.0).astype(F32)
    return jnp.tile(cos, (1, 2)), jnp.tile(sin * sign, (1, 2))


def _s5_tables(a_re, a_im, log_dt, b_re, b_im, c_re, c_im):
    dt = jnp.exp(log_dt)[..., None]
    mag = jnp.exp(a_re * dt)
    ar = mag * jnp.cos(a_im * dt)
    ai = mag * jnp.sin(a_im * dt)
    den = a_re * a_re + a_im * a_im
    qr = ((ar - 1) * a_re + ai * a_im) / den
    qi = (ai * a_re - (ar - 1) * a_im) / den
    bbr = qr[..., None] * b_re - qi[..., None] * b_im
    bbi = qr[..., None] * b_im + qi[..., None] * b_re
    eye = jnp.eye(SSM_GROUPS, dtype=F32)
    to_in = lambda m: jnp.einsum('ldgnp,gh->ldgphn', m, eye).reshape(DEPTH, 2, SSM_W, NS)
    bm = jnp.concatenate([to_in(bbr), to_in(bbi)], axis=-1).astype(BF)
    to_out = lambda m: jnp.einsum('ldgpn,gh->ldgnhp', m, eye).reshape(DEPTH, 2, NS, SSM_W)
    cm = jnp.concatenate([to_out(c_re), -to_out(c_im)], axis=2).astype(BF)
    a_tab = jnp.stack([ar.reshape(DEPTH, 2, NS), ai.reshape(DEPTH, 2, NS)], axis=2)
    return a_tab, bm, cm


def _interleave_perms():
    rows = NB * T_PERM
    r = jnp.arange(rows)
    src = (r % NB) * T_PERM + r // NB
    pin = (src[:, None] == jnp.arange(rows)[None, :]).astype(BF)
    return pin, pin.T


def kernel(x, c, ctx, c_ctx, w_mod, b_mod, norm1_w, norm2_w, w_in, q_norm_w, k_norm_w, attn_sink,
           ssm_a_re, ssm_a_im, ssm_log_dt, ssm_b_re, ssm_b_im, ssm_c_re, ssm_c_im, ssm_d, glu_w, glu_b,
           pool_w, pool_scale, out_norm_w, w_out, router_w, router_b, exp_w_gu, exp_b_gu, exp_w_down,
           exp_b_down):
    xc = jnp.concatenate([ctx, x], axis=1)
    cc = jnp.concatenate([c, c_ctx[None, :], jnp.zeros((16 - NB - 1, D), F32)], axis=0)
    mod = _adaln(cc, w_mod, b_mod)
    cos2, sin2 = _rope_tables()
    pin, pout = _interleave_perms()
    tri = (jnp.arange(TM_TOK)[:, None] < jnp.arange(TM_TOK)[None, :]).astype(BF)

    row = lambda a: a.reshape(DEPTH, 1, -1)
    dup = lambda m: jnp.concatenate([m[..., :64], m[..., :64], m[..., 64:], m[..., 64:]], axis=-1)
    w_ext = jnp.concatenate([w_in[..., :512], dup(w_in[..., 512:640]), dup(w_in[..., 640:768]),
                             w_in[..., 768:]], axis=-1).astype(BF)
    qw2 = row(jnp.tile(q_norm_w, (1, 2)))
    kw2 = row(jnp.tile(k_norm_w, (1, 2)))
    a_tab, bm, cm = _s5_tables(ssm_a_re, ssm_a_im, ssm_log_dt, ssm_b_re, ssm_b_im, ssm_c_re, ssm_c_im)
    pw_bd = jnp.einsum('lgcd,gh->lgchd', pool_w, jnp.eye(4, dtype=F32)).reshape(DEPTH, POOL_W, POOL_W).astype(BF)
    rw = jnp.concatenate([router_w, jnp.zeros((DEPTH, D, LANES - N_EXP), F32)], axis=-1)
    rb = row(jnp.concatenate([router_b, jnp.full((DEPTH, LANES - N_EXP), -1e30, F32)], axis=-1))
    glu_w_bf = glu_w.astype(BF)
    w_out_bf = w_out.astype(BF)
    n1, n2, onw = row(norm1_w), row(norm2_w), row(out_norm_w)
    ssm_d3, glu_b3, pool_sc3 = row(ssm_d), row(glu_b), row(pool_scale)
    sink = attn_sink.reshape(-1)

    for l in range(DEPTH):
        q, kd, vd, u, p = _inproj(l, xc, mod, n1, w_ext, qw2, kw2, cos2, sin2)
        attn = _attention(l, sink, q, kd, vd, onw)
        yf, yb = _s5_scan(l, u, pin, pout, a_tab, bm, cm)
        pool = _pool(l, p, pw_bd, pool_sc3)
        x1, h2, route, cnt = _mix(l, xc, attn, yf, yb, u, pool, mod, ssm_d3, glu_w_bf, glu_b3, onw,
                                  w_out_bf, n2, rw, rb, tri)

        by_k = lambda lo: route[:, lo:lo + TOP_K, :].transpose(1, 0, 2).reshape(TOP_K, N_TOK)
        top_i = by_k(0).astype(jnp.int32)
        rank = by_k(2 * TOP_K).astype(jnp.int32)
        counts = cnt[:, 0].astype(jnp.int32)
        padded = (counts + TM_MOE - 1) // TM_MOE * TM_MOE
        pend = jnp.cumsum(padded)
        pstart = pend - padded
        experts = jnp.arange(N_EXP, dtype=jnp.int32)
        dest_t = jnp.sum(jnp.where(top_i[..., None] == experts, pstart, 0), axis=-1) + rank
        n_used = (pend[-1] // TM_MOE).astype(jnp.int32)[None]
        tile_lo = jnp.arange(NT_MOE, dtype=jnp.int32) * TM_MOE
        tile_e = jnp.minimum(jnp.sum(pend[None, :] <= tile_lo[:, None], axis=1), N_EXP - 1).astype(jnp.int32)
        change = jnp.concatenate([jnp.ones((1,), jnp.int32), (tile_e[1:] != tile_e[:-1]).astype(jnp.int32)])
        slot = (jnp.cumsum(change) - 1) % 2
        next_first = jnp.sum(jnp.where(tile_e[:, None] == experts, pend // TM_MOE, 0), axis=1)
        tiles = jnp.arange(NT_MOE, dtype=jnp.int32)
        next_e = jnp.sum(jnp.where(next_first[:, None] == tiles, tile_e, 0), axis=1)
        nxt = jnp.where(next_first < n_used[0], next_e, -1)
        xs = _row_dispatch(h2, dest_t, R_MOE)

        yb_rows = _moe(l, tile_e, n_used, slot.astype(jnp.int32), nxt.astype(jnp.int32), xs,
                       exp_w_gu, exp_b_gu, exp_w_down, exp_b_down)
        gathered = _row_gather(yb_rows, dest_t.reshape(-1))
        xc = _combine(l, x1, route, mod, gathered, latent_only=(l == DEPTH - 1))
    return xc
```

```python
import functools
import math

import jax
import jax.numpy as jnp
from jax import lax
from jax.experimental import pallas as pl
from jax.experimental.pallas import tpu as pltpu
from jax.experimental.pallas import tpu_sc as plsc

D = 1024
NB = 8
L_LAT = 2048
L_CTX = 256
LT = L_CTX + L_LAT
DEPTH = 4
N_MOD = 6
EPS = 1e-6
N_HEADS = 8
HEAD_DIM = 64
ATTN_W = 512
WINDOW = 128
ATTN_SCALE = HEAD_DIM ** -0.5
LOG2E = math.log2(math.e)
ROPE_BASE = 10000.0
GRID_W = 64
SSM_W = 256
SSM_GROUP = 16
SSM_GROUPS = 16
SSM_STATE = 64
NS = SSM_GROUPS * SSM_STATE
POOL_W = 256
POOL_GROUP = 64
N_EXP = 32
TOP_K = 4
D_FF = 1024
SWIGLU_LIMIT = 7.0
SWIGLU_ALPHA = 1.702

LANES = 128
SUBLANES = 8
VMEM_LIMIT = 56 * 1024 * 1024

TM_TOK = 768
TQ = 256
KSPAN = TQ + 2 * WINDOW
T_SCAN = 128
N_CHUNK = LT // T_SCAN
N_CHUNK_CTX = L_CTX // T_SCAN
TM_MOE = 512
N_TOK = NB * LT
N_ASSIGN = N_TOK * TOP_K
NT_MOE = N_ASSIGN // TM_MOE + N_EXP
R_MOE = NT_MOE * TM_MOE
FF_CHUNK = 512
OUT_BLOCK = 256

SC_CORES = 2
SC_SUBCORES = 16
SC_WORKERS = SC_CORES * SC_SUBCORES
SC_CHUNK = 32

C_Q = 0
C_K = 512
C_V = 768
C_U = 1024
C_P = 1280
NW_IN = 1536

BF = jnp.bfloat16
F32 = jnp.float32


def _split_bf16(a):
    hi = a.astype(BF)
    lo = (a - hi.astype(F32)).astype(BF)
    return hi, lo


def _dot(a, b):
    return jnp.dot(a, b, preferred_element_type=F32)


def _dot3(a, b):
    ah, al = _split_bf16(a)
    bh, bl = _split_bf16(b)
    return _dot(ah, bh) + (_dot(ah, bl) + _dot(al, bh))


def _mod_kernel(c_ref, w_ref, b_ref, o_ref):
    c = c_ref[...]
    s = c * jax.nn.sigmoid(c)
    o_ref[0] = _dot3(s, w_ref[0]) + b_ref[0]


def _adaln(cc, w_mod, b_mod):
    tn = 1536
    return pl.pallas_call(
        _mod_kernel,
        out_shape=jax.ShapeDtypeStruct((DEPTH, 16, N_MOD * D), F32),
        grid=(DEPTH, N_MOD * D // tn),
        in_specs=[
            pl.BlockSpec((16, D), lambda l, j: (0, 0)),
            pl.BlockSpec((1, D, tn), lambda l, j: (l, 0, j)),
            pl.BlockSpec((1, 1, tn), lambda l, j: (l, 0, j)),
        ],
        out_specs=pl.BlockSpec((1, 16, tn), lambda l, j: (l, 0, j)),
        compiler_params=pltpu.CompilerParams(
            dimension_semantics=("arbitrary", "arbitrary"), vmem_limit_bytes=VMEM_LIMIT),
        name="adaln_mod",
    )(cc, w_mod, b_mod.reshape(DEPTH, 1, N_MOD * D))


def _layer_spec(l, shape, col_block=0):
    shape = tuple(shape)
    return pl.BlockSpec((None,) + shape, lambda *_: (l,) + (0,) * (len(shape) - 1) + (col_block,))


def _with_mod(fn, mod_ref, b, i, idxs, *arrays):
    lat = [mod_ref[pl.ds(b, 1), k * D:(k + 1) * D] for k in idxs]
    top = [jnp.where(i == 0, mod_ref[NB:NB + 1, k * D:(k + 1) * D], v) for k, v in zip(idxs, lat)]
    return jnp.concatenate([fn(*[a[:L_CTX] for a in arrays], *top),
                            fn(*[a[L_CTX:] for a in arrays], *lat)], axis=0)


def _rmsnorm(x, w):
    ms = jnp.mean(x * x, axis=-1, keepdims=True)
    return x * lax.rsqrt(ms + EPS) * w


def _headnorm_pair(t, w2, lane):
    sq = t * t
    first = lane < HEAD_DIM
    a = jnp.sum(jnp.where(first, sq, 0.0), axis=-1, keepdims=True)
    b = jnp.sum(jnp.where(first, 0.0, sq), axis=-1, keepdims=True)
    ms = jnp.where(first, a, b) * (1.0 / HEAD_DIM)
    return t * lax.rsqrt(ms + EPS) * w2


def _rope_pair(t, cos, sin_signed, lane):
    lower = (lane % 32) < 16
    partner = jnp.where(lower, pltpu.roll(t, LANES - 16, 1), pltpu.roll(t, 16, 1))
    return t * cos + partner * sin_signed


def _inproj_kernel(x_ref, mod_ref, n1_ref, w_ref, qw_ref, kw_ref, cos_ref, sin_ref,
                   q_ref, k_ref, v_ref, u_ref, p_ref):
    b = pl.program_id(0)
    i = pl.program_id(1)
    h = _rmsnorm(x_ref[0], n1_ref[...])
    h = _with_mod(lambda hr, shift, scale: (hr * (1.0 + scale) + shift).astype(BF), mod_ref, b, i, (0, 1), h)
    px = _dot(h, w_ref[...])
    lane = lax.broadcasted_iota(jnp.int32, (TM_TOK, LANES), 1)
    cos = cos_ref[...]
    sin = sin_ref[...]
    for j in range(ATTN_W // LANES):
        t = px[:, C_Q + j * LANES:C_Q + (j + 1) * LANES]
        t = _rope_pair(_headnorm_pair(t, qw_ref[...], lane), cos, sin, lane) * (ATTN_SCALE * LOG2E)
        q_ref[0, :, j * LANES:(j + 1) * LANES] = t.astype(BF)
    for g in range(2):
        t = px[:, C_K + g * LANES:C_K + (g + 1) * LANES]
        t = _rope_pair(_headnorm_pair(t, kw_ref[...], lane), cos, sin, lane)
        k_ref[0, :, g * LANES:(g + 1) * LANES] = t.astype(BF)
        t = px[:, C_V + g * LANES:C_V + (g + 1) * LANES]
        v_ref[0, :, g * LANES:(g + 1) * LANES] = jnp.where(lane < HEAD_DIM, t, 1.0).astype(BF)
    u_ref[0] = px[:, C_U:C_P]
    p_ref[0] = px[:, C_P:NW_IN]


def _inproj(l, xc, mod, n1, w_ext, qw2, kw2, cos2, sin2):
    tok = lambda w: pl.BlockSpec((1, TM_TOK, w), lambda b, i: (b, i, 0))
    full = lambda s: _layer_spec(l, s)
    return pl.pallas_call(
        _inproj_kernel,
        out_shape=(
            jax.ShapeDtypeStruct((NB, LT, ATTN_W), BF),
            jax.ShapeDtypeStruct((NB, LT, 256), BF),
            jax.ShapeDtypeStruct((NB, LT, 256), BF),
            jax.ShapeDtypeStruct((NB, LT, SSM_W), F32),
            jax.ShapeDtypeStruct((NB, LT, POOL_W), F32),
        ),
        grid=(NB, LT // TM_TOK),
        in_specs=[
            tok(D), full((16, N_MOD * D)), full((1, D)), full((D, NW_IN)),
            full((1, LANES)), full((1, LANES)),
            pl.BlockSpec((TM_TOK, LANES), lambda b, i: (i, 0)),
            pl.BlockSpec((TM_TOK, LANES), lambda b, i: (i, 0)),
        ],
        out_specs=(tok(ATTN_W), tok(256), tok(256), tok(SSM_W), tok(POOL_W)),
        compiler_params=pltpu.CompilerParams(
            dimension_semantics=("arbitrary", "arbitrary"), vmem_limit_bytes=VMEM_LIMIT),
        name="inproj",
    )(xc, mod, n1, w_ext, qw2, kw2, cos2, sin2)


def _attn_kernel(l, sink_ref, q_ref, k_ref, v_ref, onw_ref, o_ref, bias_ref):
    j = pl.program_id(1)
    start = pl.multiple_of(jnp.clip(j * TQ - WINDOW, LANES, LT - KSPAN), LANES)
    lane = lax.broadcasted_iota(jnp.int32, (TQ, LANES), 1)
    first = lane < HEAD_DIM
    row2 = lax.broadcasted_iota(jnp.int32, (2 * TQ, 1), 0)
    qpos = j * TQ + jnp.where(row2 < TQ, row2, row2 - TQ)
    kpos = start + lax.broadcasted_iota(jnp.int32, (1, KSPAN), 1)
    valid = (jnp.abs(qpos - kpos) <= WINDOW) & (kpos >= L_CTX) & (j >= 1)
    bias_ref[...] = jnp.where(valid, 0.0, -1e30)
    n_pairs = N_HEADS // 2

    def kv(ref, hp, rows):
        g = hp // 2
        return ref[0, rows, g * LANES:(g + 1) * LANES]

    def scores(hp):
        q2 = q_ref[0, :, hp * LANES:(hp + 1) * LANES]
        zero = jnp.zeros_like(q2)
        qs = jnp.concatenate([jnp.where(first, q2, zero), jnp.where(first, zero, q2)], axis=0)
        dn = (((1,), (1,)), ((), ()))
        s_ctx = lax.dot_general(qs, kv(k_ref, hp, slice(0, L_CTX)), dn, preferred_element_type=F32)
        s_loc = lax.dot_general(qs, kv(k_ref, hp, pl.ds(start, KSPAN)), dn, preferred_element_type=F32)
        return s_loc, s_ctx

    nxt = scores(0)
    outs = []
    for hp in range(n_pairs):
        s_loc, s_ctx = nxt
        if hp + 1 < n_pairs:
            nxt = scores(hp + 1)
        s_loc = s_loc + bias_ref[...]
        sink = jnp.where(row2 < TQ, sink_ref[l * N_HEADS + 2 * hp],
                         sink_ref[l * N_HEADS + 2 * hp + 1]) * LOG2E
        m = jnp.maximum(jnp.maximum(jnp.max(s_loc, axis=-1, keepdims=True),
                                    jnp.max(s_ctx, axis=-1, keepdims=True)), sink)
        e_loc = jnp.exp2(s_loc - m)
        e_ctx = jnp.exp2(s_ctx - m)
        o = (_dot(e_loc.astype(BF), kv(v_ref, hp, pl.ds(start, KSPAN)))
             + _dot(e_ctx.astype(BF), kv(v_ref, hp, slice(0, L_CTX))))
        den = pltpu.roll(o + jnp.exp2(sink - m), HEAD_DIM, 1)
        o = o / den
        outs.append(jnp.where(first, o[0:TQ], pltpu.roll(o[TQ:2 * TQ], HEAD_DIM, 1)))
    o_ref[0] = _rmsnorm(jnp.concatenate(outs, axis=1), onw_ref[...]).astype(BF)


def _attention(l, sink, q, kd, vd, onw):
    return pl.pallas_call(
        functools.partial(_attn_kernel, l),
        out_shape=jax.ShapeDtypeStruct((NB, LT, ATTN_W), BF),
        grid_spec=pltpu.PrefetchScalarGridSpec(
            num_scalar_prefetch=1,
            grid=(NB, LT // TQ),
            in_specs=[
                pl.BlockSpec((1, TQ, ATTN_W), lambda b, j, s: (b, j, 0)),
                pl.BlockSpec((1, LT, 256), lambda b, j, s: (b, 0, 0)),
                pl.BlockSpec((1, LT, 256), lambda b, j, s: (b, 0, 0)),
                _layer_spec(l, (1, ATTN_W)),
            ],
            out_specs=pl.BlockSpec((1, TQ, ATTN_W), lambda b, j, s: (b, j, 0)),
            scratch_shapes=[pltpu.VMEM((2 * TQ, KSPAN), F32)],
        ),
        compiler_params=pltpu.CompilerParams(
            dimension_semantics=("arbitrary", "arbitrary"), vmem_limit_bytes=VMEM_LIMIT),
        name="window_attn",
    )(sink, q, kd, vd, onw)


def _bwd_chunk(i):
    return jnp.where(i < N_CHUNK_CTX, N_CHUNK_CTX - 1 - i, N_CHUNK - 1 - (i - N_CHUNK_CTX))


def _scan_kernel(uf_ref, ub_ref, a_ref, bm_ref, cm_ref, yf_ref, yb_ref,
                 xf_ref, xb_ref, st_ref, *il_refs):
    i = pl.program_id(0)
    halves = SSM_W // LANES

    @pl.when(i == 0)
    def _():
        st_ref[...] = jnp.zeros_like(st_ref)

    def project(d, u_ref, xs_ref):
        il = il_refs[d * halves:(d + 1) * halves]
        for h in range(halves):
            for bb in range(NB):
                il[h][pl.ds(bb, T_SCAN, stride=NB), :] = u_ref[bb, :, h * LANES:(h + 1) * LANES]
        ui = jnp.concatenate([r[...] for r in il], axis=1).astype(BF)
        xs_ref[...] = _dot(ui, bm_ref[d])

    def scan(d, xs_ref, reverse):
        ar = jnp.broadcast_to(a_ref[d, 0:1, :], (NB, NS))
        ai = jnp.broadcast_to(a_ref[d, 1:2, :], (NB, NS))
        sr = st_ref[d, 0]
        si = st_ref[d, 1]
        for t in (range(T_SCAN - 1, -1, -1) if reverse else range(T_SCAN)):
            r = pl.ds(t * NB, NB)
            nr = ar * sr - ai * si + xs_ref[r, 0:NS]
            ni = ar * si + ai * sr + xs_ref[r, NS:2 * NS]
            sr, si = nr, ni
            xs_ref[r, 0:NS] = sr
            xs_ref[r, NS:2 * NS] = si
        st_ref[d, 0] = sr
        st_ref[d, 1] = si

    def readout(d, xs_ref, y_ref):
        y = _dot(xs_ref[...].astype(BF), cm_ref[d])
        il = il_refs[d * halves:(d + 1) * halves]
        for h in range(halves):
            il[h][...] = y[:, h * LANES:(h + 1) * LANES]
            for bb in range(NB):
                y_ref[bb, :, h * LANES:(h + 1) * LANES] = il[h][pl.ds(bb, T_SCAN, stride=NB), :]

    project(0, uf_ref, xf_ref)
    project(1, ub_ref, xb_ref)
    scan(0, xf_ref, False)
    readout(0, xf_ref, yf_ref)
    scan(1, xb_ref, True)
    readout(1, xb_ref, yb_ref)


def _s5_scan(l, u, a_tab, bm, cm):
    rows = NB * T_SCAN
    layer = lambda s: _layer_spec(l, s)
    chunk_f = pl.BlockSpec((NB, T_SCAN, SSM_W), lambda i: (0, i, 0))
    chunk_b = pl.BlockSpec((NB, T_SCAN, SSM_W), lambda i: (0, _bwd_chunk(i), 0))
    return pl.pallas_call(
        _scan_kernel,
        out_shape=(jax.ShapeDtypeStruct((NB, LT, SSM_W), F32),
                   jax.ShapeDtypeStruct((NB, LT, SSM_W), F32)),
        grid=(N_CHUNK,),
        in_specs=[chunk_f, chunk_b, layer((2, 2, NS)), layer((2, SSM_W, 2 * NS)), layer((2, 2 * NS, SSM_W))],
        out_specs=(chunk_f, chunk_b),
        scratch_shapes=[pltpu.VMEM((rows, 2 * NS), F32), pltpu.VMEM((rows, 2 * NS), F32),
                        pltpu.VMEM((2, 2, NB, NS), F32),
                        ] + [pltpu.VMEM((rows, LANES), F32)] * (2 * (SSM_W // LANES)),
        compiler_params=pltpu.CompilerParams(
            dimension_semantics=("arbitrary",), vmem_limit_bytes=VMEM_LIMIT),
        name="s5_scan",
    )(u, u, a_tab, bm, cm)


def _pool_segment(ps, length):
    n = length + 2 * SUBLANES
    z = jnp.zeros((SUBLANES, POOL_W), F32)
    pe = jnp.concatenate([z, ps, z], axis=0)
    a1 = pe + pltpu.roll(pe, 1, 0)
    a2 = a1 + pltpu.roll(a1, 2, 0)
    a3 = a2 + pltpu.roll(a2, 4, 0)
    a4 = a3 + pltpu.roll(a3, 8, 0)
    lane = lax.broadcasted_iota(jnp.int32, (1, POOL_W), 1)
    half = jnp.where(lane < 64, 1, jnp.where(lane < 128, 2, jnp.where(lane < 192, 4, 8)))
    s = jnp.where(lane < 64, a1,
                  jnp.where(lane < 128, pltpu.roll(a2, n - 1, 0),
                            jnp.where(lane < 192, pltpu.roll(a3, n - 3, 0), pltpu.roll(a4, n - 7, 0))))
    s = s[SUBLANES:SUBLANES + length]
    t = lax.broadcasted_iota(jnp.int32, (length, 1), 0)
    cnt = jnp.minimum(t + half, length) - jnp.maximum(t - half, 0)
    return s / cnt.astype(F32) - ps


def _pool_kernel(p_ref, w_ref, sc_ref, o_ref):
    for lo, length in ((0, L_CTX), (L_CTX, L_LAT)):
        dlt = _pool_segment(p_ref[0, lo:lo + length, :], length)
        y = _dot(dlt.astype(BF), w_ref[...]) * sc_ref[...]
        o_ref[0, lo:lo + length, :] = y.astype(BF)


def _pool(l, p, w_bd, scale):
    return pl.pallas_call(
        _pool_kernel,
        out_shape=jax.ShapeDtypeStruct((NB, LT, POOL_W), BF),
        grid=(NB,),
        in_specs=[pl.BlockSpec((1, LT, POOL_W), lambda b: (b, 0, 0)),
                  _layer_spec(l, (POOL_W, POOL_W)),
                  _layer_spec(l, (1, POOL_W))],
        out_specs=pl.BlockSpec((1, LT, POOL_W), lambda b: (b, 0, 0)),
        compiler_params=pltpu.CompilerParams(
            dimension_semantics=("arbitrary",), vmem_limit_bytes=VMEM_LIMIT),
        name="pool",
    )(p, w_bd, scale)


def _mix_kernel(route_ctx, x_ref, at_ref, yf_ref, yb_ref, u_ref, pl_ref, mod_ref, d_ref, gw_ref, gb_ref, onw_ref,
                wo_ref, n2_ref, rw_ref, rb_ref, tri_ref,
                x1_ref, h2_ref, rt_ref, cnt_ref, run_ref):
    b = pl.program_id(0)
    i = pl.program_id(1)

    @pl.when((b == 0) & (i == 0))
    def _():
        run_ref[...] = jnp.zeros_like(run_ref)

    y = yf_ref[0] + yb_ref[0] + d_ref[...] * u_ref[0]
    g = jax.nn.gelu(y, approximate=True)
    s = g * jax.nn.sigmoid(_dot(g.astype(BF), gw_ref[...]) + gb_ref[...])
    s = _rmsnorm(s, onw_ref[...]).astype(BF)
    mix = (_dot(at_ref[0], wo_ref[0:ATTN_W, :])
           + _dot(s, wo_ref[ATTN_W:ATTN_W + SSM_W, :])
           + _dot(pl_ref[0], wo_ref[ATTN_W + SSM_W:D, :]))
    x1 = _with_mod(lambda xr, mr, gate: xr + gate * mr, mod_ref, b, i, (2,), x_ref[0], mix)
    x1_ref[0] = x1
    h2 = _rmsnorm(x1, n2_ref[...])
    h2 = _with_mod(lambda hr, shift, scale: hr * (1.0 + scale) + shift, mod_ref, b, i, (3, 4), h2)
    for j in range(D // LANES):
        h2_ref[pl.ds(j, TM_TOK, stride=SUBLANES), :] = h2[:, j * LANES:(j + 1) * LANES]
    logits = (_dot3(h2, rw_ref[...]) + rb_ref[...]).T[0:N_EXP]
    eidx = lax.broadcasted_iota(jnp.int32, (N_EXP, TM_TOK), 0).astype(F32)
    vals, idxs, hits = [], [], []
    cur = logits
    for _ in range(TOP_K):
        m = jnp.max(cur, axis=0, keepdims=True)
        idx = jnp.min(jnp.where(cur == m, eidx, float(N_EXP)), axis=0, keepdims=True)
        hit = eidx == idx
        vals.append(m)
        idxs.append(idx)
        hits.append(hit)
        cur = jnp.where(hit, -jnp.inf, cur)
    ex = [jnp.exp(v - vals[0]) for v in vals]
    den = ex[0] + ex[1] + ex[2] + ex[3]
    onehot = jnp.where(hits[0] | hits[1] | hits[2] | hits[3], 1.0, 0.0)
    if not route_ctx:
        tok = lax.broadcasted_iota(jnp.int32, (1, TM_TOK), 1)
        onehot = jnp.where((tok >= L_CTX) | (i > 0), onehot, 0.0)
    run = run_ref[:, 0:1]
    before = _dot(onehot.astype(BF), tri_ref[...]) + run
    row = lax.broadcasted_iota(jnp.int32, (4 * TOP_K, TM_TOK), 0)
    route = jnp.zeros((4 * TOP_K, TM_TOK), F32)
    for k in range(TOP_K):
        rank = jnp.sum(jnp.where(hits[k], before, 0.0), axis=0, keepdims=True)
        route = jnp.where(row == k, idxs[k], route)
        route = jnp.where(row == TOP_K + k, ex[k] / den, route)
        route = jnp.where(row == 2 * TOP_K + k, rank, route)
    rt_ref[0] = route
    run_ref[...] = jnp.broadcast_to(run + jnp.sum(onehot, axis=1, keepdims=True), (N_EXP, LANES))
    cnt_ref[...] = run_ref[...]


def _mix(l, route_ctx, xc, attn, yf, yb, u, pool, mod, ssm_d, glu_w, glu_b, onw, w_out, n2, rw, rb, tri):
    tok = lambda w: pl.BlockSpec((1, TM_TOK, w), lambda b, i: (b, i, 0))
    full = lambda s: pl.BlockSpec(s, lambda b, i: (0,) * len(s))
    layer = lambda s, col=0: _layer_spec(l, s, col)
    return pl.pallas_call(
        functools.partial(_mix_kernel, route_ctx),
        out_shape=(jax.ShapeDtypeStruct((NB, LT, D), F32),
                   jax.ShapeDtypeStruct((N_TOK * SUBLANES, LANES), F32),
                   jax.ShapeDtypeStruct((N_TOK // TM_TOK, 4 * TOP_K, TM_TOK), F32),
                   jax.ShapeDtypeStruct((N_EXP, LANES), F32)),
        grid=(NB, LT // TM_TOK),
        in_specs=[tok(D), tok(ATTN_W), tok(SSM_W), tok(SSM_W), tok(SSM_W), tok(POOL_W),
                  layer((16, N_MOD * D)), layer((1, SSM_W)), layer((SSM_W, SSM_W)), layer((1, SSM_W)),
                  layer((1, SSM_W), ATTN_W // SSM_W), layer((D, D)), layer((1, D)), layer((D, LANES)),
                  layer((1, LANES)), full((TM_TOK, TM_TOK))],
        out_specs=(tok(D),
                   pl.BlockSpec((TM_TOK * SUBLANES, LANES), lambda b, i: (b * (LT // TM_TOK) + i, 0)),
                   pl.BlockSpec((1, 4 * TOP_K, TM_TOK), lambda b, i: (b * (LT // TM_TOK) + i, 0, 0)),
                   full((N_EXP, LANES))),
        scratch_shapes=[pltpu.VMEM((N_EXP, LANES), F32)],
        compiler_params=pltpu.CompilerParams(
            dimension_semantics=("arbitrary", "arbitrary"), vmem_limit_bytes=VMEM_LIMIT),
        name="mix_router",
    )(xc, attn, yf, yb, u, pool, mod, ssm_d, glu_w, glu_b, onw, w_out, n2, rw, rb, tri)


def _rows_from_tiles(ref, rows):
    return jnp.concatenate(
        [ref[pl.ds(j, rows, stride=SUBLANES), :] for j in range(D // LANES)], axis=1)


def _rows_to_tiles(ref, val, rows):
    for j in range(D // LANES):
        ref[pl.ds(j, rows, stride=SUBLANES), :] = val[:, j * LANES:(j + 1) * LANES]


def _moe_kernel(l, te_ref, nu_ref, slot_ref, nxt_ref, x_ref, wgu_hbm, bgu_ref, wd_hbm, bd_ref, o_ref,
                wgu_f, wd_f, wgu_s, wd_s, act_s, sem):
    i = pl.program_id(0)

    @pl.when(i >= nu_ref[0])
    def _():
        o_ref[...] = jnp.zeros_like(o_ref)

    def fetch(e, slot):
        return (pltpu.make_async_copy(wgu_hbm.at[l, e], wgu_f.at[slot], sem.at[0, slot]),
                pltpu.make_async_copy(wd_hbm.at[l, e], wd_f.at[slot], sem.at[1, slot]))

    @pl.when(i < nu_ref[0])
    def _():
        e = te_ref[i]
        prev = te_ref[jnp.maximum(i - 1, 0)]
        slot = slot_ref[i]

        @pl.when(i == 0)
        def _():
            for cp in fetch(e, slot):
                cp.start()

        @pl.when((i == 0) | (e != prev))
        def _():
            for cp in fetch(e, slot):
                cp.wait()
            nxt = nxt_ref[i]

            @pl.when(nxt >= 0)
            def _():
                for cp in fetch(nxt, 1 - slot):
                    cp.start()

            def cast_gu(r, c):
                rs = pl.ds(pl.multiple_of(r * 128, 128), 128)
                wgu_s[rs, :] = wgu_f[slot, rs, :].astype(BF)
                return c

            def cast_d(r, c):
                rs = pl.ds(pl.multiple_of(r * 128, 128), 128)
                wd_s[rs, :] = wd_f[slot, rs, :].astype(BF)
                return c

            lax.fori_loop(0, D // 128, cast_gu, 0)
            lax.fori_loop(0, D_FF // 128, cast_d, 0)

        x = _rows_from_tiles(x_ref, TM_MOE).astype(BF)
        for c in range(D_FF // FF_CHUNK):
            lo = c * FF_CHUNK
            gate = _dot(x, wgu_s[:, lo:lo + FF_CHUNK]) + bgu_ref[:, lo:lo + FF_CHUNK]
            up = (_dot(x, wgu_s[:, D_FF + lo:D_FF + lo + FF_CHUNK])
                  + bgu_ref[:, D_FF + lo:D_FF + lo + FF_CHUNK])
            gate = jnp.minimum(gate, SWIGLU_LIMIT)
            up = jnp.clip(up, -SWIGLU_LIMIT, SWIGLU_LIMIT)
            act = (up + 1.0) * (gate * jax.nn.sigmoid(SWIGLU_ALPHA * gate))
            act_s[:, lo:lo + FF_CHUNK] = act.astype(BF)
        for n in range(D // OUT_BLOCK):
            lo = n * OUT_BLOCK
            y = _dot(act_s[...], wd_s[:, lo:lo + OUT_BLOCK]) + bd_ref[:, lo:lo + OUT_BLOCK]
            for j in range(OUT_BLOCK // LANES):
                o_ref[pl.ds(lo // LANES + j, TM_MOE, stride=SUBLANES), :] = y[:, j * LANES:(j + 1) * LANES]


def _moe(l, tile_e, n_used, slot, nxt, xs, w_gu, b_gu, w_down, b_down):
    def tile(i, te, nu, *_):
        return (jnp.minimum(i, nu[0] - 1), 0)

    def bias(i, te, nu, *_):
        return (l, te[jnp.minimum(i, nu[0] - 1)], 0, 0)

    return pl.pallas_call(
        functools.partial(_moe_kernel, l),
        out_shape=jax.ShapeDtypeStruct((R_MOE * SUBLANES, LANES), F32),
        grid_spec=pltpu.PrefetchScalarGridSpec(
            num_scalar_prefetch=4,
            grid=(NT_MOE,),
            in_specs=[
                pl.BlockSpec((TM_MOE * SUBLANES, LANES), tile),
                pl.BlockSpec(memory_space=pl.ANY),
                pl.BlockSpec((None, None, 1, 2 * D_FF), bias),
                pl.BlockSpec(memory_space=pl.ANY),
                pl.BlockSpec((None, None, 1, D), bias),
            ],
            out_specs=pl.BlockSpec((TM_MOE * SUBLANES, LANES), lambda i, *_: (i, 0)),
            scratch_shapes=[pltpu.VMEM((2, D, 2 * D_FF), F32), pltpu.VMEM((2, D_FF, D), F32),
                            pltpu.VMEM((D, 2 * D_FF), BF), pltpu.VMEM((D_FF, D), BF),
                            pltpu.VMEM((TM_MOE, D_FF), BF), pltpu.SemaphoreType.DMA((2, 2))],
        ),
        compiler_params=pltpu.CompilerParams(
            dimension_semantics=("arbitrary",), vmem_limit_bytes=VMEM_LIMIT),
        name="moe_experts",
    )(tile_e, n_used, slot, nxt, xs, w_gu, b_gu.reshape(DEPTH, N_EXP, 1, 2 * D_FF), w_down,
      b_down.reshape(DEPTH, N_EXP, 1, D))


def _row_gather(table, idx):
    n_rows = idx.shape[0]
    per_worker = n_rows // SC_WORKERS
    n_chunks = per_worker // SC_CHUNK
    assert per_worker * SC_WORKERS == n_rows and n_chunks * SC_CHUNK == per_worker and n_chunks % 2 == 0
    mesh = plsc.VectorSubcoreMesh(core_axis_name="c", subcore_axis_name="s")
    row_tile = (SC_CHUNK, SUBLANES, LANES)

    @functools.partial(
        pl.kernel, mesh=mesh,
        out_type=jax.ShapeDtypeStruct((n_rows, SUBLANES, LANES), F32),
        scratch_types=[pltpu.VMEM((n_chunks, SC_CHUNK), jnp.int32),
                       pltpu.VMEM(row_tile, F32), pltpu.VMEM(row_tile, F32),
                       pltpu.SemaphoreType.DMA, pltpu.SemaphoreType.DMA],
        name="sc_row_gather",
    )
    def gather(table_hbm, idx_hbm, out_hbm, idx_v, buf0, buf1, sem0, sem1):
        wid = lax.axis_index("s") * SC_CORES + lax.axis_index("c")
        pltpu.sync_copy(idx_hbm.at[pl.ds(wid * n_chunks, n_chunks)], idx_v)
        base = wid * per_worker

        def fetch(chunk, buf, sem):
            return pltpu.make_async_copy(table_hbm.at[idx_v.at[chunk]], buf, sem)

        def emit(chunk, buf):
            pltpu.sync_copy(buf, out_hbm.at[pl.ds(base + chunk * SC_CHUNK, SC_CHUNK)])

        fetch(0, buf0, sem0).start()

        @pl.loop(0, n_chunks, step=2)
        def _(c):
            fetch(c + 1, buf1, sem1).start()
            fetch(c, buf0, sem0).wait()
            emit(c, buf0)

            @pl.when(c + 2 < n_chunks)
            def _():
                fetch(c + 2, buf0, sem0).start()

            fetch(c + 1, buf1, sem1).wait()
            emit(c + 1, buf1)

    out = gather(table.reshape(-1, SUBLANES, LANES), idx.reshape(n_rows // SC_CHUNK, SC_CHUNK))
    return out.reshape(n_rows * SUBLANES, LANES)


def _row_dispatch(rows, dest_t, n_out):
    per_worker = N_TOK // SC_WORKERS
    n_chunks = per_worker // SC_CHUNK
    assert per_worker * SC_WORKERS == N_TOK and n_chunks * SC_CHUNK == per_worker and n_chunks % 2 == 0
    idx_rows = TOP_K * n_chunks
    assert idx_rows % SUBLANES == 0
    mesh = plsc.VectorSubcoreMesh(core_axis_name="c", subcore_axis_name="s")
    row_tile = (SC_CHUNK, SUBLANES, LANES)

    @functools.partial(
        pl.kernel, mesh=mesh,
        out_type=jax.ShapeDtypeStruct((n_out, SUBLANES, LANES), F32),
        scratch_types=[pltpu.VMEM((idx_rows, SC_CHUNK), jnp.int32),
                       pltpu.VMEM(row_tile, F32), pltpu.VMEM(row_tile, F32),
                       pltpu.SemaphoreType.DMA, pltpu.SemaphoreType.DMA],
        name="sc_row_dispatch",
    )
    def dispatch(rows_hbm, idx_hbm, out_hbm, idx_v, buf0, buf1, sem0, sem1):
        wid = lax.axis_index("s") * SC_CORES + lax.axis_index("c")
        pltpu.sync_copy(idx_hbm.at[pl.ds(wid * idx_rows, idx_rows)], idx_v)
        base = wid * per_worker

        def fetch(chunk, buf, sem):
            return pltpu.make_async_copy(rows_hbm.at[pl.ds(base + chunk * SC_CHUNK, SC_CHUNK)], buf, sem)

        def emit(chunk, buf):
            for k in range(TOP_K):
                pltpu.sync_copy(buf, out_hbm.at[idx_v.at[k * n_chunks + chunk]])

        fetch(0, buf0, sem0).start()

        @pl.loop(0, n_chunks, step=2)
        def _(c):
            fetch(c + 1, buf1, sem1).start()
            fetch(c, buf0, sem0).wait()
            emit(c, buf0)

            @pl.when(c + 2 < n_chunks)
            def _():
                fetch(c + 2, buf0, sem0).start()

            fetch(c + 1, buf1, sem1).wait()
            emit(c + 1, buf1)

    idx = dest_t.reshape(TOP_K, SC_WORKERS, n_chunks, SC_CHUNK).transpose(1, 0, 2, 3)
    out = dispatch(rows.reshape(-1, SUBLANES, LANES), idx.reshape(SC_WORKERS * idx_rows, SC_CHUNK))
    return out.reshape(n_out * SUBLANES, LANES)


def _combine_kernel(first_tile, x1_ref, rt_ref, mod_ref, g0_ref, g1_ref, g2_ref, g3_ref, o_ref):
    t = pl.program_id(0)
    per_b = LT // TQ - first_tile
    b = t // per_b
    is_ctx = (t % per_b + first_tile) == 0
    route = rt_ref[0].T
    y = jnp.zeros((TQ, D), F32)
    for k, g_ref in enumerate((g0_ref, g1_ref, g2_ref, g3_ref)):
        y = y + route[:, TOP_K + k:TOP_K + k + 1] * _rows_from_tiles(g_ref, TQ)
    gate = jnp.where(is_ctx, mod_ref[NB:NB + 1, 5 * D:6 * D], mod_ref[pl.ds(b, 1), 5 * D:6 * D])
    o_ref[0] = x1_ref[0] + gate * y


def _combine(l, x1, route, mod, gathered, latent_only):
    tiles_b = LT // TQ
    first_tile = L_CTX // TQ if latent_only else 0
    per_b = tiles_b - first_tile
    tile_of = lambda t: (t // per_b, t % per_b + first_tile)
    tok = lambda w: pl.BlockSpec((1, TQ, w), lambda t: tile_of(t) + (0,))
    slot = lambda k: pl.BlockSpec(
        (TQ * SUBLANES, LANES), lambda t: (k * NB * tiles_b + tile_of(t)[0] * tiles_b + tile_of(t)[1], 0))
    return pl.pallas_call(
        functools.partial(_combine_kernel, first_tile),
        out_shape=jax.ShapeDtypeStruct((NB, per_b * TQ, D), F32),
        grid=(NB * per_b,),
        in_specs=[tok(D),
                  pl.BlockSpec((1, 4 * TOP_K, TQ),
                               lambda t: (tile_of(t)[0] * (LT // TM_TOK) + tile_of(t)[1] // (TM_TOK // TQ), 0,
                                          tile_of(t)[1] % (TM_TOK // TQ))),
                  _layer_spec(l, (16, N_MOD * D)),
                  slot(0), slot(1), slot(2), slot(3)],
        out_specs=pl.BlockSpec((1, TQ, D), lambda t: (t // per_b, t % per_b, 0)),
        compiler_params=pltpu.CompilerParams(
            dimension_semantics=("arbitrary",), vmem_limit_bytes=VMEM_LIMIT),
        name="moe_combine",
    )(x1, route, mod, gathered, gathered, gathered, gathered)


def _rope_tables():
    pos = jnp.arange(L_LAT)
    row = (pos // GRID_W).astype(F32)
    col = (pos % GRID_W).astype(F32)
    inv = ROPE_BASE ** (-jnp.arange(0, 32, 2, dtype=F32) / 32)
    ang_r = row[:, None] * inv
    ang_c = col[:, None] * inv
    ang = jnp.concatenate([ang_r, ang_r, ang_c, ang_c], axis=-1)
    cos = jnp.concatenate([jnp.ones((L_CTX, HEAD_DIM), F32), jnp.cos(ang)], axis=0)
    sin = jnp.concatenate([jnp.zeros((L_CTX, HEAD_DIM), F32), jnp.sin(ang)], axis=0)
    sign = jnp.where((jnp.arange(HEAD_DIM) % 32) < 16, -1.0, 1.0).astype(F32)
    return jnp.tile(cos, (1, 2)), jnp.tile(sin * sign, (1, 2))


def _s5_tables(a_re, a_im, log_dt, b_re, b_im, c_re, c_im):
    dt = jnp.exp(log_dt)[..., None]
    mag = jnp.exp(a_re * dt)
    ar = mag * jnp.cos(a_im * dt)
    ai = mag * jnp.sin(a_im * dt)
    den = a_re * a_re + a_im * a_im
    qr = ((ar - 1) * a_re + ai * a_im) / den
    qi = (ai * a_re - (ar - 1) * a_im) / den
    bbr = qr[..., None] * b_re - qi[..., None] * b_im
    bbi = qr[..., None] * b_im + qi[..., None] * b_re
    eye = jnp.eye(SSM_GROUPS, dtype=F32)
    to_in = lambda m: jnp.einsum('ldgnp,gh->ldgphn', m, eye).reshape(DEPTH, 2, SSM_W, NS)
    bm = jnp.concatenate([to_in(bbr), to_in(bbi)], axis=-1).astype(BF)
    to_out = lambda m: jnp.einsum('ldgpn,gh->ldgnhp', m, eye).reshape(DEPTH, 2, NS, SSM_W)
    cm = jnp.concatenate([to_out(c_re), -to_out(c_im)], axis=2).astype(BF)
    a_tab = jnp.stack([ar.reshape(DEPTH, 2, NS), ai.reshape(DEPTH, 2, NS)], axis=2)
    return a_tab, bm, cm


def kernel(x, c, ctx, c_ctx, w_mod, b_mod, norm1_w, norm2_w, w_in, q_norm_w, k_norm_w, attn_sink,
           ssm_a_re, ssm_a_im, ssm_log_dt, ssm_b_re, ssm_b_im, ssm_c_re, ssm_c_im, ssm_d, glu_w, glu_b,
           pool_w, pool_scale, out_norm_w, w_out, router_w, router_b, exp_w_gu, exp_b_gu, exp_w_down,
           exp_b_down):
    xc = jnp.concatenate([ctx, x], axis=1)
    cc = jnp.concatenate([c, c_ctx[None, :], jnp.zeros((16 - NB - 1, D), F32)], axis=0)
    mod = _adaln(cc, w_mod, b_mod)
    cos2, sin2 = _rope_tables()
    tri = (jnp.arange(TM_TOK)[:, None] < jnp.arange(TM_TOK)[None, :]).astype(BF)

    row = lambda a: a.reshape(DEPTH, 1, -1)
    dup = lambda m: jnp.concatenate([m[..., :64], m[..., :64], m[..., 64:], m[..., 64:]], axis=-1)
    w_ext = jnp.concatenate([w_in[..., :512], dup(w_in[..., 512:640]), dup(w_in[..., 640:768]),
                             w_in[..., 768:]], axis=-1).astype(BF)
    qw2 = row(jnp.tile(q_norm_w, (1, 2)))
    kw2 = row(jnp.tile(k_norm_w, (1, 2)))
    a_tab, bm, cm = _s5_tables(ssm_a_re, ssm_a_im, ssm_log_dt, ssm_b_re, ssm_b_im, ssm_c_re, ssm_c_im)
    pw_bd = jnp.einsum('lgcd,gh->lgchd', pool_w, jnp.eye(4, dtype=F32)).reshape(DEPTH, POOL_W, POOL_W).astype(BF)
    rw = jnp.concatenate([router_w, jnp.zeros((DEPTH, D, LANES - N_EXP), F32)], axis=-1)
    rb = row(jnp.concatenate([router_b, jnp.full((DEPTH, LANES - N_EXP), -1e30, F32)], axis=-1))
    glu_w_bf = glu_w.astype(BF)
    w_out_bf = w_out.astype(BF)
    n1, n2, onw = row(norm1_w), row(norm2_w), row(out_norm_w)
    ssm_d3, glu_b3, pool_sc3 = row(ssm_d), row(glu_b), row(pool_scale)
    sink = attn_sink.reshape(-1)

    tok = jnp.arange(N_TOK, dtype=jnp.int32)
    is_ctx_tok = (tok % LT) < L_CTX

    for l in range(DEPTH):
        last = l == DEPTH - 1
        q, kd, vd, u, p = _inproj(l, xc, mod, n1, w_ext, qw2, kw2, cos2, sin2)
        attn = _attention(l, sink, q, kd, vd, onw)
        yf, yb = _s5_scan(l, u, a_tab, bm, cm)
        pool = _pool(l, p, pw_bd, pool_sc3)
        x1, h2, route, cnt = _mix(l, not last, xc, attn, yf, yb, u, pool, mod, ssm_d3, glu_w_bf, glu_b3, onw,
                                  w_out_bf, n2, rw, rb, tri)

        by_k = lambda lo: route[:, lo:lo + TOP_K, :].transpose(1, 0, 2).reshape(TOP_K, N_TOK)
        top_i = by_k(0).astype(jnp.int32)
        rank = by_k(2 * TOP_K).astype(jnp.int32)
        counts = cnt[:, 0].astype(jnp.int32)
        padded = (counts + TM_MOE - 1) // TM_MOE * TM_MOE
        pend = jnp.cumsum(padded)
        pstart = pend - padded
        experts = jnp.arange(N_EXP, dtype=jnp.int32)
        dest_t = jnp.sum(jnp.where(top_i[..., None] == experts, pstart, 0), axis=-1) + rank
        n_used = (pend[-1] // TM_MOE).astype(jnp.int32)[None]
        tile_lo = jnp.arange(NT_MOE, dtype=jnp.int32) * TM_MOE
        tile_e = jnp.minimum(jnp.sum(pend[None, :] <= tile_lo[:, None], axis=1), N_EXP - 1).astype(jnp.int32)
        change = jnp.concatenate([jnp.ones((1,), jnp.int32), (tile_e[1:] != tile_e[:-1]).astype(jnp.int32)])
        slot = (jnp.cumsum(change) - 1) % 2
        next_first = jnp.sum(jnp.where(tile_e[:, None] == experts, pend // TM_MOE, 0), axis=1)
        tiles = jnp.arange(NT_MOE, dtype=jnp.int32)
        next_e = jnp.sum(jnp.where(next_first[:, None] == tiles, tile_e, 0), axis=1)
        nxt = jnp.where(next_first < n_used[0], next_e, -1)
        if last:
            xs = _row_dispatch(h2, jnp.where(is_ctx_tok, R_MOE + tok % TM_MOE, dest_t), R_MOE + TM_MOE)
            dest_t = jnp.where(is_ctx_tok, tok, dest_t)
        else:
            xs = _row_dispatch(h2, dest_t, R_MOE)

        yb_rows = _moe(l, tile_e, n_used, slot.astype(jnp.int32), nxt.astype(jnp.int32), xs,
                       exp_w_gu, exp_b_gu, exp_w_down, exp_b_down)
        gathered = _row_gather(yb_rows, dest_t.reshape(-1))
        xc = _combine(l, x1, route, mod, gathered, latent_only=last)
    return xc
```

```python
import functools
import math

import jax
import jax.numpy as jnp
from jax import lax
from jax.experimental import pallas as pl
from jax.experimental.pallas import tpu as pltpu
from jax.experimental.pallas import tpu_sc as plsc

D = 1024
NB = 8
L_LAT = 2048
L_CTX = 256
LT = L_CTX + L_LAT
DEPTH = 4
N_MOD = 6
EPS = 1e-6
N_HEADS = 8
HEAD_DIM = 64
ATTN_W = 512
WINDOW = 128
ATTN_SCALE = HEAD_DIM ** -0.5
LOG2E = math.log2(math.e)
ROPE_BASE = 10000.0
GRID_W = 64
SSM_W = 256
SSM_GROUP = 16
SSM_GROUPS = 16
SSM_STATE = 64
NS = SSM_GROUPS * SSM_STATE
POOL_W = 256
POOL_GROUP = 64
N_EXP = 32
TOP_K = 4
D_FF = 1024
SWIGLU_LIMIT = 7.0
SWIGLU_ALPHA = 1.702

LANES = 128
SUBLANES = 8
VMEM_LIMIT = 56 * 1024 * 1024

TM_TOK = 768
TQ = 256
KSPAN = TQ + 2 * WINDOW
T_SCAN = 128
N_CHUNK = LT // T_SCAN
N_CHUNK_CTX = L_CTX // T_SCAN
TM_MOE = 512
N_TOK = NB * LT
N_ASSIGN = N_TOK * TOP_K
NT_MOE = N_ASSIGN // TM_MOE + N_EXP
R_MOE = NT_MOE * TM_MOE
FF_CHUNK = 512
OUT_BLOCK = 256

SC_CORES = 2
SC_SUBCORES = 16
SC_WORKERS = SC_CORES * SC_SUBCORES
SC_CHUNK = 32
PACK_ROWS = D // 2 // LANES

C_Q = 0
C_K = 512
C_V = 768
C_U = 1024
C_P = 1280
NW_IN = 1536

BF = jnp.bfloat16
F32 = jnp.float32


def _split_bf16(a):
    hi = a.astype(BF)
    lo = (a - hi.astype(F32)).astype(BF)
    return hi, lo


def _dot(a, b):
    return jnp.dot(a, b, preferred_element_type=F32)


def _dot3(a, b):
    ah, al = _split_bf16(a)
    bh, bl = _split_bf16(b)
    return _dot(ah, bh) + (_dot(ah, bl) + _dot(al, bh))


def _mod_kernel(c_ref, w_ref, b_ref, o_ref):
    c = c_ref[...]
    s = c * jax.nn.sigmoid(c)
    o_ref[0] = _dot3(s, w_ref[0]) + b_ref[0]


def _adaln(cc, w_mod, b_mod):
    tn = 1536
    return pl.pallas_call(
        _mod_kernel,
        out_shape=jax.ShapeDtypeStruct((DEPTH, 16, N_MOD * D), F32),
        grid=(DEPTH, N_MOD * D // tn),
        in_specs=[
            pl.BlockSpec((16, D), lambda l, j: (0, 0)),
            pl.BlockSpec((1, D, tn), lambda l, j: (l, 0, j)),
            pl.BlockSpec((1, 1, tn), lambda l, j: (l, 0, j)),
        ],
        out_specs=pl.BlockSpec((1, 16, tn), lambda l, j: (l, 0, j)),
        compiler_params=pltpu.CompilerParams(
            dimension_semantics=("arbitrary", "arbitrary"), vmem_limit_bytes=VMEM_LIMIT),
        name="adaln_mod",
    )(cc, w_mod, b_mod.reshape(DEPTH, 1, N_MOD * D))


def _layer_spec(l, shape, col_block=0):
    shape = tuple(shape)
    return pl.BlockSpec((None,) + shape, lambda *_: (l,) + (0,) * (len(shape) - 1) + (col_block,))


def _with_mod(fn, mod_ref, b, i, idxs, *arrays):
    lat = [mod_ref[pl.ds(b, 1), k * D:(k + 1) * D] for k in idxs]
    top = [jnp.where(i == 0, mod_ref[NB:NB + 1, k * D:(k + 1) * D], v) for k, v in zip(idxs, lat)]
    return jnp.concatenate([fn(*[a[:L_CTX] for a in arrays], *top),
                            fn(*[a[L_CTX:] for a in arrays], *lat)], axis=0)


def _rmsnorm(x, w):
    ms = jnp.mean(x * x, axis=-1, keepdims=True)
    return x * lax.rsqrt(ms + EPS) * w


def _headnorm_pair(t, w2, lane):
    sq = t * t
    first = lane < HEAD_DIM
    a = jnp.sum(jnp.where(first, sq, 0.0), axis=-1, keepdims=True)
    b = jnp.sum(jnp.where(first, 0.0, sq), axis=-1, keepdims=True)
    ms = jnp.where(first, a, b) * (1.0 / HEAD_DIM)
    return t * lax.rsqrt(ms + EPS) * w2


def _rope_pair(t, cos, sin_signed, lane):
    lower = (lane % 32) < 16
    partner = jnp.where(lower, pltpu.roll(t, LANES - 16, 1), pltpu.roll(t, 16, 1))
    return t * cos + partner * sin_signed


def _inproj_kernel(x_ref, mod_ref, n1_ref, w_ref, qw_ref, kw_ref, cos_ref, sin_ref,
                   q_ref, k_ref, v_ref, u_ref, p_ref):
    b = pl.program_id(0)
    i = pl.program_id(1)
    h = _rmsnorm(x_ref[0], n1_ref[...])
    h = _with_mod(lambda hr, shift, scale: (hr * (1.0 + scale) + shift).astype(BF), mod_ref, b, i, (0, 1), h)
    px = _dot(h, w_ref[...])
    lane = lax.broadcasted_iota(jnp.int32, (TM_TOK, LANES), 1)
    cos = cos_ref[...]
    sin = sin_ref[...]
    for j in range(ATTN_W // LANES):
        t = px[:, C_Q + j * LANES:C_Q + (j + 1) * LANES]
        t = _rope_pair(_headnorm_pair(t, qw_ref[...], lane), cos, sin, lane) * (ATTN_SCALE * LOG2E)
        q_ref[0, :, j * LANES:(j + 1) * LANES] = t.astype(BF)
    for g in range(2):
        t = px[:, C_K + g * LANES:C_K + (g + 1) * LANES]
        t = _rope_pair(_headnorm_pair(t, kw_ref[...], lane), cos, sin, lane)
        k_ref[0, :, g * LANES:(g + 1) * LANES] = t.astype(BF)
        t = px[:, C_V + g * LANES:C_V + (g + 1) * LANES]
        v_ref[0, :, g * LANES:(g + 1) * LANES] = jnp.where(lane < HEAD_DIM, t, 1.0).astype(BF)
    u_ref[0] = px[:, C_U:C_P]
    p_ref[0] = px[:, C_P:NW_IN]


def _inproj(l, xc, mod, n1, w_ext, qw2, kw2, cos2, sin2):
    tok = lambda w: pl.BlockSpec((1, TM_TOK, w), lambda b, i: (b, i, 0))
    full = lambda s: _layer_spec(l, s)
    return pl.pallas_call(
        _inproj_kernel,
        out_shape=(
            jax.ShapeDtypeStruct((NB, LT, ATTN_W), BF),
            jax.ShapeDtypeStruct((NB, LT, 256), BF),
            jax.ShapeDtypeStruct((NB, LT, 256), BF),
            jax.ShapeDtypeStruct((NB, LT, SSM_W), F32),
            jax.ShapeDtypeStruct((NB, LT, POOL_W), F32),
        ),
        grid=(NB, LT // TM_TOK),
        in_specs=[
            tok(D), full((16, N_MOD * D)), full((1, D)), full((D, NW_IN)),
            full((1, LANES)), full((1, LANES)),
            pl.BlockSpec((TM_TOK, LANES), lambda b, i: (i, 0)),
            pl.BlockSpec((TM_TOK, LANES), lambda b, i: (i, 0)),
        ],
        out_specs=(tok(ATTN_W), tok(256), tok(256), tok(SSM_W), tok(POOL_W)),
        compiler_params=pltpu.CompilerParams(
            dimension_semantics=("arbitrary", "arbitrary"), vmem_limit_bytes=VMEM_LIMIT),
        name="inproj",
    )(xc, mod, n1, w_ext, qw2, kw2, cos2, sin2)


def _attn_kernel(l, sink_ref, q_ref, k_ref, v_ref, onw_ref, o_ref, bias_ref):
    j = pl.program_id(1)
    start = pl.multiple_of(jnp.clip(j * TQ - WINDOW, LANES, LT - KSPAN), LANES)
    lane = lax.broadcasted_iota(jnp.int32, (TQ, LANES), 1)
    first = lane < HEAD_DIM
    row2 = lax.broadcasted_iota(jnp.int32, (2 * TQ, 1), 0)
    qpos = j * TQ + jnp.where(row2 < TQ, row2, row2 - TQ)
    kpos = start + lax.broadcasted_iota(jnp.int32, (1, KSPAN), 1)
    valid = (jnp.abs(qpos - kpos) <= WINDOW) & (kpos >= L_CTX) & (j >= 1)
    bias_ref[...] = jnp.where(valid, 0.0, -1e30)
    n_pairs = N_HEADS // 2

    def kv(ref, hp, rows):
        g = hp // 2
        return ref[0, rows, g * LANES:(g + 1) * LANES]

    def scores(hp):
        q2 = q_ref[0, :, hp * LANES:(hp + 1) * LANES]
        zero = jnp.zeros_like(q2)
        qs = jnp.concatenate([jnp.where(first, q2, zero), jnp.where(first, zero, q2)], axis=0)
        dn = (((1,), (1,)), ((), ()))
        s_ctx = lax.dot_general(qs, kv(k_ref, hp, slice(0, L_CTX)), dn, preferred_element_type=F32)
        s_loc = lax.dot_general(qs, kv(k_ref, hp, pl.ds(start, KSPAN)), dn, preferred_element_type=F32)
        return s_loc, s_ctx

    nxt = scores(0)
    outs = []
    for hp in range(n_pairs):
        s_loc, s_ctx = nxt
        if hp + 1 < n_pairs:
            nxt = scores(hp + 1)
        s_loc = s_loc + bias_ref[...]
        sink = jnp.where(row2 < TQ, sink_ref[l * N_HEADS + 2 * hp],
                         sink_ref[l * N_HEADS + 2 * hp + 1]) * LOG2E
        m = jnp.maximum(jnp.maximum(jnp.max(s_loc, axis=-1, keepdims=True),
                                    jnp.max(s_ctx, axis=-1, keepdims=True)), sink)
        e_loc = jnp.exp2(s_loc - m)
        e_ctx = jnp.exp2(s_ctx - m)
        o = (_dot(e_loc.astype(BF), kv(v_ref, hp, pl.ds(start, KSPAN)))
             + _dot(e_ctx.astype(BF), kv(v_ref, hp, slice(0, L_CTX))))
        den = pltpu.roll(o + jnp.exp2(sink - m), HEAD_DIM, 1)
        o = o / den
        outs.append(jnp.where(first, o[0:TQ], pltpu.roll(o[TQ:2 * TQ], HEAD_DIM, 1)))
    o_ref[0] = _rmsnorm(jnp.concatenate(outs, axis=1), onw_ref[...]).astype(BF)


def _attention(l, sink, q, kd, vd, onw):
    return pl.pallas_call(
        functools.partial(_attn_kernel, l),
        out_shape=jax.ShapeDtypeStruct((NB, LT, ATTN_W), BF),
        grid_spec=pltpu.PrefetchScalarGridSpec(
            num_scalar_prefetch=1,
            grid=(NB, LT // TQ),
            in_specs=[
                pl.BlockSpec((1, TQ, ATTN_W), lambda b, j, s: (b, j, 0)),
                pl.BlockSpec((1, LT, 256), lambda b, j, s: (b, 0, 0)),
                pl.BlockSpec((1, LT, 256), lambda b, j, s: (b, 0, 0)),
                _layer_spec(l, (1, ATTN_W)),
            ],
            out_specs=pl.BlockSpec((1, TQ, ATTN_W), lambda b, j, s: (b, j, 0)),
            scratch_shapes=[pltpu.VMEM((2 * TQ, KSPAN), F32)],
        ),
        compiler_params=pltpu.CompilerParams(
            dimension_semantics=("arbitrary", "arbitrary"), vmem_limit_bytes=VMEM_LIMIT),
        name="window_attn",
    )(sink, q, kd, vd, onw)


def _bwd_chunk(i):
    return jnp.where(i < N_CHUNK_CTX, N_CHUNK_CTX - 1 - i, N_CHUNK - 1 - (i - N_CHUNK_CTX))


def _scan_kernel(uf_ref, ub_ref, a_ref, bm_ref, cm_ref, yf_ref, yb_ref,
                 xf_ref, xb_ref, st_ref, *il_refs):
    i = pl.program_id(0)
    halves = SSM_W // LANES

    @pl.when(i == 0)
    def _():
        st_ref[...] = jnp.zeros_like(st_ref)

    def project(d, u_ref, xs_ref):
        il = il_refs[d * halves:(d + 1) * halves]
        for h in range(halves):
            for bb in range(NB):
                il[h][pl.ds(bb, T_SCAN, stride=NB), :] = u_ref[bb, :, h * LANES:(h + 1) * LANES]
        ui = jnp.concatenate([r[...] for r in il], axis=1).astype(BF)
        xs_ref[...] = _dot(ui, bm_ref[d])

    def scan(d, xs_ref, reverse):
        ar = jnp.broadcast_to(a_ref[d, 0:1, :], (NB, NS))
        ai = jnp.broadcast_to(a_ref[d, 1:2, :], (NB, NS))
        sr = st_ref[d, 0]
        si = st_ref[d, 1]
        for t in (range(T_SCAN - 1, -1, -1) if reverse else range(T_SCAN)):
            r = pl.ds(t * NB, NB)
            nr = ar * sr - ai * si + xs_ref[r, 0:NS]
            ni = ar * si + ai * sr + xs_ref[r, NS:2 * NS]
            sr, si = nr, ni
            xs_ref[r, 0:NS] = sr
            xs_ref[r, NS:2 * NS] = si
        st_ref[d, 0] = sr
        st_ref[d, 1] = si

    def readout(d, xs_ref, y_ref):
        y = _dot(xs_ref[...].astype(BF), cm_ref[d])
        il = il_refs[d * halves:(d + 1) * halves]
        for h in range(halves):
            il[h][...] = y[:, h * LANES:(h + 1) * LANES]
            for bb in range(NB):
                y_ref[bb, :, h * LANES:(h + 1) * LANES] = il[h][pl.ds(bb, T_SCAN, stride=NB), :]

    project(0, uf_ref, xf_ref)
    project(1, ub_ref, xb_ref)
    scan(0, xf_ref, False)
    readout(0, xf_ref, yf_ref)
    scan(1, xb_ref, True)
    readout(1, xb_ref, yb_ref)


def _s5_scan(l, u, a_tab, bm, cm):
    rows = NB * T_SCAN
    layer = lambda s: _layer_spec(l, s)
    chunk_f = pl.BlockSpec((NB, T_SCAN, SSM_W), lambda i: (0, i, 0))
    chunk_b = pl.BlockSpec((NB, T_SCAN, SSM_W), lambda i: (0, _bwd_chunk(i), 0))
    return pl.pallas_call(
        _scan_kernel,
        out_shape=(jax.ShapeDtypeStruct((NB, LT, SSM_W), F32),
                   jax.ShapeDtypeStruct((NB, LT, SSM_W), F32)),
        grid=(N_CHUNK,),
        in_specs=[chunk_f, chunk_b, layer((2, 2, NS)), layer((2, SSM_W, 2 * NS)), layer((2, 2 * NS, SSM_W))],
        out_specs=(chunk_f, chunk_b),
        scratch_shapes=[pltpu.VMEM((rows, 2 * NS), F32), pltpu.VMEM((rows, 2 * NS), F32),
                        pltpu.VMEM((2, 2, NB, NS), F32),
                        ] + [pltpu.VMEM((rows, LANES), F32)] * (2 * (SSM_W // LANES)),
        compiler_params=pltpu.CompilerParams(
            dimension_semantics=("arbitrary",), vmem_limit_bytes=VMEM_LIMIT),
        name="s5_scan",
    )(u, u, a_tab, bm, cm)


def _pool_segment(ps, length):
    n = length + 2 * SUBLANES
    z = jnp.zeros((SUBLANES, POOL_W), F32)
    pe = jnp.concatenate([z, ps, z], axis=0)
    a1 = pe + pltpu.roll(pe, 1, 0)
    a2 = a1 + pltpu.roll(a1, 2, 0)
    a3 = a2 + pltpu.roll(a2, 4, 0)
    a4 = a3 + pltpu.roll(a3, 8, 0)
    lane = lax.broadcasted_iota(jnp.int32, (1, POOL_W), 1)
    half = jnp.where(lane < 64, 1, jnp.where(lane < 128, 2, jnp.where(lane < 192, 4, 8)))
    s = jnp.where(lane < 64, a1,
                  jnp.where(lane < 128, pltpu.roll(a2, n - 1, 0),
                            jnp.where(lane < 192, pltpu.roll(a3, n - 3, 0), pltpu.roll(a4, n - 7, 0))))
    s = s[SUBLANES:SUBLANES + length]
    t = lax.broadcasted_iota(jnp.int32, (length, 1), 0)
    cnt = jnp.minimum(t + half, length) - jnp.maximum(t - half, 0)
    return s / cnt.astype(F32) - ps


def _pool_kernel(p_ref, w_ref, sc_ref, o_ref):
    for lo, length in ((0, L_CTX), (L_CTX, L_LAT)):
        dlt = _pool_segment(p_ref[0, lo:lo + length, :], length)
        y = _dot(dlt.astype(BF), w_ref[...]) * sc_ref[...]
        o_ref[0, lo:lo + length, :] = y.astype(BF)


def _pool(l, p, w_bd, scale):
    return pl.pallas_call(
        _pool_kernel,
        out_shape=jax.ShapeDtypeStruct((NB, LT, POOL_W), BF),
        grid=(NB,),
        in_specs=[pl.BlockSpec((1, LT, POOL_W), lambda b: (b, 0, 0)),
                  _layer_spec(l, (POOL_W, POOL_W)),
                  _layer_spec(l, (1, POOL_W))],
        out_specs=pl.BlockSpec((1, LT, POOL_W), lambda b: (b, 0, 0)),
        compiler_params=pltpu.CompilerParams(
            dimension_semantics=("arbitrary",), vmem_limit_bytes=VMEM_LIMIT),
        name="pool",
    )(p, w_bd, scale)


def _mix_kernel(route_ctx, x_ref, at_ref, yf_ref, yb_ref, u_ref, pl_ref, mod_ref, d_ref, gw_ref, gb_ref, onw_ref,
                wo_ref, n2_ref, rw_ref, rb_ref, tri_ref,
                x1_ref, h2_ref, rt_ref, cnt_ref, run_ref):
    b = pl.program_id(0)
    i = pl.program_id(1)

    @pl.when((b == 0) & (i == 0))
    def _():
        run_ref[...] = jnp.zeros_like(run_ref)

    y = yf_ref[0] + yb_ref[0] + d_ref[...] * u_ref[0]
    g = jax.nn.gelu(y, approximate=True)
    s = g * jax.nn.sigmoid(_dot(g.astype(BF), gw_ref[...]) + gb_ref[...])
    s = _rmsnorm(s, onw_ref[...]).astype(BF)
    mix = (_dot(at_ref[0], wo_ref[0:ATTN_W, :])
           + _dot(s, wo_ref[ATTN_W:ATTN_W + SSM_W, :])
           + _dot(pl_ref[0], wo_ref[ATTN_W + SSM_W:D, :]))
    x1 = _with_mod(lambda xr, mr, gate: xr + gate * mr, mod_ref, b, i, (2,), x_ref[0], mix)
    x1_ref[0] = x1
    h2 = _rmsnorm(x1, n2_ref[...])
    h2 = _with_mod(lambda hr, shift, scale: hr * (1.0 + scale) + shift, mod_ref, b, i, (3, 4), h2)
    for j in range(D // LANES):
        h2_ref[pl.ds(j, TM_TOK, stride=SUBLANES), :] = h2[:, j * LANES:(j + 1) * LANES]
    logits = (_dot3(h2, rw_ref[...]) + rb_ref[...]).T[0:N_EXP]
    eidx = lax.broadcasted_iota(jnp.int32, (N_EXP, TM_TOK), 0).astype(F32)
    vals, idxs, hits = [], [], []
    cur = logits
    for _ in range(TOP_K):
        m = jnp.max(cur, axis=0, keepdims=True)
        idx = jnp.min(jnp.where(cur == m, eidx, float(N_EXP)), axis=0, keepdims=True)
        hit = eidx == idx
        vals.append(m)
        idxs.append(idx)
        hits.append(hit)
        cur = jnp.where(hit, -jnp.inf, cur)
    ex = [jnp.exp(v - vals[0]) for v in vals]
    den = ex[0] + ex[1] + ex[2] + ex[3]
    onehot = jnp.where(hits[0] | hits[1] | hits[2] | hits[3], 1.0, 0.0)
    if not route_ctx:
        tok = lax.broadcasted_iota(jnp.int32, (1, TM_TOK), 1)
        onehot = jnp.where((tok >= L_CTX) | (i > 0), onehot, 0.0)
    run = run_ref[:, 0:1]
    before = _dot(onehot.astype(BF), tri_ref[...]) + run
    row = lax.broadcasted_iota(jnp.int32, (4 * TOP_K, TM_TOK), 0)
    route = jnp.zeros((4 * TOP_K, TM_TOK), F32)
    for k in range(TOP_K):
        rank = jnp.sum(jnp.where(hits[k], before, 0.0), axis=0, keepdims=True)
        route = jnp.where(row == k, idxs[k], route)
        route = jnp.where(row == TOP_K + k, ex[k] / den, route)
        route = jnp.where(row == 2 * TOP_K + k, rank, route)
    rt_ref[0] = route
    run_ref[...] = jnp.broadcast_to(run + jnp.sum(onehot, axis=1, keepdims=True), (N_EXP, LANES))
    cnt_ref[...] = run_ref[...]


def _mix(l, route_ctx, xc, attn, yf, yb, u, pool, mod, ssm_d, glu_w, glu_b, onw, w_out, n2, rw, rb, tri):
    tok = lambda w: pl.BlockSpec((1, TM_TOK, w), lambda b, i: (b, i, 0))
    full = lambda s: pl.BlockSpec(s, lambda b, i: (0,) * len(s))
    layer = lambda s, col=0: _layer_spec(l, s, col)
    return pl.pallas_call(
        functools.partial(_mix_kernel, route_ctx),
        out_shape=(jax.ShapeDtypeStruct((NB, LT, D), F32),
                   jax.ShapeDtypeStruct((N_TOK * SUBLANES, LANES), F32),
                   jax.ShapeDtypeStruct((N_TOK // TM_TOK, 4 * TOP_K, TM_TOK), F32),
                   jax.ShapeDtypeStruct((N_EXP, LANES), F32)),
        grid=(NB, LT // TM_TOK),
        in_specs=[tok(D), tok(ATTN_W), tok(SSM_W), tok(SSM_W), tok(SSM_W), tok(POOL_W),
                  layer((16, N_MOD * D)), layer((1, SSM_W)), layer((SSM_W, SSM_W)), layer((1, SSM_W)),
                  layer((1, SSM_W), ATTN_W // SSM_W), layer((D, D)), layer((1, D)), layer((D, LANES)),
                  layer((1, LANES)), full((TM_TOK, TM_TOK))],
        out_specs=(tok(D),
                   pl.BlockSpec((TM_TOK * SUBLANES, LANES), lambda b, i: (b * (LT // TM_TOK) + i, 0)),
                   pl.BlockSpec((1, 4 * TOP_K, TM_TOK), lambda b, i: (b * (LT // TM_TOK) + i, 0, 0)),
                   full((N_EXP, LANES))),
        scratch_shapes=[pltpu.VMEM((N_EXP, LANES), F32)],
        compiler_params=pltpu.CompilerParams(
            dimension_semantics=("arbitrary", "arbitrary"), vmem_limit_bytes=VMEM_LIMIT),
        name="mix_router",
    )(xc, attn, yf, yb, u, pool, mod, ssm_d, glu_w, glu_b, onw, w_out, n2, rw, rb, tri)


def _rows_from_tiles(ref, rows):
    return jnp.concatenate(
        [ref[pl.ds(j, rows, stride=SUBLANES), :] for j in range(D // LANES)], axis=1)


def _rows_to_tiles(ref, val, rows):
    for j in range(D // LANES):
        ref[pl.ds(j, rows, stride=SUBLANES), :] = val[:, j * LANES:(j + 1) * LANES]


def _moe_kernel(l, te_ref, nu_ref, slot_ref, nxt_ref, x_ref, wgu_hbm, bgu_ref, wd_hbm, bd_ref, o_ref,
                wgu_f, wd_f, wgu_s, wd_s, act_s, sem):
    i = pl.program_id(0)

    @pl.when(i >= nu_ref[0])
    def _():
        o_ref[...] = jnp.zeros_like(o_ref)

    def fetch(e, slot):
        return (pltpu.make_async_copy(wgu_hbm.at[l, e], wgu_f.at[slot], sem.at[0, slot]),
                pltpu.make_async_copy(wd_hbm.at[l, e], wd_f.at[slot], sem.at[1, slot]))

    @pl.when(i < nu_ref[0])
    def _():
        e = te_ref[i]
        prev = te_ref[jnp.maximum(i - 1, 0)]
        slot = slot_ref[i]

        @pl.when(i == 0)
        def _():
            for cp in fetch(e, slot):
                cp.start()

        @pl.when((i == 0) | (e != prev))
        def _():
            for cp in fetch(e, slot):
                cp.wait()
            nxt = nxt_ref[i]

            @pl.when(nxt >= 0)
            def _():
                for cp in fetch(nxt, 1 - slot):
                    cp.start()

            def cast_gu(r, c):
                rs = pl.ds(pl.multiple_of(r * 128, 128), 128)
                wgu_s[rs, :] = wgu_f[slot, rs, :].astype(BF)
                return c

            def cast_d(r, c):
                rs = pl.ds(pl.multiple_of(r * 128, 128), 128)
                wd_s[rs, :] = wd_f[slot, rs, :].astype(BF)
                return c

            lax.fori_loop(0, D // 128, cast_gu, 0)
            lax.fori_loop(0, D_FF // 128, cast_d, 0)

        x = _rows_from_tiles(x_ref, TM_MOE).astype(BF)
        for c in range(D_FF // FF_CHUNK):
            lo = c * FF_CHUNK
            gate = _dot(x, wgu_s[:, lo:lo + FF_CHUNK]) + bgu_ref[:, lo:lo + FF_CHUNK]
            up = (_dot(x, wgu_s[:, D_FF + lo:D_FF + lo + FF_CHUNK])
                  + bgu_ref[:, D_FF + lo:D_FF + lo + FF_CHUNK])
            gate = jnp.minimum(gate, SWIGLU_LIMIT)
            up = jnp.clip(up, -SWIGLU_LIMIT, SWIGLU_LIMIT)
            act = (up + 1.0) * (gate * jax.nn.sigmoid(SWIGLU_ALPHA * gate))
            act_s[:, lo:lo + FF_CHUNK] = act.astype(BF)
        def bf16_bits(v):
            return lax.bitcast_convert_type(v.astype(BF).astype(F32), jnp.uint32)

        for n in range(D // 2 // OUT_BLOCK):
            lo = n * OUT_BLOCK
            y_hi = _dot(act_s[...], wd_s[:, lo:lo + OUT_BLOCK]) + bd_ref[:, lo:lo + OUT_BLOCK]
            y_lo = (_dot(act_s[...], wd_s[:, D // 2 + lo:D // 2 + lo + OUT_BLOCK])
                    + bd_ref[:, D // 2 + lo:D // 2 + lo + OUT_BLOCK])
            words = bf16_bits(y_hi) | (bf16_bits(y_lo) >> 16)
            for j in range(OUT_BLOCK // LANES):
                o_ref[pl.ds(lo // LANES + j, TM_MOE, stride=PACK_ROWS), :] = words[:, j * LANES:(j + 1) * LANES]


def _moe(l, tile_e, n_used, slot, nxt, xs, w_gu, b_gu, w_down, b_down):
    def tile(i, te, nu, *_):
        return (jnp.minimum(i, nu[0] - 1), 0)

    def bias(i, te, nu, *_):
        return (l, te[jnp.minimum(i, nu[0] - 1)], 0, 0)

    return pl.pallas_call(
        functools.partial(_moe_kernel, l),
        out_shape=jax.ShapeDtypeStruct((R_MOE * PACK_ROWS, LANES), jnp.uint32),
        grid_spec=pltpu.PrefetchScalarGridSpec(
            num_scalar_prefetch=4,
            grid=(NT_MOE,),
            in_specs=[
                pl.BlockSpec((TM_MOE * SUBLANES, LANES), tile),
                pl.BlockSpec(memory_space=pl.ANY),
                pl.BlockSpec((None, None, 1, 2 * D_FF), bias),
                pl.BlockSpec(memory_space=pl.ANY),
                pl.BlockSpec((None, None, 1, D), bias),
            ],
            out_specs=pl.BlockSpec((TM_MOE * PACK_ROWS, LANES), lambda i, *_: (i, 0)),
            scratch_shapes=[pltpu.VMEM((2, D, 2 * D_FF), F32), pltpu.VMEM((2, D_FF, D), F32),
                            pltpu.VMEM((D, 2 * D_FF), BF), pltpu.VMEM((D_FF, D), BF),
                            pltpu.VMEM((TM_MOE, D_FF), BF), pltpu.SemaphoreType.DMA((2, 2))],
        ),
        compiler_params=pltpu.CompilerParams(
            dimension_semantics=("arbitrary",), vmem_limit_bytes=VMEM_LIMIT),
        name="moe_experts",
    )(tile_e, n_used, slot, nxt, xs, w_gu, b_gu.reshape(DEPTH, N_EXP, 1, 2 * D_FF), w_down,
      b_down.reshape(DEPTH, N_EXP, 1, D))


def _row_gather(table, idx):
    n_rows = idx.shape[0]
    per_worker = n_rows // SC_WORKERS
    n_chunks = per_worker // SC_CHUNK
    assert per_worker * SC_WORKERS == n_rows and n_chunks * SC_CHUNK == per_worker and n_chunks % 2 == 0
    mesh = plsc.VectorSubcoreMesh(core_axis_name="c", subcore_axis_name="s")
    row_tile = (SC_CHUNK,) + table.shape[1:]

    @functools.partial(
        pl.kernel, mesh=mesh,
        out_type=jax.ShapeDtypeStruct((n_rows,) + table.shape[1:], table.dtype),
        scratch_types=[pltpu.VMEM((n_chunks, SC_CHUNK), jnp.int32),
                       pltpu.VMEM(row_tile, table.dtype), pltpu.VMEM(row_tile, table.dtype),
                       pltpu.SemaphoreType.DMA, pltpu.SemaphoreType.DMA],
        name="sc_row_gather",
    )
    def gather(table_hbm, idx_hbm, out_hbm, idx_v, buf0, buf1, sem0, sem1):
        wid = lax.axis_index("s") * SC_CORES + lax.axis_index("c")
        pltpu.sync_copy(idx_hbm.at[pl.ds(wid * n_chunks, n_chunks)], idx_v)
        base = wid * per_worker

        def fetch(chunk, buf, sem):
            return pltpu.make_async_copy(table_hbm.at[idx_v.at[chunk]], buf, sem)

        def emit(chunk, buf):
            pltpu.sync_copy(buf, out_hbm.at[pl.ds(base + chunk * SC_CHUNK, SC_CHUNK)])

        fetch(0, buf0, sem0).start()

        @pl.loop(0, n_chunks, step=2)
        def _(c):
            fetch(c + 1, buf1, sem1).start()
            fetch(c, buf0, sem0).wait()
            emit(c, buf0)

            @pl.when(c + 2 < n_chunks)
            def _():
                fetch(c + 2, buf0, sem0).start()

            fetch(c + 1, buf1, sem1).wait()
            emit(c + 1, buf1)

    return gather(table, idx.reshape(n_rows // SC_CHUNK, SC_CHUNK))


def _row_dispatch(rows, dest_t, n_out):
    per_worker = N_TOK // SC_WORKERS
    n_chunks = per_worker // SC_CHUNK
    assert per_worker * SC_WORKERS == N_TOK and n_chunks * SC_CHUNK == per_worker and n_chunks % 2 == 0
    idx_rows = TOP_K * n_chunks
    assert idx_rows % SUBLANES == 0
    mesh = plsc.VectorSubcoreMesh(core_axis_name="c", subcore_axis_name="s")
    row_tile = (SC_CHUNK, SUBLANES, LANES)

    @functools.partial(
        pl.kernel, mesh=mesh,
        out_type=jax.ShapeDtypeStruct((n_out, SUBLANES, LANES), F32),
        scratch_types=[pltpu.VMEM((idx_rows, SC_CHUNK), jnp.int32),
                       pltpu.VMEM(row_tile, F32), pltpu.VMEM(row_tile, F32),
                       pltpu.SemaphoreType.DMA, pltpu.SemaphoreType.DMA],
        name="sc_row_dispatch",
    )
    def dispatch(rows_hbm, idx_hbm, out_hbm, idx_v, buf0, buf1, sem0, sem1):
        wid = lax.axis_index("s") * SC_CORES + lax.axis_index("c")
        pltpu.sync_copy(idx_hbm.at[pl.ds(wid * idx_rows, idx_rows)], idx_v)
        base = wid * per_worker

        def fetch(chunk, buf, sem):
            return pltpu.make_async_copy(rows_hbm.at[pl.ds(base + chunk * SC_CHUNK, SC_CHUNK)], buf, sem)

        def emit(chunk, buf):
            for k in range(TOP_K):
                pltpu.sync_copy(buf, out_hbm.at[idx_v.at[k * n_chunks + chunk]])

        fetch(0, buf0, sem0).start()

        @pl.loop(0, n_chunks, step=2)
        def _(c):
            fetch(c + 1, buf1, sem1).start()
            fetch(c, buf0, sem0).wait()
            emit(c, buf0)

            @pl.when(c + 2 < n_chunks)
            def _():
                fetch(c + 2, buf0, sem0).start()

            fetch(c + 1, buf1, sem1).wait()
            emit(c + 1, buf1)

    idx = dest_t.reshape(TOP_K, SC_WORKERS, n_chunks, SC_CHUNK).transpose(1, 0, 2, 3)
    out = dispatch(rows.reshape(-1, SUBLANES, LANES), idx.reshape(SC_WORKERS * idx_rows, SC_CHUNK))
    return out.reshape(n_out * SUBLANES, LANES)


def _combine_kernel(first_tile, x1_ref, rt_ref, mod_ref, g0_ref, g1_ref, g2_ref, g3_ref, o_ref):
    t = pl.program_id(0)
    per_b = LT // TQ - first_tile
    b = t // per_b
    is_ctx = (t % per_b + first_tile) == 0
    route = rt_ref[0].T
    y = jnp.zeros((TQ, D), F32)
    for k, g_ref in enumerate((g0_ref, g1_ref, g2_ref, g3_ref)):
        words = jnp.concatenate([g_ref[pl.ds(j, TQ, stride=PACK_ROWS), :] for j in range(PACK_ROWS)], axis=1)
        rows = jnp.concatenate(
            [lax.bitcast_convert_type(words & jnp.uint32(0xFFFF0000), F32),
             lax.bitcast_convert_type(words << 16, F32)], axis=1)
        y = y + route[:, TOP_K + k:TOP_K + k + 1] * rows
    gate = jnp.where(is_ctx, mod_ref[NB:NB + 1, 5 * D:6 * D], mod_ref[pl.ds(b, 1), 5 * D:6 * D])
    o_ref[0] = x1_ref[0] + gate * y


def _combine(l, x1, route, mod, gathered, latent_only):
    tiles_b = LT // TQ
    first_tile = L_CTX // TQ if latent_only else 0
    per_b = tiles_b - first_tile
    tile_of = lambda t: (t // per_b, t % per_b + first_tile)
    tok = lambda w: pl.BlockSpec((1, TQ, w), lambda t: tile_of(t) + (0,))
    slot = lambda k: pl.BlockSpec(
        (TQ * PACK_ROWS, LANES),
        lambda t: (k * NB * tiles_b + tile_of(t)[0] * tiles_b + tile_of(t)[1], 0))
    return pl.pallas_call(
        functools.partial(_combine_kernel, first_tile),
        out_shape=jax.ShapeDtypeStruct((NB, per_b * TQ, D), F32),
        grid=(NB * per_b,),
        in_specs=[tok(D),
                  pl.BlockSpec((1, 4 * TOP_K, TQ),
                               lambda t: (tile_of(t)[0] * (LT // TM_TOK) + tile_of(t)[1] // (TM_TOK // TQ), 0,
                                          tile_of(t)[1] % (TM_TOK // TQ))),
                  _layer_spec(l, (16, N_MOD * D)),
                  slot(0), slot(1), slot(2), slot(3)],
        out_specs=pl.BlockSpec((1, TQ, D), lambda t: (t // per_b, t % per_b, 0)),
        compiler_params=pltpu.CompilerParams(
            dimension_semantics=("arbitrary",), vmem_limit_bytes=VMEM_LIMIT),
        name="moe_combine",
    )(x1, route, mod, gathered, gathered, gathered, gathered)


def _rope_tables():
    pos = jnp.arange(L_LAT)
    row = (pos // GRID_W).astype(F32)
    col = (pos % GRID_W).astype(F32)
    inv = ROPE_BASE ** (-jnp.arange(0, 32, 2, dtype=F32) / 32)
    ang_r = row[:, None] * inv
    ang_c = col[:, None] * inv
    ang = jnp.concatenate([ang_r, ang_r, ang_c, ang_c], axis=-1)
    cos = jnp.concatenate([jnp.ones((L_CTX, HEAD_DIM), F32), jnp.cos(ang)], axis=0)
    sin = jnp.concatenate([jnp.zeros((L_CTX, HEAD_DIM), F32), jnp.sin(ang)], axis=0)
    sign = jnp.where((jnp.arange(HEAD_DIM) % 32) < 16, -1.0, 1.0).astype(F32)
    return jnp.tile(cos, (1, 2)), jnp.tile(sin * sign, (1, 2))


def _s5_tables(a_re, a_im, log_dt, b_re, b_im, c_re, c_im):
    dt = jnp.exp(log_dt)[..., None]
    mag = jnp.exp(a_re * dt)
    ar = mag * jnp.cos(a_im * dt)
    ai = mag * jnp.sin(a_im * dt)
    den = a_re * a_re + a_im * a_im
    qr = ((ar - 1) * a_re + ai * a_im) / den
    qi = (ai * a_re - (ar - 1) * a_im) / den
    bbr = qr[..., None] * b_re - qi[..., None] * b_im
    bbi = qr[..., None] * b_im + qi[..., None] * b_re
    eye = jnp.eye(SSM_GROUPS, dtype=F32)
    to_in = lambda m: jnp.einsum('ldgnp,gh->ldgphn', m, eye).reshape(DEPTH, 2, SSM_W, NS)
    bm = jnp.concatenate([to_in(bbr), to_in(bbi)], axis=-1).astype(BF)
    to_out = lambda m: jnp.einsum('ldgpn,gh->ldgnhp', m, eye).reshape(DEPTH, 2, NS, SSM_W)
    cm = jnp.concatenate([to_out(c_re), -to_out(c_im)], axis=2).astype(BF)
    a_tab = jnp.stack([ar.reshape(DEPTH, 2, NS), ai.reshape(DEPTH, 2, NS)], axis=2)
    return a_tab, bm, cm


def kernel(x, c, ctx, c_ctx, w_mod, b_mod, norm1_w, norm2_w, w_in, q_norm_w, k_norm_w, attn_sink,
           ssm_a_re, ssm_a_im, ssm_log_dt, ssm_b_re, ssm_b_im, ssm_c_re, ssm_c_im, ssm_d, glu_w, glu_b,
           pool_w, pool_scale, out_norm_w, w_out, router_w, router_b, exp_w_gu, exp_b_gu, exp_w_down,
           exp_b_down):
    xc = jnp.concatenate([ctx, x], axis=1)
    cc = jnp.concatenate([c, c_ctx[None, :], jnp.zeros((16 - NB - 1, D), F32)], axis=0)
    mod = _adaln(cc, w_mod, b_mod)
    cos2, sin2 = _rope_tables()
    tri = (jnp.arange(TM_TOK)[:, None] < jnp.arange(TM_TOK)[None, :]).astype(BF)

    row = lambda a: a.reshape(DEPTH, 1, -1)
    dup = lambda m: jnp.concatenate([m[..., :64], m[..., :64], m[..., 64:], m[..., 64:]], axis=-1)
    w_ext = jnp.concatenate([w_in[..., :512], dup(w_in[..., 512:640]), dup(w_in[..., 640:768]),
                             w_in[..., 768:]], axis=-1).astype(BF)
    qw2 = row(jnp.tile(q_norm_w, (1, 2)))
    kw2 = row(jnp.tile(k_norm_w, (1, 2)))
    a_tab, bm, cm = _s5_tables(ssm_a_re, ssm_a_im, ssm_log_dt, ssm_b_re, ssm_b_im, ssm_c_re, ssm_c_im)
    pw_bd = jnp.einsum('lgcd,gh->lgchd', pool_w, jnp.eye(4, dtype=F32)).reshape(DEPTH, POOL_W, POOL_W).astype(BF)
    rw = jnp.concatenate([router_w, jnp.zeros((DEPTH, D, LANES - N_EXP), F32)], axis=-1)
    rb = row(jnp.concatenate([router_b, jnp.full((DEPTH, LANES - N_EXP), -1e30, F32)], axis=-1))
    glu_w_bf = glu_w.astype(BF)
    w_out_bf = w_out.astype(BF)
    n1, n2, onw = row(norm1_w), row(norm2_w), row(out_norm_w)
    ssm_d3, glu_b3, pool_sc3 = row(ssm_d), row(glu_b), row(pool_scale)
    sink = attn_sink.reshape(-1)

    tok = jnp.arange(N_TOK, dtype=jnp.int32)
    is_ctx_tok = (tok % LT) < L_CTX

    for l in range(DEPTH):
        last = l == DEPTH - 1
        q, kd, vd, u, p = _inproj(l, xc, mod, n1, w_ext, qw2, kw2, cos2, sin2)
        attn = _attention(l, sink, q, kd, vd, onw)
        yf, yb = _s5_scan(l, u, a_tab, bm, cm)
        pool = _pool(l, p, pw_bd, pool_sc3)
        x1, h2, route, cnt = _mix(l, not last, xc, attn, yf, yb, u, pool, mod, ssm_d3, glu_w_bf, glu_b3, onw,
                                  w_out_bf, n2, rw, rb, tri)

        by_k = lambda lo: route[:, lo:lo + TOP_K, :].transpose(1, 0, 2).reshape(TOP_K, N_TOK)
        top_i = by_k(0).astype(jnp.int32)
        rank = by_k(2 * TOP_K).astype(jnp.int32)
        counts = cnt[:, 0].astype(jnp.int32)
        padded = (counts + TM_MOE - 1) // TM_MOE * TM_MOE
        pend = jnp.cumsum(padded)
        pstart = pend - padded
        experts = jnp.arange(N_EXP, dtype=jnp.int32)
        dest_t = jnp.sum(jnp.where(top_i[..., None] == experts, pstart, 0), axis=-1) + rank
        n_used = (pend[-1] // TM_MOE).astype(jnp.int32)[None]
        tile_lo = jnp.arange(NT_MOE, dtype=jnp.int32) * TM_MOE
        tile_e = jnp.minimum(jnp.sum(pend[None, :] <= tile_lo[:, None], axis=1), N_EXP - 1).astype(jnp.int32)
        change = jnp.concatenate([jnp.ones((1,), jnp.int32), (tile_e[1:] != tile_e[:-1]).astype(jnp.int32)])
        slot = (jnp.cumsum(change) - 1) % 2
        next_first = jnp.sum(jnp.where(tile_e[:, None] == experts, pend // TM_MOE, 0), axis=1)
        tiles = jnp.arange(NT_MOE, dtype=jnp.int32)
        next_e = jnp.sum(jnp.where(next_first[:, None] == tiles, tile_e, 0), axis=1)
        nxt = jnp.where(next_first < n_used[0], next_e, -1)
        if last:
            xs = _row_dispatch(h2, jnp.where(is_ctx_tok, R_MOE + tok % TM_MOE, dest_t), R_MOE + TM_MOE)
            dest_t = jnp.where(is_ctx_tok, tok, dest_t)
        else:
            xs = _row_dispatch(h2, dest_t, R_MOE)

        yb_rows = _moe(l, tile_e, n_used, slot.astype(jnp.int32), nxt.astype(jnp.int32), xs,
                       exp_w_gu, exp_b_gu, exp_w_down, exp_b_down)
        gathered = _row_gather(yb_rows.reshape(R_MOE, PACK_ROWS, LANES),
                               dest_t.reshape(-1)).reshape(-1, LANES)
        xc = _combine(l, x1, route, mod, gathered, latent_only=last)
    return xc
```

```python
import functools
import math

import jax
import jax.numpy as jnp
from jax import lax
from jax.experimental import pallas as pl
from jax.experimental.pallas import tpu as pltpu
from jax.experimental.pallas import tpu_sc as plsc

D = 1024
NB = 8
L_LAT = 2048
L_CTX = 256
LT = L_CTX + L_LAT
DEPTH = 4
N_MOD = 6
EPS = 1e-6
N_HEADS = 8
HEAD_DIM = 64
ATTN_W = 512
WINDOW = 128
ATTN_SCALE = HEAD_DIM ** -0.5
LOG2E = math.log2(math.e)
ROPE_BASE = 10000.0
GRID_W = 64
SSM_W = 256
SSM_GROUP = 16
SSM_GROUPS = 16
SSM_STATE = 64
NS = SSM_GROUPS * SSM_STATE
POOL_W = 256
POOL_GROUP = 64
N_EXP = 32
TOP_K = 4
D_FF = 1024
SWIGLU_LIMIT = 7.0
SWIGLU_ALPHA = 1.702

LANES = 128
SUBLANES = 8
VMEM_LIMIT = 56 * 1024 * 1024

TM_TOK = 768
TQ = 256
KSPAN = TQ + 2 * WINDOW
T_SCAN = 128
N_CHUNK = LT // T_SCAN
N_CHUNK_CTX = L_CTX // T_SCAN
TM_MOE = 512
N_TOK = NB * LT
N_ASSIGN = N_TOK * TOP_K
NT_MOE = N_ASSIGN // TM_MOE + N_EXP
R_MOE = NT_MOE * TM_MOE
FF_CHUNK = 512
OUT_BLOCK = 256

SC_CORES = 2
SC_SUBCORES = 16
SC_WORKERS = SC_CORES * SC_SUBCORES
SC_CHUNK = 32
PACK_ROWS = D // 2 // LANES

C_Q = 0
C_K = 512
C_V = 768
C_U = 1024
C_P = 1280
NW_IN = 1536

BF = jnp.bfloat16
F32 = jnp.float32


def _split_bf16(a):
    hi = a.astype(BF)
    lo = (a - hi.astype(F32)).astype(BF)
    return hi, lo


def _dot(a, b):
    return jnp.dot(a, b, preferred_element_type=F32)


def _dot3(a, b):
    ah, al = _split_bf16(a)
    bh, bl = _split_bf16(b)
    return _dot(ah, bh) + (_dot(ah, bl) + _dot(al, bh))


def _mod_kernel(c_ref, w_ref, b_ref, o_ref):
    c = c_ref[...]
    s = c * jax.nn.sigmoid(c)
    o_ref[0] = _dot3(s, w_ref[0]) + b_ref[0]


def _adaln(cc, w_mod, b_mod):
    tn = 1536
    return pl.pallas_call(
        _mod_kernel,
        out_shape=jax.ShapeDtypeStruct((DEPTH, 16, N_MOD * D), F32),
        grid=(DEPTH, N_MOD * D // tn),
        in_specs=[
            pl.BlockSpec((16, D), lambda l, j: (0, 0)),
            pl.BlockSpec((1, D, tn), lambda l, j: (l, 0, j)),
            pl.BlockSpec((1, 1, tn), lambda l, j: (l, 0, j)),
        ],
        out_specs=pl.BlockSpec((1, 16, tn), lambda l, j: (l, 0, j)),
        compiler_params=pltpu.CompilerParams(
            dimension_semantics=("arbitrary", "arbitrary"), vmem_limit_bytes=VMEM_LIMIT),
        name="adaln_mod",
    )(cc, w_mod, b_mod.reshape(DEPTH, 1, N_MOD * D))


def _pack_rows(ref, val, rows):
    def bits(v):
        return lax.bitcast_convert_type(v.astype(BF).astype(F32), jnp.uint32)

    words = bits(val[:, :D // 2]) | (bits(val[:, D // 2:]) >> 16)
    for j in range(PACK_ROWS):
        ref[pl.ds(j, rows, stride=PACK_ROWS), :] = words[:, j * LANES:(j + 1) * LANES]


def _unpack_rows(ref, rows):
    words = jnp.concatenate([ref[pl.ds(j, rows, stride=PACK_ROWS), :] for j in range(PACK_ROWS)], axis=1)
    return jnp.concatenate([lax.bitcast_convert_type(words & jnp.uint32(0xFFFF0000), F32),
                            lax.bitcast_convert_type(words << 16, F32)], axis=1)


def _layer_spec(l, shape, col_block=0):
    shape = tuple(shape)
    return pl.BlockSpec((None,) + shape, lambda *_: (l,) + (0,) * (len(shape) - 1) + (col_block,))


def _with_mod(fn, mod_ref, b, i, idxs, *arrays):
    lat = [mod_ref[pl.ds(b, 1), k * D:(k + 1) * D] for k in idxs]
    top = [jnp.where(i == 0, mod_ref[NB:NB + 1, k * D:(k + 1) * D], v) for k, v in zip(idxs, lat)]
    return jnp.concatenate([fn(*[a[:L_CTX] for a in arrays], *top),
                            fn(*[a[L_CTX:] for a in arrays], *lat)], axis=0)


def _rmsnorm(x, w):
    ms = jnp.mean(x * x, axis=-1, keepdims=True)
    return x * lax.rsqrt(ms + EPS) * w


def _headnorm_pair(t, w2, lane):
    sq = t * t
    first = lane < HEAD_DIM
    a = jnp.sum(jnp.where(first, sq, 0.0), axis=-1, keepdims=True)
    b = jnp.sum(jnp.where(first, 0.0, sq), axis=-1, keepdims=True)
    ms = jnp.where(first, a, b) * (1.0 / HEAD_DIM)
    return t * lax.rsqrt(ms + EPS) * w2


def _rope_pair(t, cos, sin_signed, lane):
    lower = (lane % 32) < 16
    partner = jnp.where(lower, pltpu.roll(t, LANES - 16, 1), pltpu.roll(t, 16, 1))
    return t * cos + partner * sin_signed


def _inproj_kernel(x_ref, mod_ref, n1_ref, w_ref, qw_ref, kw_ref, cos_ref, sin_ref,
                   q_ref, k_ref, v_ref, u_ref, p_ref):
    b = pl.program_id(0)
    i = pl.program_id(1)
    h = _rmsnorm(x_ref[0], n1_ref[...])
    h = _with_mod(lambda hr, shift, scale: (hr * (1.0 + scale) + shift).astype(BF), mod_ref, b, i, (0, 1), h)
    px = _dot(h, w_ref[...])
    lane = lax.broadcasted_iota(jnp.int32, (TM_TOK, LANES), 1)
    cos = cos_ref[...]
    sin = sin_ref[...]
    for j in range(ATTN_W // LANES):
        t = px[:, C_Q + j * LANES:C_Q + (j + 1) * LANES]
        t = _rope_pair(_headnorm_pair(t, qw_ref[...], lane), cos, sin, lane) * (ATTN_SCALE * LOG2E)
        q_ref[0, :, j * LANES:(j + 1) * LANES] = t.astype(BF)
    for g in range(2):
        t = px[:, C_K + g * LANES:C_K + (g + 1) * LANES]
        t = _rope_pair(_headnorm_pair(t, kw_ref[...], lane), cos, sin, lane)
        k_ref[0, :, g * LANES:(g + 1) * LANES] = t.astype(BF)
        t = px[:, C_V + g * LANES:C_V + (g + 1) * LANES]
        v_ref[0, :, g * LANES:(g + 1) * LANES] = jnp.where(lane < HEAD_DIM, t, 1.0).astype(BF)
    u_ref[0] = px[:, C_U:C_P]
    p_ref[0] = px[:, C_P:NW_IN]


def _inproj(l, xc, mod, n1, w_ext, qw2, kw2, cos2, sin2):
    tok = lambda w: pl.BlockSpec((1, TM_TOK, w), lambda b, i: (b, i, 0))
    full = lambda s: _layer_spec(l, s)
    return pl.pallas_call(
        _inproj_kernel,
        out_shape=(
            jax.ShapeDtypeStruct((NB, LT, ATTN_W), BF),
            jax.ShapeDtypeStruct((NB, LT, 256), BF),
            jax.ShapeDtypeStruct((NB, LT, 256), BF),
            jax.ShapeDtypeStruct((NB, LT, SSM_W), F32),
            jax.ShapeDtypeStruct((NB, LT, POOL_W), F32),
        ),
        grid=(NB, LT // TM_TOK),
        in_specs=[
            tok(D), full((16, N_MOD * D)), full((1, D)), full((D, NW_IN)),
            full((1, LANES)), full((1, LANES)),
            pl.BlockSpec((TM_TOK, LANES), lambda b, i: (i, 0)),
            pl.BlockSpec((TM_TOK, LANES), lambda b, i: (i, 0)),
        ],
        out_specs=(tok(ATTN_W), tok(256), tok(256), tok(SSM_W), tok(POOL_W)),
        compiler_params=pltpu.CompilerParams(
            dimension_semantics=("arbitrary", "arbitrary"), vmem_limit_bytes=VMEM_LIMIT),
        name="inproj",
    )(xc, mod, n1, w_ext, qw2, kw2, cos2, sin2)


def _attn_kernel(l, sink_ref, q_ref, k_ref, v_ref, onw_ref, o_ref, bias_ref):
    j = pl.program_id(1)
    start = pl.multiple_of(jnp.clip(j * TQ - WINDOW, LANES, LT - KSPAN), LANES)
    lane = lax.broadcasted_iota(jnp.int32, (TQ, LANES), 1)
    first = lane < HEAD_DIM
    row2 = lax.broadcasted_iota(jnp.int32, (2 * TQ, 1), 0)
    qpos = j * TQ + jnp.where(row2 < TQ, row2, row2 - TQ)
    kpos = start + lax.broadcasted_iota(jnp.int32, (1, KSPAN), 1)
    valid = (jnp.abs(qpos - kpos) <= WINDOW) & (kpos >= L_CTX) & (j >= 1)
    bias_ref[...] = jnp.where(valid, 0.0, -1e30)
    n_pairs = N_HEADS // 2

    def kv(ref, hp, rows):
        g = hp // 2
        return ref[0, rows, g * LANES:(g + 1) * LANES]

    def scores(hp):
        q2 = q_ref[0, :, hp * LANES:(hp + 1) * LANES]
        zero = jnp.zeros_like(q2)
        qs = jnp.concatenate([jnp.where(first, q2, zero), jnp.where(first, zero, q2)], axis=0)
        dn = (((1,), (1,)), ((), ()))
        s_ctx = lax.dot_general(qs, kv(k_ref, hp, slice(0, L_CTX)), dn, preferred_element_type=F32)
        s_loc = lax.dot_general(qs, kv(k_ref, hp, pl.ds(start, KSPAN)), dn, preferred_element_type=F32)
        return s_loc, s_ctx

    nxt = scores(0)
    outs = []
    for hp in range(n_pairs):
        s_loc, s_ctx = nxt
        if hp + 1 < n_pairs:
            nxt = scores(hp + 1)
        s_loc = s_loc + bias_ref[...]
        sink = jnp.where(row2 < TQ, sink_ref[l * N_HEADS + 2 * hp],
                         sink_ref[l * N_HEADS + 2 * hp + 1]) * LOG2E
        m = jnp.maximum(jnp.maximum(jnp.max(s_loc, axis=-1, keepdims=True),
                                    jnp.max(s_ctx, axis=-1, keepdims=True)), sink)
        e_loc = jnp.exp2(s_loc - m)
        e_ctx = jnp.exp2(s_ctx - m)
        o = (_dot(e_loc.astype(BF), kv(v_ref, hp, pl.ds(start, KSPAN)))
             + _dot(e_ctx.astype(BF), kv(v_ref, hp, slice(0, L_CTX))))
        den = pltpu.roll(o + jnp.exp2(sink - m), HEAD_DIM, 1)
        o = o / den
        outs.append(jnp.where(first, o[0:TQ], pltpu.roll(o[TQ:2 * TQ], HEAD_DIM, 1)))
    o_ref[0] = _rmsnorm(jnp.concatenate(outs, axis=1), onw_ref[...]).astype(BF)


def _attention(l, sink, q, kd, vd, onw):
    return pl.pallas_call(
        functools.partial(_attn_kernel, l),
        out_shape=jax.ShapeDtypeStruct((NB, LT, ATTN_W), BF),
        grid_spec=pltpu.PrefetchScalarGridSpec(
            num_scalar_prefetch=1,
            grid=(NB, LT // TQ),
            in_specs=[
                pl.BlockSpec((1, TQ, ATTN_W), lambda b, j, s: (b, j, 0)),
                pl.BlockSpec((1, LT, 256), lambda b, j, s: (b, 0, 0)),
                pl.BlockSpec((1, LT, 256), lambda b, j, s: (b, 0, 0)),
                _layer_spec(l, (1, ATTN_W)),
            ],
            out_specs=pl.BlockSpec((1, TQ, ATTN_W), lambda b, j, s: (b, j, 0)),
            scratch_shapes=[pltpu.VMEM((2 * TQ, KSPAN), F32)],
        ),
        compiler_params=pltpu.CompilerParams(
            dimension_semantics=("arbitrary", "arbitrary"), vmem_limit_bytes=VMEM_LIMIT),
        name="window_attn",
    )(sink, q, kd, vd, onw)


def _bwd_chunk(i):
    return jnp.where(i < N_CHUNK_CTX, N_CHUNK_CTX - 1 - i, N_CHUNK - 1 - (i - N_CHUNK_CTX))


def _scan_kernel(uf_ref, ub_ref, a_ref, bm_ref, cm_ref, yf_ref, yb_ref,
                 xf_ref, xb_ref, st_ref, *il_refs):
    i = pl.program_id(0)
    halves = SSM_W // LANES

    @pl.when(i == 0)
    def _():
        st_ref[...] = jnp.zeros_like(st_ref)

    def project(d, u_ref, xs_ref):
        il = il_refs[d * halves:(d + 1) * halves]
        for h in range(halves):
            for bb in range(NB):
                il[h][pl.ds(bb, T_SCAN, stride=NB), :] = u_ref[bb, :, h * LANES:(h + 1) * LANES]
        ui = jnp.concatenate([r[...] for r in il], axis=1).astype(BF)
        xs_ref[...] = _dot(ui, bm_ref[d])

    def scan(d, xs_ref, reverse):
        ar = jnp.broadcast_to(a_ref[d, 0:1, :], (NB, NS))
        ai = jnp.broadcast_to(a_ref[d, 1:2, :], (NB, NS))
        sr = st_ref[d, 0]
        si = st_ref[d, 1]
        for t in (range(T_SCAN - 1, -1, -1) if reverse else range(T_SCAN)):
            r = pl.ds(t * NB, NB)
            nr = ar * sr - ai * si + xs_ref[r, 0:NS]
            ni = ar * si + ai * sr + xs_ref[r, NS:2 * NS]
            sr, si = nr, ni
            xs_ref[r, 0:NS] = sr
            xs_ref[r, NS:2 * NS] = si
        st_ref[d, 0] = sr
        st_ref[d, 1] = si

    def readout(d, xs_ref, y_ref):
        y = _dot(xs_ref[...].astype(BF), cm_ref[d])
        il = il_refs[d * halves:(d + 1) * halves]
        for h in range(halves):
            il[h][...] = y[:, h * LANES:(h + 1) * LANES]
            for bb in range(NB):
                y_ref[bb, :, h * LANES:(h + 1) * LANES] = il[h][pl.ds(bb, T_SCAN, stride=NB), :]

    project(0, uf_ref, xf_ref)
    project(1, ub_ref, xb_ref)
    scan(0, xf_ref, False)
    readout(0, xf_ref, yf_ref)
    scan(1, xb_ref, True)
    readout(1, xb_ref, yb_ref)


def _s5_scan(l, u, a_tab, bm, cm):
    rows = NB * T_SCAN
    layer = lambda s: _layer_spec(l, s)
    chunk_f = pl.BlockSpec((NB, T_SCAN, SSM_W), lambda i: (0, i, 0))
    chunk_b = pl.BlockSpec((NB, T_SCAN, SSM_W), lambda i: (0, _bwd_chunk(i), 0))
    return pl.pallas_call(
        _scan_kernel,
        out_shape=(jax.ShapeDtypeStruct((NB, LT, SSM_W), F32),
                   jax.ShapeDtypeStruct((NB, LT, SSM_W), F32)),
        grid=(N_CHUNK,),
        in_specs=[chunk_f, chunk_b, layer((2, 2, NS)), layer((2, SSM_W, 2 * NS)), layer((2, 2 * NS, SSM_W))],
        out_specs=(chunk_f, chunk_b),
        scratch_shapes=[pltpu.VMEM((rows, 2 * NS), F32), pltpu.VMEM((rows, 2 * NS), F32),
                        pltpu.VMEM((2, 2, NB, NS), F32),
                        ] + [pltpu.VMEM((rows, LANES), F32)] * (2 * (SSM_W // LANES)),
        compiler_params=pltpu.CompilerParams(
            dimension_semantics=("arbitrary",), vmem_limit_bytes=VMEM_LIMIT),
        name="s5_scan",
    )(u, u, a_tab, bm, cm)


def _pool_segment(ps, length):
    n = length + 2 * SUBLANES
    z = jnp.zeros((SUBLANES, POOL_W), F32)
    pe = jnp.concatenate([z, ps, z], axis=0)
    a1 = pe + pltpu.roll(pe, 1, 0)
    a2 = a1 + pltpu.roll(a1, 2, 0)
    a3 = a2 + pltpu.roll(a2, 4, 0)
    a4 = a3 + pltpu.roll(a3, 8, 0)
    lane = lax.broadcasted_iota(jnp.int32, (1, POOL_W), 1)
    half = jnp.where(lane < 64, 1, jnp.where(lane < 128, 2, jnp.where(lane < 192, 4, 8)))
    s = jnp.where(lane < 64, a1,
                  jnp.where(lane < 128, pltpu.roll(a2, n - 1, 0),
                            jnp.where(lane < 192, pltpu.roll(a3, n - 3, 0), pltpu.roll(a4, n - 7, 0))))
    s = s[SUBLANES:SUBLANES + length]
    t = lax.broadcasted_iota(jnp.int32, (length, 1), 0)
    cnt = jnp.minimum(t + half, length) - jnp.maximum(t - half, 0)
    return s / cnt.astype(F32) - ps


def _pool_kernel(p_ref, w_ref, sc_ref, o_ref):
    for lo, length in ((0, L_CTX), (L_CTX, L_LAT)):
        dlt = _pool_segment(p_ref[0, lo:lo + length, :], length)
        y = _dot(dlt.astype(BF), w_ref[...]) * sc_ref[...]
        o_ref[0, lo:lo + length, :] = y.astype(BF)


def _pool(l, p, w_bd, scale):
    return pl.pallas_call(
        _pool_kernel,
        out_shape=jax.ShapeDtypeStruct((NB, LT, POOL_W), BF),
        grid=(NB,),
        in_specs=[pl.BlockSpec((1, LT, POOL_W), lambda b: (b, 0, 0)),
                  _layer_spec(l, (POOL_W, POOL_W)),
                  _layer_spec(l, (1, POOL_W))],
        out_specs=pl.BlockSpec((1, LT, POOL_W), lambda b: (b, 0, 0)),
        compiler_params=pltpu.CompilerParams(
            dimension_semantics=("arbitrary",), vmem_limit_bytes=VMEM_LIMIT),
        name="pool",
    )(p, w_bd, scale)


def _mix_kernel(route_ctx, x_ref, at_ref, yf_ref, yb_ref, u_ref, pl_ref, mod_ref, d_ref, gw_ref, gb_ref, onw_ref,
                wo_ref, n2_ref, rw_ref, rb_ref, tri_ref,
                x1_ref, h2_ref, rt_ref, cnt_ref, run_ref):
    b = pl.program_id(0)
    i = pl.program_id(1)

    @pl.when((b == 0) & (i == 0))
    def _():
        run_ref[...] = jnp.zeros_like(run_ref)

    y = yf_ref[0] + yb_ref[0] + d_ref[...] * u_ref[0]
    g = jax.nn.gelu(y, approximate=True)
    s = g * jax.nn.sigmoid(_dot(g.astype(BF), gw_ref[...]) + gb_ref[...])
    s = _rmsnorm(s, onw_ref[...]).astype(BF)
    mix = (_dot(at_ref[0], wo_ref[0:ATTN_W, :])
           + _dot(s, wo_ref[ATTN_W:ATTN_W + SSM_W, :])
           + _dot(pl_ref[0], wo_ref[ATTN_W + SSM_W:D, :]))
    x1 = _with_mod(lambda xr, mr, gate: xr + gate * mr, mod_ref, b, i, (2,), x_ref[0], mix)
    x1_ref[0] = x1
    h2 = _rmsnorm(x1, n2_ref[...])
    h2 = _with_mod(lambda hr, shift, scale: hr * (1.0 + scale) + shift, mod_ref, b, i, (3, 4), h2)
    _pack_rows(h2_ref, h2, TM_TOK)
    logits = (_dot3(h2, rw_ref[...]) + rb_ref[...]).T[0:N_EXP]
    eidx = lax.broadcasted_iota(jnp.int32, (N_EXP, TM_TOK), 0).astype(F32)
    vals, idxs, hits = [], [], []
    cur = logits
    for _ in range(TOP_K):
        m = jnp.max(cur, axis=0, keepdims=True)
        idx = jnp.min(jnp.where(cur == m, eidx, float(N_EXP)), axis=0, keepdims=True)
        hit = eidx == idx
        vals.append(m)
        idxs.append(idx)
        hits.append(hit)
        cur = jnp.where(hit, -jnp.inf, cur)
    ex = [jnp.exp(v - vals[0]) for v in vals]
    den = ex[0] + ex[1] + ex[2] + ex[3]
    onehot = jnp.where(hits[0] | hits[1] | hits[2] | hits[3], 1.0, 0.0)
    if not route_ctx:
        tok = lax.broadcasted_iota(jnp.int32, (1, TM_TOK), 1)
        onehot = jnp.where((tok >= L_CTX) | (i > 0), onehot, 0.0)
    run = run_ref[:, 0:1]
    before = _dot(onehot.astype(BF), tri_ref[...]) + run
    row = lax.broadcasted_iota(jnp.int32, (4 * TOP_K, TM_TOK), 0)
    route = jnp.zeros((4 * TOP_K, TM_TOK), F32)
    for k in range(TOP_K):
        rank = jnp.sum(jnp.where(hits[k], before, 0.0), axis=0, keepdims=True)
        route = jnp.where(row == k, idxs[k], route)
        route = jnp.where(row == TOP_K + k, ex[k] / den, route)
        route = jnp.where(row == 2 * TOP_K + k, rank, route)
    rt_ref[0] = route
    run_ref[...] = jnp.broadcast_to(run + jnp.sum(onehot, axis=1, keepdims=True), (N_EXP, LANES))
    cnt_ref[...] = run_ref[...]


def _mix(l, route_ctx, xc, attn, yf, yb, u, pool, mod, ssm_d, glu_w, glu_b, onw, w_out, n2, rw, rb, tri):
    tok = lambda w: pl.BlockSpec((1, TM_TOK, w), lambda b, i: (b, i, 0))
    full = lambda s: pl.BlockSpec(s, lambda b, i: (0,) * len(s))
    layer = lambda s, col=0: _layer_spec(l, s, col)
    return pl.pallas_call(
        functools.partial(_mix_kernel, route_ctx),
        out_shape=(jax.ShapeDtypeStruct((NB, LT, D), F32),
                   jax.ShapeDtypeStruct((N_TOK * PACK_ROWS, LANES), jnp.uint32),
                   jax.ShapeDtypeStruct((N_TOK // TM_TOK, 4 * TOP_K, TM_TOK), F32),
                   jax.ShapeDtypeStruct((N_EXP, LANES), F32)),
        grid=(NB, LT // TM_TOK),
        in_specs=[tok(D), tok(ATTN_W), tok(SSM_W), tok(SSM_W), tok(SSM_W), tok(POOL_W),
                  layer((16, N_MOD * D)), layer((1, SSM_W)), layer((SSM_W, SSM_W)), layer((1, SSM_W)),
                  layer((1, SSM_W), ATTN_W // SSM_W), layer((D, D)), layer((1, D)), layer((D, LANES)),
                  layer((1, LANES)), full((TM_TOK, TM_TOK))],
        out_specs=(tok(D),
                   pl.BlockSpec((TM_TOK * PACK_ROWS, LANES), lambda b, i: (b * (LT // TM_TOK) + i, 0)),
                   pl.BlockSpec((1, 4 * TOP_K, TM_TOK), lambda b, i: (b * (LT // TM_TOK) + i, 0, 0)),
                   full((N_EXP, LANES))),
        scratch_shapes=[pltpu.VMEM((N_EXP, LANES), F32)],
        compiler_params=pltpu.CompilerParams(
            dimension_semantics=("arbitrary", "arbitrary"), vmem_limit_bytes=VMEM_LIMIT),
        name="mix_router",
    )(xc, attn, yf, yb, u, pool, mod, ssm_d, glu_w, glu_b, onw, w_out, n2, rw, rb, tri)


def _moe_kernel(l, te_ref, nu_ref, slot_ref, nxt_ref, x_ref, wgu_hbm, bgu_ref, wd_hbm, bd_ref, o_ref,
                wgu_f, wd_f, wgu_s, wd_s, act_s, sem):
    i = pl.program_id(0)

    @pl.when(i >= nu_ref[0])
    def _():
        o_ref[...] = jnp.zeros_like(o_ref)

    def fetch(e, slot):
        return (pltpu.make_async_copy(wgu_hbm.at[l, e], wgu_f.at[slot], sem.at[0, slot]),
                pltpu.make_async_copy(wd_hbm.at[l, e], wd_f.at[slot], sem.at[1, slot]))

    @pl.when(i < nu_ref[0])
    def _():
        e = te_ref[i]
        prev = te_ref[jnp.maximum(i - 1, 0)]
        slot = slot_ref[i]

        @pl.when(i == 0)
        def _():
            for cp in fetch(e, slot):
                cp.start()

        @pl.when((i == 0) | (e != prev))
        def _():
            for cp in fetch(e, slot):
                cp.wait()
            nxt = nxt_ref[i]

            @pl.when(nxt >= 0)
            def _():
                for cp in fetch(nxt, 1 - slot):
                    cp.start()

            def cast_gu(r, c):
                rs = pl.ds(pl.multiple_of(r * 128, 128), 128)
                wgu_s[rs, :] = wgu_f[slot, rs, :].astype(BF)
                return c

            def cast_d(r, c):
                rs = pl.ds(pl.multiple_of(r * 128, 128), 128)
                wd_s[rs, :] = wd_f[slot, rs, :].astype(BF)
                return c

            lax.fori_loop(0, D // 128, cast_gu, 0)
            lax.fori_loop(0, D_FF // 128, cast_d, 0)

        x = _unpack_rows(x_ref, TM_MOE).astype(BF)
        for c in range(D_FF // FF_CHUNK):
            lo = c * FF_CHUNK
            gate = _dot(x, wgu_s[:, lo:lo + FF_CHUNK]) + bgu_ref[:, lo:lo + FF_CHUNK]
            up = (_dot(x, wgu_s[:, D_FF + lo:D_FF + lo + FF_CHUNK])
                  + bgu_ref[:, D_FF + lo:D_FF + lo + FF_CHUNK])
            gate = jnp.minimum(gate, SWIGLU_LIMIT)
            up = jnp.clip(up, -SWIGLU_LIMIT, SWIGLU_LIMIT)
            act = (up + 1.0) * (gate * jax.nn.sigmoid(SWIGLU_ALPHA * gate))
            act_s[:, lo:lo + FF_CHUNK] = act.astype(BF)
        def bf16_bits(v):
            return lax.bitcast_convert_type(v.astype(BF).astype(F32), jnp.uint32)

        for n in range(D // 2 // OUT_BLOCK):
            lo = n * OUT_BLOCK
            y_hi = _dot(act_s[...], wd_s[:, lo:lo + OUT_BLOCK]) + bd_ref[:, lo:lo + OUT_BLOCK]
            y_lo = (_dot(act_s[...], wd_s[:, D // 2 + lo:D // 2 + lo + OUT_BLOCK])
                    + bd_ref[:, D // 2 + lo:D // 2 + lo + OUT_BLOCK])
            words = bf16_bits(y_hi) | (bf16_bits(y_lo) >> 16)
            for j in range(OUT_BLOCK // LANES):
                o_ref[pl.ds(lo // LANES + j, TM_MOE, stride=PACK_ROWS), :] = words[:, j * LANES:(j + 1) * LANES]


def _moe(l, tile_e, n_used, slot, nxt, xs, w_gu, b_gu, w_down, b_down):
    def tile(i, te, nu, *_):
        return (jnp.minimum(i, nu[0] - 1), 0)

    def bias(i, te, nu, *_):
        return (l, te[jnp.minimum(i, nu[0] - 1)], 0, 0)

    return pl.pallas_call(
        functools.partial(_moe_kernel, l),
        out_shape=jax.ShapeDtypeStruct((R_MOE * PACK_ROWS, LANES), jnp.uint32),
        grid_spec=pltpu.PrefetchScalarGridSpec(
            num_scalar_prefetch=4,
            grid=(NT_MOE,),
            in_specs=[
                pl.BlockSpec((TM_MOE * PACK_ROWS, LANES), tile),
                pl.BlockSpec(memory_space=pl.ANY),
                pl.BlockSpec((None, None, 1, 2 * D_FF), bias),
                pl.BlockSpec(memory_space=pl.ANY),
                pl.BlockSpec((None, None, 1, D), bias),
            ],
            out_specs=pl.BlockSpec((TM_MOE * PACK_ROWS, LANES), lambda i, *_: (i, 0)),
            scratch_shapes=[pltpu.VMEM((2, D, 2 * D_FF), F32), pltpu.VMEM((2, D_FF, D), F32),
                            pltpu.VMEM((D, 2 * D_FF), BF), pltpu.VMEM((D_FF, D), BF),
                            pltpu.VMEM((TM_MOE, D_FF), BF), pltpu.SemaphoreType.DMA((2, 2))],
        ),
        compiler_params=pltpu.CompilerParams(
            dimension_semantics=("arbitrary",), vmem_limit_bytes=VMEM_LIMIT),
        name="moe_experts",
    )(tile_e, n_used, slot, nxt, xs, w_gu, b_gu.reshape(DEPTH, N_EXP, 1, 2 * D_FF), w_down,
      b_down.reshape(DEPTH, N_EXP, 1, D))


def _row_gather(table, idx):
    n_rows = idx.shape[0]
    per_worker = n_rows // SC_WORKERS
    n_chunks = per_worker // SC_CHUNK
    assert per_worker * SC_WORKERS == n_rows and n_chunks * SC_CHUNK == per_worker and n_chunks % 2 == 0
    mesh = plsc.VectorSubcoreMesh(core_axis_name="c", subcore_axis_name="s")
    row_tile = (SC_CHUNK,) + table.shape[1:]

    @functools.partial(
        pl.kernel, mesh=mesh,
        out_type=jax.ShapeDtypeStruct((n_rows,) + table.shape[1:], table.dtype),
        scratch_types=[pltpu.VMEM((n_chunks, SC_CHUNK), jnp.int32),
                       pltpu.VMEM(row_tile, table.dtype), pltpu.VMEM(row_tile, table.dtype),
                       pltpu.SemaphoreType.DMA, pltpu.SemaphoreType.DMA],
        name="sc_row_gather",
    )
    def gather(table_hbm, idx_hbm, out_hbm, idx_v, buf0, buf1, sem0, sem1):
        wid = lax.axis_index("s") * SC_CORES + lax.axis_index("c")
        pltpu.sync_copy(idx_hbm.at[pl.ds(wid * n_chunks, n_chunks)], idx_v)
        base = wid * per_worker

        def fetch(chunk, buf, sem):
            return pltpu.make_async_copy(table_hbm.at[idx_v.at[chunk]], buf, sem)

        def emit(chunk, buf):
            pltpu.sync_copy(buf, out_hbm.at[pl.ds(base + chunk * SC_CHUNK, SC_CHUNK)])

        fetch(0, buf0, sem0).start()

        @pl.loop(0, n_chunks, step=2)
        def _(c):
            fetch(c + 1, buf1, sem1).start()
            fetch(c, buf0, sem0).wait()
            emit(c, buf0)

            @pl.when(c + 2 < n_chunks)
            def _():
                fetch(c + 2, buf0, sem0).start()

            fetch(c + 1, buf1, sem1).wait()
            emit(c + 1, buf1)

    return gather(table, idx.reshape(n_rows // SC_CHUNK, SC_CHUNK))


def _row_dispatch(rows, dest_t, n_out):
    per_worker = N_TOK // SC_WORKERS
    n_chunks = per_worker // SC_CHUNK
    assert per_worker * SC_WORKERS == N_TOK and n_chunks * SC_CHUNK == per_worker and n_chunks % 2 == 0
    idx_rows = TOP_K * n_chunks
    assert idx_rows % SUBLANES == 0
    mesh = plsc.VectorSubcoreMesh(core_axis_name="c", subcore_axis_name="s")
    row_tile = (SC_CHUNK,) + rows.shape[1:]

    @functools.partial(
        pl.kernel, mesh=mesh,
        out_type=jax.ShapeDtypeStruct((n_out,) + rows.shape[1:], rows.dtype),
        scratch_types=[pltpu.VMEM((idx_rows, SC_CHUNK), jnp.int32),
                       pltpu.VMEM(row_tile, rows.dtype), pltpu.VMEM(row_tile, rows.dtype),
                       pltpu.SemaphoreType.DMA, pltpu.SemaphoreType.DMA],
        name="sc_row_dispatch",
    )
    def dispatch(rows_hbm, idx_hbm, out_hbm, idx_v, buf0, buf1, sem0, sem1):
        wid = lax.axis_index("s") * SC_CORES + lax.axis_index("c")
        pltpu.sync_copy(idx_hbm.at[pl.ds(wid * idx_rows, idx_rows)], idx_v)
        base = wid * per_worker

        def fetch(chunk, buf, sem):
            return pltpu.make_async_copy(rows_hbm.at[pl.ds(base + chunk * SC_CHUNK, SC_CHUNK)], buf, sem)

        def emit(chunk, buf):
            for k in range(TOP_K):
                pltpu.sync_copy(buf, out_hbm.at[idx_v.at[k * n_chunks + chunk]])

        fetch(0, buf0, sem0).start()

        @pl.loop(0, n_chunks, step=2)
        def _(c):
            fetch(c + 1, buf1, sem1).start()
            fetch(c, buf0, sem0).wait()
            emit(c, buf0)

            @pl.when(c + 2 < n_chunks)
            def _():
                fetch(c + 2, buf0, sem0).start()

            fetch(c + 1, buf1, sem1).wait()
            emit(c + 1, buf1)

    idx = dest_t.reshape(TOP_K, SC_WORKERS, n_chunks, SC_CHUNK).transpose(1, 0, 2, 3)
    return dispatch(rows, idx.reshape(SC_WORKERS * idx_rows, SC_CHUNK))


def _combine_kernel(first_tile, x1_ref, rt_ref, mod_ref, g0_ref, g1_ref, g2_ref, g3_ref, o_ref):
    t = pl.program_id(0)
    per_b = LT // TQ - first_tile
    b = t // per_b
    is_ctx = (t % per_b + first_tile) == 0
    route = rt_ref[0].T
    y = jnp.zeros((TQ, D), F32)
    for k, g_ref in enumerate((g0_ref, g1_ref, g2_ref, g3_ref)):
        y = y + route[:, TOP_K + k:TOP_K + k + 1] * _unpack_rows(g_ref, TQ)
    gate = jnp.where(is_ctx, mod_ref[NB:NB + 1, 5 * D:6 * D], mod_ref[pl.ds(b, 1), 5 * D:6 * D])
    o_ref[0] = x1_ref[0] + gate * y


def _combine(l, x1, route, mod, gathered, latent_only):
    tiles_b = LT // TQ
    first_tile = L_CTX // TQ if latent_only else 0
    per_b = tiles_b - first_tile
    tile_of = lambda t: (t // per_b, t % per_b + first_tile)
    tok = lambda w: pl.BlockSpec((1, TQ, w), lambda t: tile_of(t) + (0,))
    slot = lambda k: pl.BlockSpec(
        (TQ * PACK_ROWS, LANES),
        lambda t: (k * NB * tiles_b + tile_of(t)[0] * tiles_b + tile_of(t)[1], 0))
    return pl.pallas_call(
        functools.partial(_combine_kernel, first_tile),
        out_shape=jax.ShapeDtypeStruct((NB, per_b * TQ, D), F32),
        grid=(NB * per_b,),
        in_specs=[tok(D),
                  pl.BlockSpec((1, 4 * TOP_K, TQ),
                               lambda t: (tile_of(t)[0] * (LT // TM_TOK) + tile_of(t)[1] // (TM_TOK // TQ), 0,
                                          tile_of(t)[1] % (TM_TOK // TQ))),
                  _layer_spec(l, (16, N_MOD * D)),
                  slot(0), slot(1), slot(2), slot(3)],
        out_specs=pl.BlockSpec((1, TQ, D), lambda t: (t // per_b, t % per_b, 0)),
        compiler_params=pltpu.CompilerParams(
            dimension_semantics=("arbitrary",), vmem_limit_bytes=VMEM_LIMIT),
        name="moe_combine",
    )(x1, route, mod, gathered, gathered, gathered, gathered)


def _rope_tables():
    pos = jnp.arange(L_LAT)
    row = (pos // GRID_W).astype(F32)
    col = (pos % GRID_W).astype(F32)
    inv = ROPE_BASE ** (-jnp.arange(0, 32, 2, dtype=F32) / 32)
    ang_r = row[:, None] * inv
    ang_c = col[:, None] * inv
    ang = jnp.concatenate([ang_r, ang_r, ang_c, ang_c], axis=-1)
    cos = jnp.concatenate([jnp.ones((L_CTX, HEAD_DIM), F32), jnp.cos(ang)], axis=0)
    sin = jnp.concatenate([jnp.zeros((L_CTX, HEAD_DIM), F32), jnp.sin(ang)], axis=0)
    sign = jnp.where((jnp.arange(HEAD_DIM) % 32) < 16, -1.0, 1.0).astype(F32)
    return jnp.tile(cos, (1, 2)), jnp.tile(sin * sign, (1, 2))


def _s5_tables(a_re, a_im, log_dt, b_re, b_im, c_re, c_im):
    dt = jnp.exp(log_dt)[..., None]
    mag = jnp.exp(a_re * dt)
    ar = mag * jnp.cos(a_im * dt)
    ai = mag * jnp.sin(a_im * dt)
    den = a_re * a_re + a_im * a_im
    qr = ((ar - 1) * a_re + ai * a_im) / den
    qi = (ai * a_re - (ar - 1) * a_im) / den
    bbr = qr[..., None] * b_re - qi[..., None] * b_im
    bbi = qr[..., None] * b_im + qi[..., None] * b_re
    eye = jnp.eye(SSM_GROUPS, dtype=F32)
    to_in = lambda m: jnp.einsum('ldgnp,gh->ldgphn', m, eye).reshape(DEPTH, 2, SSM_W, NS)
    bm = jnp.concatenate([to_in(bbr), to_in(bbi)], axis=-1).astype(BF)
    to_out = lambda m: jnp.einsum('ldgpn,gh->ldgnhp', m, eye).reshape(DEPTH, 2, NS, SSM_W)
    cm = jnp.concatenate([to_out(c_re), -to_out(c_im)], axis=2).astype(BF)
    a_tab = jnp.stack([ar.reshape(DEPTH, 2, NS), ai.reshape(DEPTH, 2, NS)], axis=2)
    return a_tab, bm, cm


def kernel(x, c, ctx, c_ctx, w_mod, b_mod, norm1_w, norm2_w, w_in, q_norm_w, k_norm_w, attn_sink,
           ssm_a_re, ssm_a_im, ssm_log_dt, ssm_b_re, ssm_b_im, ssm_c_re, ssm_c_im, ssm_d, glu_w, glu_b,
           pool_w, pool_scale, out_norm_w, w_out, router_w, router_b, exp_w_gu, exp_b_gu, exp_w_down,
           exp_b_down):
    xc = jnp.concatenate([ctx, x], axis=1)
    cc = jnp.concatenate([c, c_ctx[None, :], jnp.zeros((16 - NB - 1, D), F32)], axis=0)
    mod = _adaln(cc, w_mod, b_mod)
    cos2, sin2 = _rope_tables()
    tri = (jnp.arange(TM_TOK)[:, None] < jnp.arange(TM_TOK)[None, :]).astype(BF)

    row = lambda a: a.reshape(DEPTH, 1, -1)
    dup = lambda m: jnp.concatenate([m[..., :64], m[..., :64], m[..., 64:], m[..., 64:]], axis=-1)
    w_ext = jnp.concatenate([w_in[..., :512], dup(w_in[..., 512:640]), dup(w_in[..., 640:768]),
                             w_in[..., 768:]], axis=-1).astype(BF)
    qw2 = row(jnp.tile(q_norm_w, (1, 2)))
    kw2 = row(jnp.tile(k_norm_w, (1, 2)))
    a_tab, bm, cm = _s5_tables(ssm_a_re, ssm_a_im, ssm_log_dt, ssm_b_re, ssm_b_im, ssm_c_re, ssm_c_im)
    pw_bd = jnp.einsum('lgcd,gh->lgchd', pool_w, jnp.eye(4, dtype=F32)).reshape(DEPTH, POOL_W, POOL_W).astype(BF)
    rw = jnp.concatenate([router_w, jnp.zeros((DEPTH, D, LANES - N_EXP), F32)], axis=-1)
    rb = row(jnp.concatenate([router_b, jnp.full((DEPTH, LANES - N_EXP), -1e30, F32)], axis=-1))
    glu_w_bf = glu_w.astype(BF)
    w_out_bf = w_out.astype(BF)
    n1, n2, onw = row(norm1_w), row(norm2_w), row(out_norm_w)
    ssm_d3, glu_b3, pool_sc3 = row(ssm_d), row(glu_b), row(pool_scale)
    sink = attn_sink.reshape(-1)

    tok = jnp.arange(N_TOK, dtype=jnp.int32)
    is_ctx_tok = (tok % LT) < L_CTX

    for l in range(DEPTH):
        last = l == DEPTH - 1
        q, kd, vd, u, p = _inproj(l, xc, mod, n1, w_ext, qw2, kw2, cos2, sin2)
        attn = _attention(l, sink, q, kd, vd, onw)
        yf, yb = _s5_scan(l, u, a_tab, bm, cm)
        pool = _pool(l, p, pw_bd, pool_sc3)
        x1, h2, route, cnt = _mix(l, not last, xc, attn, yf, yb, u, pool, mod, ssm_d3, glu_w_bf, glu_b3, onw,
                                  w_out_bf, n2, rw, rb, tri)

        by_k = lambda lo: route[:, lo:lo + TOP_K, :].transpose(1, 0, 2).reshape(TOP_K, N_TOK)
        top_i = by_k(0).astype(jnp.int32)
        rank = by_k(2 * TOP_K).astype(jnp.int32)
        counts = cnt[:, 0].astype(jnp.int32)
        padded = (counts + TM_MOE - 1) // TM_MOE * TM_MOE
        pend = jnp.cumsum(padded)
        pstart = pend - padded
        experts = jnp.arange(N_EXP, dtype=jnp.int32)
        dest_t = jnp.sum(jnp.where(top_i[..., None] == experts, pstart, 0), axis=-1) + rank
        n_used = (pend[-1] // TM_MOE).astype(jnp.int32)[None]
        tile_lo = jnp.arange(NT_MOE, dtype=jnp.int32) * TM_MOE
        tile_e = jnp.minimum(jnp.sum(pend[None, :] <= tile_lo[:, None], axis=1), N_EXP - 1).astype(jnp.int32)
        change = jnp.concatenate([jnp.ones((1,), jnp.int32), (tile_e[1:] != tile_e[:-1]).astype(jnp.int32)])
        slot = (jnp.cumsum(change) - 1) % 2
        next_first = jnp.sum(jnp.where(tile_e[:, None] == experts, pend // TM_MOE, 0), axis=1)
        tiles = jnp.arange(NT_MOE, dtype=jnp.int32)
        next_e = jnp.sum(jnp.where(next_first[:, None] == tiles, tile_e, 0), axis=1)
        nxt = jnp.where(next_first < n_used[0], next_e, -1)
        h2t = h2.reshape(N_TOK, PACK_ROWS, LANES)
        if last:
            xs = _row_dispatch(h2t, jnp.where(is_ctx_tok, R_MOE + tok % TM_MOE, dest_t), R_MOE + TM_MOE)
            dest_t = jnp.where(is_ctx_tok, tok, dest_t)
        else:
            xs = _row_dispatch(h2t, dest_t, R_MOE)

        yb_rows = _moe(l, tile_e, n_used, slot.astype(jnp.int32), nxt.astype(jnp.int32),
                       xs.reshape(-1, LANES), exp_w_gu, exp_b_gu, exp_w_down, exp_b_down)
        gathered = _row_gather(yb_rows.reshape(R_MOE, PACK_ROWS, LANES),
                               dest_t.reshape(-1)).reshape(-1, LANES)
        xc = _combine(l, x1, route, mod, gathered, latent_only=last)
    return xc
```

```python
import functools
import math

import jax
import jax.numpy as jnp
from jax import lax
from jax.experimental import pallas as pl
from jax.experimental.pallas import tpu as pltpu
from jax.experimental.pallas import tpu_sc as plsc

D = 1024
NB = 8
L_LAT = 2048
L_CTX = 256
LT = L_CTX + L_LAT
DEPTH = 4
N_MOD = 6
EPS = 1e-6
N_HEADS = 8
HEAD_DIM = 64
ATTN_W = 512
WINDOW = 128
ATTN_SCALE = HEAD_DIM ** -0.5
LOG2E = math.log2(math.e)
ROPE_BASE = 10000.0
GRID_W = 64
SSM_W = 256
SSM_GROUP = 16
SSM_GROUPS = 16
SSM_STATE = 64
NS = SSM_GROUPS * SSM_STATE
POOL_W = 256
POOL_GROUP = 64
N_EXP = 32
TOP_K = 4
D_FF = 1024
SWIGLU_LIMIT = 7.0
SWIGLU_ALPHA = 1.702

LANES = 128
SUBLANES = 8
VMEM_LIMIT = 56 * 1024 * 1024

TM_TOK = 768
TQ = 256
KSPAN = TQ + 2 * WINDOW
T_SCAN = 128
N_CHUNK = LT // T_SCAN
N_CHUNK_CTX = L_CTX // T_SCAN
TM_MOE = 512
N_TOK = NB * LT
N_ASSIGN = N_TOK * TOP_K
NT_MOE = N_ASSIGN // TM_MOE + N_EXP
R_MOE = NT_MOE * TM_MOE
FF_CHUNK = 512
OUT_BLOCK = 256

SC_CORES = 2
SC_SUBCORES = 16
SC_WORKERS = SC_CORES * SC_SUBCORES
SC_CHUNK = 32
PACK_ROWS = D // 2 // LANES

C_Q = 0
C_K = 512
C_V = 768
C_U = 1024
C_P = 1280
NW_IN = 1536

BF = jnp.bfloat16
F32 = jnp.float32


def _split_bf16(a):
    hi = a.astype(BF)
    lo = (a - hi.astype(F32)).astype(BF)
    return hi, lo


def _dot(a, b):
    return jnp.dot(a, b, preferred_element_type=F32)


def _dot3(a, b):
    ah, al = _split_bf16(a)
    bh, bl = _split_bf16(b)
    return _dot(ah, bh) + (_dot(ah, bl) + _dot(al, bh))


def _mod_kernel(c_ref, w_ref, b_ref, o_ref):
    c = c_ref[...]
    s = c * jax.nn.sigmoid(c)
    o_ref[0] = _dot3(s, w_ref[0]) + b_ref[0]


def _adaln(cc, w_mod, b_mod):
    tn = 1536
    return pl.pallas_call(
        _mod_kernel,
        out_shape=jax.ShapeDtypeStruct((DEPTH, 16, N_MOD * D), F32),
        grid=(DEPTH, N_MOD * D // tn),
        in_specs=[
            pl.BlockSpec((16, D), lambda l, j: (0, 0)),
            pl.BlockSpec((1, D, tn), lambda l, j: (l, 0, j)),
            pl.BlockSpec((1, 1, tn), lambda l, j: (l, 0, j)),
        ],
        out_specs=pl.BlockSpec((1, 16, tn), lambda l, j: (l, 0, j)),
        compiler_params=pltpu.CompilerParams(
            dimension_semantics=("arbitrary", "arbitrary"), vmem_limit_bytes=VMEM_LIMIT),
        name="adaln_mod",
    )(cc, w_mod, b_mod.reshape(DEPTH, 1, N_MOD * D))


def _pack_rows(ref, val, rows):
    def bits(v):
        return lax.bitcast_convert_type(v.astype(BF).astype(F32), jnp.uint32)

    words = bits(val[:, :D // 2]) | (bits(val[:, D // 2:]) >> 16)
    for j in range(PACK_ROWS):
        ref[pl.ds(j, rows, stride=PACK_ROWS), :] = words[:, j * LANES:(j + 1) * LANES]


def _unpack_rows(ref, rows):
    words = jnp.concatenate([ref[pl.ds(j, rows, stride=PACK_ROWS), :] for j in range(PACK_ROWS)], axis=1)
    return jnp.concatenate([lax.bitcast_convert_type(words & jnp.uint32(0xFFFF0000), F32),
                            lax.bitcast_convert_type(words << 16, F32)], axis=1)


def _layer_spec(l, shape, col_block=0):
    shape = tuple(shape)
    return pl.BlockSpec((None,) + shape, lambda *_: (l,) + (0,) * (len(shape) - 1) + (col_block,))


def _with_mod(fn, mod_ref, b, i, idxs, *arrays):
    lat = [mod_ref[pl.ds(b, 1), k * D:(k + 1) * D] for k in idxs]
    top = [jnp.where(i == 0, mod_ref[NB:NB + 1, k * D:(k + 1) * D], v) for k, v in zip(idxs, lat)]
    return jnp.concatenate([fn(*[a[:L_CTX] for a in arrays], *top),
                            fn(*[a[L_CTX:] for a in arrays], *lat)], axis=0)


def _rmsnorm(x, w):
    ms = jnp.mean(x * x, axis=-1, keepdims=True)
    return x * lax.rsqrt(ms + EPS) * w


def _headnorm_pair(t, w2, lane):
    sq = t * t
    first = lane < HEAD_DIM
    a = jnp.sum(jnp.where(first, sq, 0.0), axis=-1, keepdims=True)
    b = jnp.sum(jnp.where(first, 0.0, sq), axis=-1, keepdims=True)
    ms = jnp.where(first, a, b) * (1.0 / HEAD_DIM)
    return t * lax.rsqrt(ms + EPS) * w2


def _rope_pair(t, cos, sin_signed, lane):
    lower = (lane % 32) < 16
    partner = jnp.where(lower, pltpu.roll(t, LANES - 16, 1), pltpu.roll(t, 16, 1))
    return t * cos + partner * sin_signed


def _inproj_kernel(x_ref, mod_ref, n1_ref, w_ref, qw_ref, kw_ref, cos_ref, sin_ref,
                   q_ref, k_ref, v_ref, u_ref, p_ref):
    b = pl.program_id(0)
    i = pl.program_id(1)
    h = _rmsnorm(x_ref[0], n1_ref[...])
    h = _with_mod(lambda hr, shift, scale: (hr * (1.0 + scale) + shift).astype(BF), mod_ref, b, i, (0, 1), h)
    px = _dot(h, w_ref[...])
    lane = lax.broadcasted_iota(jnp.int32, (TM_TOK, LANES), 1)
    cos = cos_ref[...]
    sin = sin_ref[...]
    for j in range(ATTN_W // LANES):
        t = px[:, C_Q + j * LANES:C_Q + (j + 1) * LANES]
        t = _rope_pair(_headnorm_pair(t, qw_ref[...], lane), cos, sin, lane) * (ATTN_SCALE * LOG2E)
        q_ref[0, :, j * LANES:(j + 1) * LANES] = t.astype(BF)
    for g in range(2):
        t = px[:, C_K + g * LANES:C_K + (g + 1) * LANES]
        t = _rope_pair(_headnorm_pair(t, kw_ref[...], lane), cos, sin, lane)
        k_ref[0, :, g * LANES:(g + 1) * LANES] = t.astype(BF)
        t = px[:, C_V + g * LANES:C_V + (g + 1) * LANES]
        v_ref[0, :, g * LANES:(g + 1) * LANES] = jnp.where(lane < HEAD_DIM, t, 1.0).astype(BF)
    u_ref[0] = px[:, C_U:C_P]
    p_ref[0] = px[:, C_P:NW_IN]


def _inproj(l, xc, mod, n1, w_ext, qw2, kw2, cos2, sin2):
    tok = lambda w: pl.BlockSpec((1, TM_TOK, w), lambda b, i: (b, i, 0))
    full = lambda s: _layer_spec(l, s)
    return pl.pallas_call(
        _inproj_kernel,
        out_shape=(
            jax.ShapeDtypeStruct((NB, LT, ATTN_W), BF),
            jax.ShapeDtypeStruct((NB, LT, 256), BF),
            jax.ShapeDtypeStruct((NB, LT, 256), BF),
            jax.ShapeDtypeStruct((NB, LT, SSM_W), F32),
            jax.ShapeDtypeStruct((NB, LT, POOL_W), F32),
        ),
        grid=(NB, LT // TM_TOK),
        in_specs=[
            tok(D), full((16, N_MOD * D)), full((1, D)), full((D, NW_IN)),
            full((1, LANES)), full((1, LANES)),
            pl.BlockSpec((TM_TOK, LANES), lambda b, i: (i, 0)),
            pl.BlockSpec((TM_TOK, LANES), lambda b, i: (i, 0)),
        ],
        out_specs=(tok(ATTN_W), tok(256), tok(256), tok(SSM_W), tok(POOL_W)),
        compiler_params=pltpu.CompilerParams(
            dimension_semantics=("arbitrary", "arbitrary"), vmem_limit_bytes=VMEM_LIMIT),
        name="inproj",
    )(xc, mod, n1, w_ext, qw2, kw2, cos2, sin2)


def _attn_kernel(l, sink_ref, q_ref, k_ref, v_ref, onw_ref, o_ref, bias_ref):
    j = pl.program_id(1)
    start = pl.multiple_of(jnp.clip(j * TQ - WINDOW, LANES, LT - KSPAN), LANES)
    lane = lax.broadcasted_iota(jnp.int32, (TQ, LANES), 1)
    first = lane < HEAD_DIM
    row2 = lax.broadcasted_iota(jnp.int32, (2 * TQ, 1), 0)
    qpos = j * TQ + jnp.where(row2 < TQ, row2, row2 - TQ)
    kpos = start + lax.broadcasted_iota(jnp.int32, (1, KSPAN), 1)
    valid = (jnp.abs(qpos - kpos) <= WINDOW) & (kpos >= L_CTX) & (j >= 1)
    bias_ref[...] = jnp.where(valid, 0.0, -1e30)
    n_pairs = N_HEADS // 2

    def kv(ref, hp, rows):
        g = hp // 2
        return ref[0, rows, g * LANES:(g + 1) * LANES]

    def scores(hp):
        q2 = q_ref[0, :, hp * LANES:(hp + 1) * LANES]
        zero = jnp.zeros_like(q2)
        qs = jnp.concatenate([jnp.where(first, q2, zero), jnp.where(first, zero, q2)], axis=0)
        dn = (((1,), (1,)), ((), ()))
        s_ctx = lax.dot_general(qs, kv(k_ref, hp, slice(0, L_CTX)), dn, preferred_element_type=F32)
        s_loc = lax.dot_general(qs, kv(k_ref, hp, pl.ds(start, KSPAN)), dn, preferred_element_type=F32)
        return s_loc, s_ctx

    nxt = scores(0)
    outs = []
    for hp in range(n_pairs):
        s_loc, s_ctx = nxt
        if hp + 1 < n_pairs:
            nxt = scores(hp + 1)
        s_loc = s_loc + bias_ref[...]
        sink = jnp.where(row2 < TQ, sink_ref[l * N_HEADS + 2 * hp],
                         sink_ref[l * N_HEADS + 2 * hp + 1]) * LOG2E
        m = jnp.maximum(jnp.maximum(jnp.max(s_loc, axis=-1, keepdims=True),
                                    jnp.max(s_ctx, axis=-1, keepdims=True)), sink)
        e_loc = jnp.exp2(s_loc - m)
        e_ctx = jnp.exp2(s_ctx - m)
        o = (_dot(e_loc.astype(BF), kv(v_ref, hp, pl.ds(start, KSPAN)))
             + _dot(e_ctx.astype(BF), kv(v_ref, hp, slice(0, L_CTX))))
        den = pltpu.roll(o + jnp.exp2(sink - m), HEAD_DIM, 1)
        o = o / den
        outs.append(jnp.where(first, o[0:TQ], pltpu.roll(o[TQ:2 * TQ], HEAD_DIM, 1)))
    o_ref[0] = _rmsnorm(jnp.concatenate(outs, axis=1), onw_ref[...]).astype(BF)


def _attention(l, sink, q, kd, vd, onw):
    return pl.pallas_call(
        functools.partial(_attn_kernel, l),
        out_shape=jax.ShapeDtypeStruct((NB, LT, ATTN_W), BF),
        grid_spec=pltpu.PrefetchScalarGridSpec(
            num_scalar_prefetch=1,
            grid=(NB, LT // TQ),
            in_specs=[
                pl.BlockSpec((1, TQ, ATTN_W), lambda b, j, s: (b, j, 0)),
                pl.BlockSpec((1, LT, 256), lambda b, j, s: (b, 0, 0)),
                pl.BlockSpec((1, LT, 256), lambda b, j, s: (b, 0, 0)),
                _layer_spec(l, (1, ATTN_W)),
            ],
            out_specs=pl.BlockSpec((1, TQ, ATTN_W), lambda b, j, s: (b, j, 0)),
            scratch_shapes=[pltpu.VMEM((2 * TQ, KSPAN), F32)],
        ),
        compiler_params=pltpu.CompilerParams(
            dimension_semantics=("arbitrary", "arbitrary"), vmem_limit_bytes=VMEM_LIMIT),
        name="window_attn",
    )(sink, q, kd, vd, onw)


def _bwd_chunk(i):
    return jnp.where(i < N_CHUNK_CTX, N_CHUNK_CTX - 1 - i, N_CHUNK - 1 - (i - N_CHUNK_CTX))


def _scan_kernel(uf_ref, ub_ref, a_ref, bm_ref, cm_ref, yf_ref, yb_ref,
                 xf_ref, xb_ref, st_ref, *il_refs):
    i = pl.program_id(0)
    halves = SSM_W // LANES

    @pl.when(i == 0)
    def _():
        st_ref[...] = jnp.zeros_like(st_ref)

    def project(d, u_ref, xs_ref):
        il = il_refs[d * halves:(d + 1) * halves]
        for h in range(halves):
            for bb in range(NB):
                il[h][pl.ds(bb, T_SCAN, stride=NB), :] = u_ref[bb, :, h * LANES:(h + 1) * LANES]
        ui = jnp.concatenate([r[...] for r in il], axis=1).astype(BF)
        xs_ref[...] = _dot(ui, bm_ref[d])

    def scan(d, xs_ref, reverse):
        ar = jnp.broadcast_to(a_ref[d, 0:1, :], (NB, NS))
        ai = jnp.broadcast_to(a_ref[d, 1:2, :], (NB, NS))
        sr = st_ref[d, 0]
        si = st_ref[d, 1]
        for t in (range(T_SCAN - 1, -1, -1) if reverse else range(T_SCAN)):
            r = pl.ds(t * NB, NB)
            nr = ar * sr - ai * si + xs_ref[r, 0:NS]
            ni = ar * si + ai * sr + xs_ref[r, NS:2 * NS]
            sr, si = nr, ni
            xs_ref[r, 0:NS] = sr
            xs_ref[r, NS:2 * NS] = si
        st_ref[d, 0] = sr
        st_ref[d, 1] = si

    def readout(d, xs_ref, y_ref):
        y = _dot(xs_ref[...].astype(BF), cm_ref[d])
        il = il_refs[d * halves:(d + 1) * halves]
        for h in range(halves):
            il[h][...] = y[:, h * LANES:(h + 1) * LANES]
            for bb in range(NB):
                y_ref[bb, :, h * LANES:(h + 1) * LANES] = il[h][pl.ds(bb, T_SCAN, stride=NB), :]

    project(0, uf_ref, xf_ref)
    project(1, ub_ref, xb_ref)
    scan(0, xf_ref, False)
    readout(0, xf_ref, yf_ref)
    scan(1, xb_ref, True)
    readout(1, xb_ref, yb_ref)


def _s5_scan(l, u, a_tab, bm, cm):
    rows = NB * T_SCAN
    layer = lambda s: _layer_spec(l, s)
    chunk_f = pl.BlockSpec((NB, T_SCAN, SSM_W), lambda i: (0, i, 0))
    chunk_b = pl.BlockSpec((NB, T_SCAN, SSM_W), lambda i: (0, _bwd_chunk(i), 0))
    return pl.pallas_call(
        _scan_kernel,
        out_shape=(jax.ShapeDtypeStruct((NB, LT, SSM_W), F32),
                   jax.ShapeDtypeStruct((NB, LT, SSM_W), F32)),
        grid=(N_CHUNK,),
        in_specs=[chunk_f, chunk_b, layer((2, 2, NS)), layer((2, SSM_W, 2 * NS)), layer((2, 2 * NS, SSM_W))],
        out_specs=(chunk_f, chunk_b),
        scratch_shapes=[pltpu.VMEM((rows, 2 * NS), F32), pltpu.VMEM((rows, 2 * NS), F32),
                        pltpu.VMEM((2, 2, NB, NS), F32),
                        ] + [pltpu.VMEM((rows, LANES), F32)] * (2 * (SSM_W // LANES)),
        compiler_params=pltpu.CompilerParams(
            dimension_semantics=("arbitrary",), vmem_limit_bytes=VMEM_LIMIT),
        name="s5_scan",
    )(u, u, a_tab, bm, cm)


def _pool_segment(ps, length):
    n = length + 2 * SUBLANES
    z = jnp.zeros((SUBLANES, POOL_W), F32)
    pe = jnp.concatenate([z, ps, z], axis=0)
    a1 = pe + pltpu.roll(pe, 1, 0)
    a2 = a1 + pltpu.roll(a1, 2, 0)
    a3 = a2 + pltpu.roll(a2, 4, 0)
    a4 = a3 + pltpu.roll(a3, 8, 0)
    lane = lax.broadcasted_iota(jnp.int32, (1, POOL_W), 1)
    half = jnp.where(lane < 64, 1, jnp.where(lane < 128, 2, jnp.where(lane < 192, 4, 8)))
    s = jnp.where(lane < 64, a1,
                  jnp.where(lane < 128, pltpu.roll(a2, n - 1, 0),
                            jnp.where(lane < 192, pltpu.roll(a3, n - 3, 0), pltpu.roll(a4, n - 7, 0))))
    s = s[SUBLANES:SUBLANES + length]
    t = lax.broadcasted_iota(jnp.int32, (length, 1), 0)
    cnt = jnp.minimum(t + half, length) - jnp.maximum(t - half, 0)
    return s / cnt.astype(F32) - ps


def _pool_kernel(p_ref, w_ref, sc_ref, o_ref):
    for lo, length in ((0, L_CTX), (L_CTX, L_LAT)):
        dlt = _pool_segment(p_ref[0, lo:lo + length, :], length)
        y = _dot(dlt.astype(BF), w_ref[...]) * sc_ref[...]
        o_ref[0, lo:lo + length, :] = y.astype(BF)


def _pool(l, p, w_bd, scale):
    return pl.pallas_call(
        _pool_kernel,
        out_shape=jax.ShapeDtypeStruct((NB, LT, POOL_W), BF),
        grid=(NB,),
        in_specs=[pl.BlockSpec((1, LT, POOL_W), lambda b: (b, 0, 0)),
                  _layer_spec(l, (POOL_W, POOL_W)),
                  _layer_spec(l, (1, POOL_W))],
        out_specs=pl.BlockSpec((1, LT, POOL_W), lambda b: (b, 0, 0)),
        compiler_params=pltpu.CompilerParams(
            dimension_semantics=("arbitrary",), vmem_limit_bytes=VMEM_LIMIT),
        name="pool",
    )(p, w_bd, scale)


def _mix_kernel(route_ctx, x_ref, at_ref, yf_ref, yb_ref, u_ref, pl_ref, mod_ref, d_ref, gw_ref, gb_ref, onw_ref,
                wo_ref, n2_ref, rw_ref, rb_ref, tri_ref,
                x1_ref, h2_ref, rt_ref, cnt_ref, run_ref):
    b = pl.program_id(0)
    i = pl.program_id(1)

    @pl.when((b == 0) & (i == 0))
    def _():
        run_ref[...] = jnp.zeros_like(run_ref)

    y = yf_ref[0] + yb_ref[0] + d_ref[...] * u_ref[0]
    g = jax.nn.gelu(y, approximate=True)
    s = g * jax.nn.sigmoid(_dot(g.astype(BF), gw_ref[...]) + gb_ref[...])
    s = _rmsnorm(s, onw_ref[...]).astype(BF)
    mix = (_dot(at_ref[0], wo_ref[0:ATTN_W, :])
           + _dot(s, wo_ref[ATTN_W:ATTN_W + SSM_W, :])
           + _dot(pl_ref[0], wo_ref[ATTN_W + SSM_W:D, :]))
    x1 = _with_mod(lambda xr, mr, gate: xr + gate * mr, mod_ref, b, i, (2,), x_ref[0], mix)
    x1_ref[0] = x1
    h2 = _rmsnorm(x1, n2_ref[...])
    h2 = _with_mod(lambda hr, shift, scale: hr * (1.0 + scale) + shift, mod_ref, b, i, (3, 4), h2)
    _pack_rows(h2_ref, h2, TM_TOK)
    logits = (_dot3(h2, rw_ref[...]) + rb_ref[...]).T[0:N_EXP]
    eidx = lax.broadcasted_iota(jnp.int32, (N_EXP, TM_TOK), 0).astype(F32)
    vals, idxs, hits = [], [], []
    cur = logits
    for _ in range(TOP_K):
        m = jnp.max(cur, axis=0, keepdims=True)
        idx = jnp.min(jnp.where(cur == m, eidx, float(N_EXP)), axis=0, keepdims=True)
        hit = eidx == idx
        vals.append(m)
        idxs.append(idx)
        hits.append(hit)
        cur = jnp.where(hit, -jnp.inf, cur)
    ex = [jnp.exp(v - vals[0]) for v in vals]
    den = ex[0] + ex[1] + ex[2] + ex[3]
    onehot = jnp.where(hits[0] | hits[1] | hits[2] | hits[3], 1.0, 0.0)
    if not route_ctx:
        tok = lax.broadcasted_iota(jnp.int32, (1, TM_TOK), 1)
        onehot = jnp.where((tok >= L_CTX) | (i > 0), onehot, 0.0)
    run = run_ref[:, 0:1]
    before = _dot(onehot.astype(BF), tri_ref[...]) + run
    row = lax.broadcasted_iota(jnp.int32, (4 * TOP_K, TM_TOK), 0)
    route = jnp.zeros((4 * TOP_K, TM_TOK), F32)
    for k in range(TOP_K):
        rank = jnp.sum(jnp.where(hits[k], before, 0.0), axis=0, keepdims=True)
        route = jnp.where(row == k, idxs[k], route)
        route = jnp.where(row == TOP_K + k, ex[k] / den, route)
        route = jnp.where(row == 2 * TOP_K + k, rank, route)
    rt_ref[...] = route
    run_ref[...] = jnp.broadcast_to(run + jnp.sum(onehot, axis=1, keepdims=True), (N_EXP, LANES))
    cnt_ref[...] = run_ref[...]


def _mix(l, route_ctx, xc, attn, yf, yb, u, pool, mod, ssm_d, glu_w, glu_b, onw, w_out, n2, rw, rb, tri):
    tok = lambda w: pl.BlockSpec((1, TM_TOK, w), lambda b, i: (b, i, 0))
    full = lambda s: pl.BlockSpec(s, lambda b, i: (0,) * len(s))
    layer = lambda s, col=0: _layer_spec(l, s, col)
    return pl.pallas_call(
        functools.partial(_mix_kernel, route_ctx),
        out_shape=(jax.ShapeDtypeStruct((NB, LT, D), F32),
                   jax.ShapeDtypeStruct((N_TOK * PACK_ROWS, LANES), jnp.uint32),
                   jax.ShapeDtypeStruct((4 * TOP_K, N_TOK), F32),
                   jax.ShapeDtypeStruct((N_EXP, LANES), F32)),
        grid=(NB, LT // TM_TOK),
        in_specs=[tok(D), tok(ATTN_W), tok(SSM_W), tok(SSM_W), tok(SSM_W), tok(POOL_W),
                  layer((16, N_MOD * D)), layer((1, SSM_W)), layer((SSM_W, SSM_W)), layer((1, SSM_W)),
                  layer((1, SSM_W), ATTN_W // SSM_W), layer((D, D)), layer((1, D)), layer((D, LANES)),
                  layer((1, LANES)), full((TM_TOK, TM_TOK))],
        out_specs=(tok(D),
                   pl.BlockSpec((TM_TOK * PACK_ROWS, LANES), lambda b, i: (b * (LT // TM_TOK) + i, 0)),
                   pl.BlockSpec((4 * TOP_K, TM_TOK), lambda b, i: (0, b * (LT // TM_TOK) + i)),
                   full((N_EXP, LANES))),
        scratch_shapes=[pltpu.VMEM((N_EXP, LANES), F32)],
        compiler_params=pltpu.CompilerParams(
            dimension_semantics=("arbitrary", "arbitrary"), vmem_limit_bytes=VMEM_LIMIT),
        name="mix_router",
    )(xc, attn, yf, yb, u, pool, mod, ssm_d, glu_w, glu_b, onw, w_out, n2, rw, rb, tri)


def _moe_kernel(l, te_ref, nu_ref, slot_ref, nxt_ref, x_ref, wgu_hbm, bgu_ref, wd_hbm, bd_ref, o_ref,
                wgu_f, wd_f, wgu_s, wd_s, act_s, sem):
    i = pl.program_id(0)

    @pl.when(i >= nu_ref[0])
    def _():
        o_ref[...] = jnp.zeros_like(o_ref)

    def fetch(e, slot):
        return (pltpu.make_async_copy(wgu_hbm.at[l, e], wgu_f.at[slot], sem.at[0, slot]),
                pltpu.make_async_copy(wd_hbm.at[l, e], wd_f.at[slot], sem.at[1, slot]))

    @pl.when(i < nu_ref[0])
    def _():
        e = te_ref[i]
        prev = te_ref[jnp.maximum(i - 1, 0)]
        slot = slot_ref[i]

        @pl.when(i == 0)
        def _():
            for cp in fetch(e, slot):
                cp.start()

        @pl.when((i == 0) | (e != prev))
        def _():
            for cp in fetch(e, slot):
                cp.wait()
            nxt = nxt_ref[i]

            @pl.when(nxt >= 0)
            def _():
                for cp in fetch(nxt, 1 - slot):
                    cp.start()

            def cast_gu(r, c):
                rs = pl.ds(pl.multiple_of(r * 128, 128), 128)
                wgu_s[rs, :] = wgu_f[slot, rs, :].astype(BF)
                return c

            def cast_d(r, c):
                rs = pl.ds(pl.multiple_of(r * 128, 128), 128)
                wd_s[rs, :] = wd_f[slot, rs, :].astype(BF)
                return c

            lax.fori_loop(0, D // 128, cast_gu, 0)
            lax.fori_loop(0, D_FF // 128, cast_d, 0)

        x = _unpack_rows(x_ref, TM_MOE).astype(BF)
        for c in range(D_FF // FF_CHUNK):
            lo = c * FF_CHUNK
            gate = _dot(x, wgu_s[:, lo:lo + FF_CHUNK]) + bgu_ref[:, lo:lo + FF_CHUNK]
            up = (_dot(x, wgu_s[:, D_FF + lo:D_FF + lo + FF_CHUNK])
                  + bgu_ref[:, D_FF + lo:D_FF + lo + FF_CHUNK])
            gate = jnp.minimum(gate, SWIGLU_LIMIT)
            up = jnp.clip(up, -SWIGLU_LIMIT, SWIGLU_LIMIT)
            act = (up + 1.0) * (gate * jax.nn.sigmoid(SWIGLU_ALPHA * gate))
            act_s[:, lo:lo + FF_CHUNK] = act.astype(BF)
        def bf16_bits(v):
            return lax.bitcast_convert_type(v.astype(BF).astype(F32), jnp.uint32)

        for n in range(D // 2 // OUT_BLOCK):
            lo = n * OUT_BLOCK
            y_hi = _dot(act_s[...], wd_s[:, lo:lo + OUT_BLOCK]) + bd_ref[:, lo:lo + OUT_BLOCK]
            y_lo = (_dot(act_s[...], wd_s[:, D // 2 + lo:D // 2 + lo + OUT_BLOCK])
                    + bd_ref[:, D // 2 + lo:D // 2 + lo + OUT_BLOCK])
            words = bf16_bits(y_hi) | (bf16_bits(y_lo) >> 16)
            for j in range(OUT_BLOCK // LANES):
                o_ref[pl.ds(lo // LANES + j, TM_MOE, stride=PACK_ROWS), :] = words[:, j * LANES:(j + 1) * LANES]


def _moe(l, tile_e, n_used, slot, nxt, xs, w_gu, b_gu, w_down, b_down):
    def tile(i, te, nu, *_):
        return (jnp.minimum(i, nu[0] - 1), 0)

    def bias(i, te, nu, *_):
        return (l, te[jnp.minimum(i, nu[0] - 1)], 0, 0)

    return pl.pallas_call(
        functools.partial(_moe_kernel, l),
        out_shape=jax.ShapeDtypeStruct((R_MOE * PACK_ROWS, LANES), jnp.uint32),
        grid_spec=pltpu.PrefetchScalarGridSpec(
            num_scalar_prefetch=4,
            grid=(NT_MOE,),
            in_specs=[
                pl.BlockSpec((TM_MOE * PACK_ROWS, LANES), tile),
                pl.BlockSpec(memory_space=pl.ANY),
                pl.BlockSpec((None, None, 1, 2 * D_FF), bias),
                pl.BlockSpec(memory_space=pl.ANY),
                pl.BlockSpec((None, None, 1, D), bias),
            ],
            out_specs=pl.BlockSpec((TM_MOE * PACK_ROWS, LANES), lambda i, *_: (i, 0)),
            scratch_shapes=[pltpu.VMEM((2, D, 2 * D_FF), F32), pltpu.VMEM((2, D_FF, D), F32),
                            pltpu.VMEM((D, 2 * D_FF), BF), pltpu.VMEM((D_FF, D), BF),
                            pltpu.VMEM((TM_MOE, D_FF), BF), pltpu.SemaphoreType.DMA((2, 2))],
        ),
        compiler_params=pltpu.CompilerParams(
            dimension_semantics=("arbitrary",), vmem_limit_bytes=VMEM_LIMIT),
        name="moe_experts",
    )(tile_e, n_used, slot, nxt, xs, w_gu, b_gu.reshape(DEPTH, N_EXP, 1, 2 * D_FF), w_down,
      b_down.reshape(DEPTH, N_EXP, 1, D))


def _row_gather(table, idx):
    n_rows = idx.shape[0]
    per_worker = n_rows // SC_WORKERS
    n_chunks = per_worker // SC_CHUNK
    assert per_worker * SC_WORKERS == n_rows and n_chunks * SC_CHUNK == per_worker and n_chunks % 2 == 0
    mesh = plsc.VectorSubcoreMesh(core_axis_name="c", subcore_axis_name="s")
    row_tile = (SC_CHUNK,) + table.shape[1:]

    @functools.partial(
        pl.kernel, mesh=mesh,
        out_type=jax.ShapeDtypeStruct((n_rows,) + table.shape[1:], table.dtype),
        scratch_types=[pltpu.VMEM((n_chunks, SC_CHUNK), jnp.int32),
                       pltpu.VMEM(row_tile, table.dtype), pltpu.VMEM(row_tile, table.dtype),
                       pltpu.SemaphoreType.DMA, pltpu.SemaphoreType.DMA],
        name="sc_row_gather",
    )
    def gather(table_hbm, idx_hbm, out_hbm, idx_v, buf0, buf1, sem0, sem1):
        wid = lax.axis_index("s") * SC_CORES + lax.axis_index("c")
        pltpu.sync_copy(idx_hbm.at[pl.ds(wid * n_chunks, n_chunks)], idx_v)
        base = wid * per_worker

        def fetch(chunk, buf, sem):
            return pltpu.make_async_copy(table_hbm.at[idx_v.at[chunk]], buf, sem)

        def emit(chunk, buf):
            pltpu.sync_copy(buf, out_hbm.at[pl.ds(base + chunk * SC_CHUNK, SC_CHUNK)])

        fetch(0, buf0, sem0).start()

        @pl.loop(0, n_chunks, step=2)
        def _(c):
            fetch(c + 1, buf1, sem1).start()
            fetch(c, buf0, sem0).wait()
            emit(c, buf0)

            @pl.when(c + 2 < n_chunks)
            def _():
                fetch(c + 2, buf0, sem0).start()

            fetch(c + 1, buf1, sem1).wait()
            emit(c + 1, buf1)

    return gather(table, idx.reshape(n_rows // SC_CHUNK, SC_CHUNK))


def _row_dispatch(rows, dest_t, n_out):
    per_worker = N_TOK // SC_WORKERS
    n_chunks = per_worker // SC_CHUNK
    assert per_worker * SC_WORKERS == N_TOK and n_chunks * SC_CHUNK == per_worker and n_chunks % 2 == 0
    idx_rows = TOP_K * n_chunks
    assert idx_rows % SUBLANES == 0
    mesh = plsc.VectorSubcoreMesh(core_axis_name="c", subcore_axis_name="s")
    row_tile = (SC_CHUNK,) + rows.shape[1:]

    @functools.partial(
        pl.kernel, mesh=mesh,
        out_type=jax.ShapeDtypeStruct((n_out,) + rows.shape[1:], rows.dtype),
        scratch_types=[pltpu.VMEM((idx_rows, SC_CHUNK), jnp.int32),
                       pltpu.VMEM(row_tile, rows.dtype), pltpu.VMEM(row_tile, rows.dtype),
                       pltpu.SemaphoreType.DMA, pltpu.SemaphoreType.DMA],
        name="sc_row_dispatch",
    )
    def dispatch(rows_hbm, idx_hbm, out_hbm, idx_v, buf0, buf1, sem0, sem1):
        wid = lax.axis_index("s") * SC_CORES + lax.axis_index("c")
        pltpu.sync_copy(idx_hbm.at[pl.ds(wid * idx_rows, idx_rows)], idx_v)
        base = wid * per_worker

        def fetch(chunk, buf, sem):
            return pltpu.make_async_copy(rows_hbm.at[pl.ds(base + chunk * SC_CHUNK, SC_CHUNK)], buf, sem)

        def emit(chunk, buf):
            for k in range(TOP_K):
                pltpu.sync_copy(buf, out_hbm.at[idx_v.at[k * n_chunks + chunk]])

        fetch(0, buf0, sem0).start()

        @pl.loop(0, n_chunks, step=2)
        def _(c):
            fetch(c + 1, buf1, sem1).start()
            fetch(c, buf0, sem0).wait()
            emit(c, buf0)

            @pl.when(c + 2 < n_chunks)
            def _():
                fetch(c + 2, buf0, sem0).start()

            fetch(c + 1, buf1, sem1).wait()
            emit(c + 1, buf1)

    idx = dest_t.reshape(TOP_K, SC_WORKERS, n_chunks, SC_CHUNK).transpose(1, 0, 2, 3)
    return dispatch(rows, idx.reshape(SC_WORKERS * idx_rows, SC_CHUNK))


def _combine_kernel(first_tile, x1_ref, rt_ref, mod_ref, g0_ref, g1_ref, g2_ref, g3_ref, o_ref):
    t = pl.program_id(0)
    per_b = LT // TQ - first_tile
    b = t // per_b
    is_ctx = (t % per_b + first_tile) == 0
    route = rt_ref[...].T
    y = jnp.zeros((TQ, D), F32)
    for k, g_ref in enumerate((g0_ref, g1_ref, g2_ref, g3_ref)):
        y = y + route[:, TOP_K + k:TOP_K + k + 1] * _unpack_rows(g_ref, TQ)
    gate = jnp.where(is_ctx, mod_ref[NB:NB + 1, 5 * D:6 * D], mod_ref[pl.ds(b, 1), 5 * D:6 * D])
    o_ref[0] = x1_ref[0] + gate * y


def _combine(l, x1, route, mod, gathered, latent_only):
    tiles_b = LT // TQ
    first_tile = L_CTX // TQ if latent_only else 0
    per_b = tiles_b - first_tile
    tile_of = lambda t: (t // per_b, t % per_b + first_tile)
    tok = lambda w: pl.BlockSpec((1, TQ, w), lambda t: tile_of(t) + (0,))
    slot = lambda k: pl.BlockSpec(
        (TQ * PACK_ROWS, LANES),
        lambda t: (k * NB * tiles_b + tile_of(t)[0] * tiles_b + tile_of(t)[1], 0))
    return pl.pallas_call(
        functools.partial(_combine_kernel, first_tile),
        out_shape=jax.ShapeDtypeStruct((NB, per_b * TQ, D), F32),
        grid=(NB * per_b,),
        in_specs=[tok(D),
                  pl.BlockSpec((4 * TOP_K, TQ), lambda t: (0, tile_of(t)[0] * tiles_b + tile_of(t)[1])),
                  _layer_spec(l, (16, N_MOD * D)),
                  slot(0), slot(1), slot(2), slot(3)],
        out_specs=pl.BlockSpec((1, TQ, D), lambda t: (t // per_b, t % per_b, 0)),
        compiler_params=pltpu.CompilerParams(
            dimension_semantics=("arbitrary",), vmem_limit_bytes=VMEM_LIMIT),
        name="moe_combine",
    )(x1, route, mod, gathered, gathered, gathered, gathered)


def _rope_tables():
    pos = jnp.arange(L_LAT)
    row = (pos // GRID_W).astype(F32)
    col = (pos % GRID_W).astype(F32)
    inv = ROPE_BASE ** (-jnp.arange(0, 32, 2, dtype=F32) / 32)
    ang_r = row[:, None] * inv
    ang_c = col[:, None] * inv
    ang = jnp.concatenate([ang_r, ang_r, ang_c, ang_c], axis=-1)
    cos = jnp.concatenate([jnp.ones((L_CTX, HEAD_DIM), F32), jnp.cos(ang)], axis=0)
    sin = jnp.concatenate([jnp.zeros((L_CTX, HEAD_DIM), F32), jnp.sin(ang)], axis=0)
    sign = jnp.where((jnp.arange(HEAD_DIM) % 32) < 16, -1.0, 1.0).astype(F32)
    return jnp.tile(cos, (1, 2)), jnp.tile(sin * sign, (1, 2))


def _s5_tables(a_re, a_im, log_dt, b_re, b_im, c_re, c_im):
    dt = jnp.exp(log_dt)[..., None]
    mag = jnp.exp(a_re * dt)
    ar = mag * jnp.cos(a_im * dt)
    ai = mag * jnp.sin(a_im * dt)
    den = a_re * a_re + a_im * a_im
    qr = ((ar - 1) * a_re + ai * a_im) / den
    qi = (ai * a_re - (ar - 1) * a_im) / den
    bbr = qr[..., None] * b_re - qi[..., None] * b_im
    bbi = qr[..., None] * b_im + qi[..., None] * b_re
    eye = jnp.eye(SSM_GROUPS, dtype=F32)
    to_in = lambda m: jnp.einsum('ldgnp,gh->ldgphn', m, eye).reshape(DEPTH, 2, SSM_W, NS)
    bm = jnp.concatenate([to_in(bbr), to_in(bbi)], axis=-1).astype(BF)
    to_out = lambda m: jnp.einsum('ldgpn,gh->ldgnhp', m, eye).reshape(DEPTH, 2, NS, SSM_W)
    cm = jnp.concatenate([to_out(c_re), -to_out(c_im)], axis=2).astype(BF)
    a_tab = jnp.stack([ar.reshape(DEPTH, 2, NS), ai.reshape(DEPTH, 2, NS)], axis=2)
    return a_tab, bm, cm


def kernel(x, c, ctx, c_ctx, w_mod, b_mod, norm1_w, norm2_w, w_in, q_norm_w, k_norm_w, attn_sink,
           ssm_a_re, ssm_a_im, ssm_log_dt, ssm_b_re, ssm_b_im, ssm_c_re, ssm_c_im, ssm_d, glu_w, glu_b,
           pool_w, pool_scale, out_norm_w, w_out, router_w, router_b, exp_w_gu, exp_b_gu, exp_w_down,
           exp_b_down):
    xc = jnp.concatenate([ctx, x], axis=1)
    cc = jnp.concatenate([c, c_ctx[None, :], jnp.zeros((16 - NB - 1, D), F32)], axis=0)
    mod = _adaln(cc, w_mod, b_mod)
    cos2, sin2 = _rope_tables()
    tri = (jnp.arange(TM_TOK)[:, None] < jnp.arange(TM_TOK)[None, :]).astype(BF)

    row = lambda a: a.reshape(DEPTH, 1, -1)
    dup = lambda m: jnp.concatenate([m[..., :64], m[..., :64], m[..., 64:], m[..., 64:]], axis=-1)
    w_ext = jnp.concatenate([w_in[..., :512], dup(w_in[..., 512:640]), dup(w_in[..., 640:768]),
                             w_in[..., 768:]], axis=-1).astype(BF)
    qw2 = row(jnp.tile(q_norm_w, (1, 2)))
    kw2 = row(jnp.tile(k_norm_w, (1, 2)))
    a_tab, bm, cm = _s5_tables(ssm_a_re, ssm_a_im, ssm_log_dt, ssm_b_re, ssm_b_im, ssm_c_re, ssm_c_im)
    pw_bd = jnp.einsum('lgcd,gh->lgchd', pool_w, jnp.eye(4, dtype=F32)).reshape(DEPTH, POOL_W, POOL_W).astype(BF)
    rw = jnp.concatenate([router_w, jnp.zeros((DEPTH, D, LANES - N_EXP), F32)], axis=-1)
    rb = row(jnp.concatenate([router_b, jnp.full((DEPTH, LANES - N_EXP), -1e30, F32)], axis=-1))
    glu_w_bf = glu_w.astype(BF)
    w_out_bf = w_out.astype(BF)
    n1, n2, onw = row(norm1_w), row(norm2_w), row(out_norm_w)
    ssm_d3, glu_b3, pool_sc3 = row(ssm_d), row(glu_b), row(pool_scale)
    sink = attn_sink.reshape(-1)

    tok = jnp.arange(N_TOK, dtype=jnp.int32)
    is_ctx_tok = (tok % LT) < L_CTX

    for l in range(DEPTH):
        last = l == DEPTH - 1
        q, kd, vd, u, p = _inproj(l, xc, mod, n1, w_ext, qw2, kw2, cos2, sin2)
        attn = _attention(l, sink, q, kd, vd, onw)
        yf, yb = _s5_scan(l, u, a_tab, bm, cm)
        pool = _pool(l, p, pw_bd, pool_sc3)
        x1, h2, route, cnt = _mix(l, not last, xc, attn, yf, yb, u, pool, mod, ssm_d3, glu_w_bf, glu_b3, onw,
                                  w_out_bf, n2, rw, rb, tri)

        top_i = route[0:TOP_K].astype(jnp.int32)
        rank = route[2 * TOP_K:3 * TOP_K].astype(jnp.int32)
        counts = cnt[:, 0].astype(jnp.int32)
        padded = (counts + TM_MOE - 1) // TM_MOE * TM_MOE
        pend = jnp.cumsum(padded)
        pstart = pend - padded
        experts = jnp.arange(N_EXP, dtype=jnp.int32)
        dest_t = jnp.sum(jnp.where(top_i[..., None] == experts, pstart, 0), axis=-1) + rank
        n_used = (pend[-1] // TM_MOE).astype(jnp.int32)[None]
        tile_lo = jnp.arange(NT_MOE, dtype=jnp.int32) * TM_MOE
        tile_e = jnp.minimum(jnp.sum(pend[None, :] <= tile_lo[:, None], axis=1), N_EXP - 1).astype(jnp.int32)
        change = jnp.concatenate([jnp.ones((1,), jnp.int32), (tile_e[1:] != tile_e[:-1]).astype(jnp.int32)])
        slot = (jnp.cumsum(change) - 1) % 2
        next_first = jnp.sum(jnp.where(tile_e[:, None] == experts, pend // TM_MOE, 0), axis=1)
        tiles = jnp.arange(NT_MOE, dtype=jnp.int32)
        next_e = jnp.sum(jnp.where(next_first[:, None] == tiles, tile_e, 0), axis=1)
        nxt = jnp.where(next_first < n_used[0], next_e, -1)
        h2t = h2.reshape(N_TOK, PACK_ROWS, LANES)
        if last:
            xs = _row_dispatch(h2t, jnp.where(is_ctx_tok, R_MOE + tok % TM_MOE, dest_t), R_MOE + TM_MOE)
            dest_t = jnp.where(is_ctx_tok, tok, dest_t)
        else:
            xs = _row_dispatch(h2t, dest_t, R_MOE)

        yb_rows = _moe(l, tile_e, n_used, slot.astype(jnp.int32), nxt.astype(jnp.int32),
                       xs.reshape(-1, LANES), exp_w_gu, exp_b_gu, exp_w_down, exp_b_down)
        gathered = _row_gather(yb_rows.reshape(R_MOE, PACK_ROWS, LANES),
                               dest_t.reshape(-1)).reshape(-1, LANES)
        xc = _combine(l, x1, route, mod, gathered, latent_only=last)
    return xc
```

```python
import functools
import math

import jax
import jax.numpy as jnp
from jax import lax
from jax.experimental import pallas as pl
from jax.experimental.pallas import tpu as pltpu
from jax.experimental.pallas import tpu_sc as plsc

D = 1024
NB = 8
L_LAT = 2048
L_CTX = 256
LT = L_CTX + L_LAT
DEPTH = 4
N_MOD = 6
EPS = 1e-6
N_HEADS = 8
HEAD_DIM = 64
ATTN_W = 512
WINDOW = 128
ATTN_SCALE = HEAD_DIM ** -0.5
LOG2E = math.log2(math.e)
ROPE_BASE = 10000.0
GRID_W = 64
SSM_W = 256
SSM_GROUP = 16
SSM_GROUPS = 16
SSM_STATE = 64
NS = SSM_GROUPS * SSM_STATE
POOL_W = 256
POOL_GROUP = 64
N_EXP = 32
TOP_K = 4
D_FF = 1024
SWIGLU_LIMIT = 7.0
SWIGLU_ALPHA = 1.702

LANES = 128
SUBLANES = 8
VMEM_LIMIT = 56 * 1024 * 1024

TM_TOK = 768
TQ = 256
KSPAN = TQ + 2 * WINDOW
T_SCAN = 128
N_CHUNK = LT // T_SCAN
N_CHUNK_CTX = L_CTX // T_SCAN
TM_MOE = 512
N_TOK = NB * LT
N_ASSIGN = N_TOK * TOP_K
NT_MOE = N_ASSIGN // TM_MOE + N_EXP
R_MOE = NT_MOE * TM_MOE
FF_CHUNK = 512
OUT_BLOCK = 256

SC_CORES = 2
SC_SUBCORES = 16
SC_WORKERS = SC_CORES * SC_SUBCORES
SC_CHUNK = 32
PACK_ROWS = D // 2 // LANES

C_Q = 0
C_K = 512
C_V = 768
C_U = 1024
C_P = 1280
NW_IN = 1536

BF = jnp.bfloat16
F32 = jnp.float32


def _split_bf16(a):
    hi = a.astype(BF)
    lo = (a - hi.astype(F32)).astype(BF)
    return hi, lo


def _dot(a, b):
    return jnp.dot(a, b, preferred_element_type=F32)


def _dot3(a, b):
    ah, al = _split_bf16(a)
    bh, bl = _split_bf16(b)
    return _dot(ah, bh) + (_dot(ah, bl) + _dot(al, bh))


def _mod_kernel(c_ref, w_ref, b_ref, o_ref):
    c = c_ref[...]
    s = c * jax.nn.sigmoid(c)
    o_ref[0] = _dot3(s, w_ref[0]) + b_ref[0]


def _adaln(cc, w_mod, b_mod):
    tn = 1536
    return pl.pallas_call(
        _mod_kernel,
        out_shape=jax.ShapeDtypeStruct((DEPTH, 16, N_MOD * D), F32),
        grid=(DEPTH, N_MOD * D // tn),
        in_specs=[
            pl.BlockSpec((16, D), lambda l, j: (0, 0)),
            pl.BlockSpec((1, D, tn), lambda l, j: (l, 0, j)),
            pl.BlockSpec((1, 1, tn), lambda l, j: (l, 0, j)),
        ],
        out_specs=pl.BlockSpec((1, 16, tn), lambda l, j: (l, 0, j)),
        compiler_params=pltpu.CompilerParams(
            dimension_semantics=("arbitrary", "arbitrary"), vmem_limit_bytes=VMEM_LIMIT),
        name="adaln_mod",
    )(cc, w_mod, b_mod.reshape(DEPTH, 1, N_MOD * D))


def _pack_rows(ref, val, rows):
    def bits(v):
        return lax.bitcast_convert_type(v.astype(BF).astype(F32), jnp.uint32)

    words = bits(val[:, :D // 2]) | (bits(val[:, D // 2:]) >> 16)
    for j in range(PACK_ROWS):
        ref[pl.ds(j, rows, stride=PACK_ROWS), :] = words[:, j * LANES:(j + 1) * LANES]


def _unpack_rows(ref, rows):
    words = jnp.concatenate([ref[pl.ds(j, rows, stride=PACK_ROWS), :] for j in range(PACK_ROWS)], axis=1)
    return jnp.concatenate([lax.bitcast_convert_type(words & jnp.uint32(0xFFFF0000), F32),
                            lax.bitcast_convert_type(words << 16, F32)], axis=1)


def _layer_spec(l, shape, col_block=0):
    shape = tuple(shape)
    return pl.BlockSpec((None,) + shape, lambda *_: (l,) + (0,) * (len(shape) - 1) + (col_block,))


def _with_mod(fn, mod_ref, b, i, idxs, *arrays):
    lat = [mod_ref[pl.ds(b, 1), k * D:(k + 1) * D] for k in idxs]
    top = [jnp.where(i == 0, mod_ref[NB:NB + 1, k * D:(k + 1) * D], v) for k, v in zip(idxs, lat)]
    return jnp.concatenate([fn(*[a[:L_CTX] for a in arrays], *top),
                            fn(*[a[L_CTX:] for a in arrays], *lat)], axis=0)


def _rmsnorm(x, w):
    ms = jnp.mean(x * x, axis=-1, keepdims=True)
    return x * lax.rsqrt(ms + EPS) * w


def _headnorm_pair(t, w2, lane):
    sq = t * t
    first = lane < HEAD_DIM
    a = jnp.sum(jnp.where(first, sq, 0.0), axis=-1, keepdims=True)
    b = jnp.sum(jnp.where(first, 0.0, sq), axis=-1, keepdims=True)
    ms = jnp.where(first, a, b) * (1.0 / HEAD_DIM)
    return t * lax.rsqrt(ms + EPS) * w2


def _rope_pair(t, cos, sin_signed, lane):
    lower = (lane % 32) < 16
    partner = jnp.where(lower, pltpu.roll(t, LANES - 16, 1), pltpu.roll(t, 16, 1))
    return t * cos + partner * sin_signed


def _inproj_kernel(x_ref, mod_ref, n1_ref, w_ref, qw_ref, kw_ref, cos_ref, sin_ref,
                   q_ref, k_ref, v_ref, u_ref, p_ref):
    b = pl.program_id(0)
    i = pl.program_id(1)
    h = _rmsnorm(x_ref[0], n1_ref[...])
    h = _with_mod(lambda hr, shift, scale: (hr * (1.0 + scale) + shift).astype(BF), mod_ref, b, i, (0, 1), h)
    px = _dot(h, w_ref[...])
    lane = lax.broadcasted_iota(jnp.int32, (TM_TOK, LANES), 1)
    cos = cos_ref[...]
    sin = sin_ref[...]
    for j in range(ATTN_W // LANES):
        t = px[:, C_Q + j * LANES:C_Q + (j + 1) * LANES]
        t = _rope_pair(_headnorm_pair(t, qw_ref[...], lane), cos, sin, lane) * (ATTN_SCALE * LOG2E)
        q_ref[0, :, j * LANES:(j + 1) * LANES] = t.astype(BF)
    for g in range(2):
        t = px[:, C_K + g * LANES:C_K + (g + 1) * LANES]
        t = _rope_pair(_headnorm_pair(t, kw_ref[...], lane), cos, sin, lane)
        k_ref[0, :, g * LANES:(g + 1) * LANES] = t.astype(BF)
        t = px[:, C_V + g * LANES:C_V + (g + 1) * LANES]
        v_ref[0, :, g * LANES:(g + 1) * LANES] = jnp.where(lane < HEAD_DIM, t, 1.0).astype(BF)
    u_ref[0] = px[:, C_U:C_P]
    p_ref[0] = px[:, C_P:NW_IN]


def _inproj(l, xc, mod, n1, w_ext, qw2, kw2, cos2, sin2):
    tok = lambda w: pl.BlockSpec((1, TM_TOK, w), lambda b, i: (b, i, 0))
    full = lambda s: _layer_spec(l, s)
    return pl.pallas_call(
        _inproj_kernel,
        out_shape=(
            jax.ShapeDtypeStruct((NB, LT, ATTN_W), BF),
            jax.ShapeDtypeStruct((NB, LT, 256), BF),
            jax.ShapeDtypeStruct((NB, LT, 256), BF),
            jax.ShapeDtypeStruct((NB, LT, SSM_W), F32),
            jax.ShapeDtypeStruct((NB, LT, POOL_W), F32),
        ),
        grid=(NB, LT // TM_TOK),
        in_specs=[
            tok(D), full((16, N_MOD * D)), full((1, D)), full((D, NW_IN)),
            full((1, LANES)), full((1, LANES)),
            pl.BlockSpec((TM_TOK, LANES), lambda b, i: (i, 0)),
            pl.BlockSpec((TM_TOK, LANES), lambda b, i: (i, 0)),
        ],
        out_specs=(tok(ATTN_W), tok(256), tok(256), tok(SSM_W), tok(POOL_W)),
        compiler_params=pltpu.CompilerParams(
            dimension_semantics=("arbitrary", "arbitrary"), vmem_limit_bytes=VMEM_LIMIT),
        name="inproj",
    )(xc, mod, n1, w_ext, qw2, kw2, cos2, sin2)


def _attn_kernel(l, sink_ref, q_ref, k_ref, v_ref, onw_ref, o_ref, bias_ref):
    j = pl.program_id(1)
    start = pl.multiple_of(jnp.clip(j * TQ - WINDOW, LANES, LT - KSPAN), LANES)
    lane = lax.broadcasted_iota(jnp.int32, (TQ, LANES), 1)
    first = lane < HEAD_DIM
    row2 = lax.broadcasted_iota(jnp.int32, (2 * TQ, 1), 0)
    qpos = j * TQ + jnp.where(row2 < TQ, row2, row2 - TQ)
    kpos = start + lax.broadcasted_iota(jnp.int32, (1, KSPAN), 1)
    n_pairs = N_HEADS // 2

    def kv(ref, hp, rows):
        g = hp // 2
        return ref[0, rows, g * LANES:(g + 1) * LANES]

    def attend(local):
        def scores(hp):
            q2 = q_ref[0, :, hp * LANES:(hp + 1) * LANES]
            zero = jnp.zeros_like(q2)
            qs = jnp.concatenate([jnp.where(first, q2, zero), jnp.where(first, zero, q2)], axis=0)
            dn = (((1,), (1,)), ((), ()))
            s_ctx = lax.dot_general(qs, kv(k_ref, hp, slice(0, L_CTX)), dn, preferred_element_type=F32)
            if not local:
                return None, s_ctx
            return lax.dot_general(qs, kv(k_ref, hp, pl.ds(start, KSPAN)), dn, preferred_element_type=F32), s_ctx

        nxt = scores(0)
        outs = []
        for hp in range(n_pairs):
            s_loc, s_ctx = nxt
            if hp + 1 < n_pairs:
                nxt = scores(hp + 1)
            sink = jnp.where(row2 < TQ, sink_ref[l * N_HEADS + 2 * hp],
                             sink_ref[l * N_HEADS + 2 * hp + 1]) * LOG2E
            m = jnp.maximum(jnp.max(s_ctx, axis=-1, keepdims=True), sink)
            if local:
                s_loc = s_loc + bias_ref[...]
                m = jnp.maximum(m, jnp.max(s_loc, axis=-1, keepdims=True))
            o = _dot(jnp.exp2(s_ctx - m).astype(BF), kv(v_ref, hp, slice(0, L_CTX)))
            if local:
                o = o + _dot(jnp.exp2(s_loc - m).astype(BF), kv(v_ref, hp, pl.ds(start, KSPAN)))
            den = pltpu.roll(o + jnp.exp2(sink - m), HEAD_DIM, 1)
            o = o / den
            outs.append(jnp.where(first, o[0:TQ], pltpu.roll(o[TQ:2 * TQ], HEAD_DIM, 1)))
        o_ref[0] = _rmsnorm(jnp.concatenate(outs, axis=1), onw_ref[...]).astype(BF)

    @pl.when(j == 0)
    def _():
        attend(False)

    @pl.when(j > 0)
    def _():
        valid = (jnp.abs(qpos - kpos) <= WINDOW) & (kpos >= L_CTX)
        bias_ref[...] = jnp.where(valid, 0.0, -1e30)
        attend(True)


def _attention(l, sink, q, kd, vd, onw):
    return pl.pallas_call(
        functools.partial(_attn_kernel, l),
        out_shape=jax.ShapeDtypeStruct((NB, LT, ATTN_W), BF),
        grid_spec=pltpu.PrefetchScalarGridSpec(
            num_scalar_prefetch=1,
            grid=(NB, LT // TQ),
            in_specs=[
                pl.BlockSpec((1, TQ, ATTN_W), lambda b, j, s: (b, j, 0)),
                pl.BlockSpec((1, LT, 256), lambda b, j, s: (b, 0, 0)),
                pl.BlockSpec((1, LT, 256), lambda b, j, s: (b, 0, 0)),
                _layer_spec(l, (1, ATTN_W)),
            ],
            out_specs=pl.BlockSpec((1, TQ, ATTN_W), lambda b, j, s: (b, j, 0)),
            scratch_shapes=[pltpu.VMEM((2 * TQ, KSPAN), F32)],
        ),
        compiler_params=pltpu.CompilerParams(
            dimension_semantics=("arbitrary", "arbitrary"), vmem_limit_bytes=VMEM_LIMIT),
        name="window_attn",
    )(sink, q, kd, vd, onw)


def _bwd_chunk(i):
    return jnp.where(i < N_CHUNK_CTX, N_CHUNK_CTX - 1 - i, N_CHUNK - 1 - (i - N_CHUNK_CTX))


def _scan_kernel(uf_ref, ub_ref, a_ref, bm_ref, cm_ref, yf_ref, yb_ref,
                 xf_ref, xb_ref, st_ref, *il_refs):
    i = pl.program_id(0)
    halves = SSM_W // LANES

    @pl.when(i == 0)
    def _():
        st_ref[...] = jnp.zeros_like(st_ref)

    def project(d, u_ref, xs_ref):
        il = il_refs[d * halves:(d + 1) * halves]
        for h in range(halves):
            for bb in range(NB):
                il[h][pl.ds(bb, T_SCAN, stride=NB), :] = u_ref[bb, :, h * LANES:(h + 1) * LANES]
        ui = jnp.concatenate([r[...] for r in il], axis=1).astype(BF)
        xs_ref[...] = _dot(ui, bm_ref[d])

    def scan(d, xs_ref, reverse):
        ar = jnp.broadcast_to(a_ref[d, 0:1, :], (NB, NS))
        ai = jnp.broadcast_to(a_ref[d, 1:2, :], (NB, NS))
        sr = st_ref[d, 0]
        si = st_ref[d, 1]
        for t in (range(T_SCAN - 1, -1, -1) if reverse else range(T_SCAN)):
            r = pl.ds(t * NB, NB)
            nr = ar * sr - ai * si + xs_ref[r, 0:NS]
            ni = ar * si + ai * sr + xs_ref[r, NS:2 * NS]
            sr, si = nr, ni
            xs_ref[r, 0:NS] = sr
            xs_ref[r, NS:2 * NS] = si
        st_ref[d, 0] = sr
        st_ref[d, 1] = si

    def readout(d, xs_ref, y_ref):
        y = _dot(xs_ref[...].astype(BF), cm_ref[d])
        il = il_refs[d * halves:(d + 1) * halves]
        for h in range(halves):
            il[h][...] = y[:, h * LANES:(h + 1) * LANES]
            for bb in range(NB):
                y_ref[bb, :, h * LANES:(h + 1) * LANES] = il[h][pl.ds(bb, T_SCAN, stride=NB), :]

    project(0, uf_ref, xf_ref)
    project(1, ub_ref, xb_ref)
    scan(0, xf_ref, False)
    readout(0, xf_ref, yf_ref)
    scan(1, xb_ref, True)
    readout(1, xb_ref, yb_ref)


def _s5_scan(l, u, a_tab, bm, cm):
    rows = NB * T_SCAN
    layer = lambda s: _layer_spec(l, s)
    chunk_f = pl.BlockSpec((NB, T_SCAN, SSM_W), lambda i: (0, i, 0))
    chunk_b = pl.BlockSpec((NB, T_SCAN, SSM_W), lambda i: (0, _bwd_chunk(i), 0))
    return pl.pallas_call(
        _scan_kernel,
        out_shape=(jax.ShapeDtypeStruct((NB, LT, SSM_W), F32),
                   jax.ShapeDtypeStruct((NB, LT, SSM_W), F32)),
        grid=(N_CHUNK,),
        in_specs=[chunk_f, chunk_b, layer((2, 2, NS)), layer((2, SSM_W, 2 * NS)), layer((2, 2 * NS, SSM_W))],
        out_specs=(chunk_f, chunk_b),
        scratch_shapes=[pltpu.VMEM((rows, 2 * NS), F32), pltpu.VMEM((rows, 2 * NS), F32),
                        pltpu.VMEM((2, 2, NB, NS), F32),
                        ] + [pltpu.VMEM((rows, LANES), F32)] * (2 * (SSM_W // LANES)),
        compiler_params=pltpu.CompilerParams(
            dimension_semantics=("arbitrary",), vmem_limit_bytes=VMEM_LIMIT),
        name="s5_scan",
    )(u, u, a_tab, bm, cm)


def _pool_segment(ps, length):
    n = length + 2 * SUBLANES
    z = jnp.zeros((SUBLANES, POOL_W), F32)
    pe = jnp.concatenate([z, ps, z], axis=0)
    a1 = pe + pltpu.roll(pe, 1, 0)
    a2 = a1 + pltpu.roll(a1, 2, 0)
    a3 = a2 + pltpu.roll(a2, 4, 0)
    a4 = a3 + pltpu.roll(a3, 8, 0)
    lane = lax.broadcasted_iota(jnp.int32, (1, POOL_W), 1)
    half = jnp.where(lane < 64, 1, jnp.where(lane < 128, 2, jnp.where(lane < 192, 4, 8)))
    s = jnp.where(lane < 64, a1,
                  jnp.where(lane < 128, pltpu.roll(a2, n - 1, 0),
                            jnp.where(lane < 192, pltpu.roll(a3, n - 3, 0), pltpu.roll(a4, n - 7, 0))))
    s = s[SUBLANES:SUBLANES + length]
    t = lax.broadcasted_iota(jnp.int32, (length, 1), 0)
    cnt = jnp.minimum(t + half, length) - jnp.maximum(t - half, 0)
    return s / cnt.astype(F32) - ps


def _pool_kernel(p_ref, w_ref, sc_ref, o_ref):
    for lo, length in ((0, L_CTX), (L_CTX, L_LAT)):
        dlt = _pool_segment(p_ref[0, lo:lo + length, :], length)
        y = _dot(dlt.astype(BF), w_ref[...]) * sc_ref[...]
        o_ref[0, lo:lo + length, :] = y.astype(BF)


def _pool(l, p, w_bd, scale):
    return pl.pallas_call(
        _pool_kernel,
        out_shape=jax.ShapeDtypeStruct((NB, LT, POOL_W), BF),
        grid=(NB,),
        in_specs=[pl.BlockSpec((1, LT, POOL_W), lambda b: (b, 0, 0)),
                  _layer_spec(l, (POOL_W, POOL_W)),
                  _layer_spec(l, (1, POOL_W))],
        out_specs=pl.BlockSpec((1, LT, POOL_W), lambda b: (b, 0, 0)),
        compiler_params=pltpu.CompilerParams(
            dimension_semantics=("arbitrary",), vmem_limit_bytes=VMEM_LIMIT),
        name="pool",
    )(p, w_bd, scale)


def _mix_kernel(route_ctx, x_ref, at_ref, yf_ref, yb_ref, u_ref, pl_ref, mod_ref, d_ref, gw_ref, gb_ref, onw_ref,
                wo_ref, n2_ref, rw_ref, rb_ref, tri_ref,
                x1_ref, h2_ref, rt_ref, cnt_ref, run_ref):
    b = pl.program_id(0)
    i = pl.program_id(1)

    @pl.when((b == 0) & (i == 0))
    def _():
        run_ref[...] = jnp.zeros_like(run_ref)

    y = yf_ref[0] + yb_ref[0] + d_ref[...] * u_ref[0]
    g = jax.nn.gelu(y, approximate=True)
    s = g * jax.nn.sigmoid(_dot(g.astype(BF), gw_ref[...]) + gb_ref[...])
    s = _rmsnorm(s, onw_ref[...]).astype(BF)
    mix = (_dot(at_ref[0], wo_ref[0:ATTN_W, :])
           + _dot(s, wo_ref[ATTN_W:ATTN_W + SSM_W, :])
           + _dot(pl_ref[0], wo_ref[ATTN_W + SSM_W:D, :]))
    x1 = _with_mod(lambda xr, mr, gate: xr + gate * mr, mod_ref, b, i, (2,), x_ref[0], mix)
    x1_ref[0] = x1
    h2 = _rmsnorm(x1, n2_ref[...])
    h2 = _with_mod(lambda hr, shift, scale: hr * (1.0 + scale) + shift, mod_ref, b, i, (3, 4), h2)
    _pack_rows(h2_ref, h2, TM_TOK)
    logits = (_dot3(h2, rw_ref[...]) + rb_ref[...]).T[0:N_EXP]
    eidx = lax.broadcasted_iota(jnp.int32, (N_EXP, TM_TOK), 0).astype(F32)
    vals, idxs, hits = [], [], []
    cur = logits
    for _ in range(TOP_K):
        m = jnp.max(cur, axis=0, keepdims=True)
        idx = jnp.min(jnp.where(cur == m, eidx, float(N_EXP)), axis=0, keepdims=True)
        hit = eidx == idx
        vals.append(m)
        idxs.append(idx)
        hits.append(hit)
        cur = jnp.where(hit, -jnp.inf, cur)
    ex = [jnp.exp(v - vals[0]) for v in vals]
    den = ex[0] + ex[1] + ex[2] + ex[3]
    onehot = jnp.where(hits[0] | hits[1] | hits[2] | hits[3], 1.0, 0.0)
    if not route_ctx:
        tok = lax.broadcasted_iota(jnp.int32, (1, TM_TOK), 1)
        onehot = jnp.where((tok >= L_CTX) | (i > 0), onehot, 0.0)
    run = run_ref[:, 0:1]
    before = _dot(onehot.astype(BF), tri_ref[...]) + run
    row = lax.broadcasted_iota(jnp.int32, (4 * TOP_K, TM_TOK), 0)
    route = jnp.zeros((4 * TOP_K, TM_TOK), F32)
    for k in range(TOP_K):
        rank = jnp.sum(jnp.where(hits[k], before, 0.0), axis=0, keepdims=True)
        route = jnp.where(row == k, idxs[k], route)
        route = jnp.where(row == TOP_K + k, ex[k] / den, route)
        route = jnp.where(row == 2 * TOP_K + k, rank, route)
    rt_ref[...] = route
    run_ref[...] = jnp.broadcast_to(run + jnp.sum(onehot, axis=1, keepdims=True), (N_EXP, LANES))
    cnt_ref[...] = run_ref[...]


def _mix(l, route_ctx, xc, attn, yf, yb, u, pool, mod, ssm_d, glu_w, glu_b, onw, w_out, n2, rw, rb, tri):
    tok = lambda w: pl.BlockSpec((1, TM_TOK, w), lambda b, i: (b, i, 0))
    full = lambda s: pl.BlockSpec(s, lambda b, i: (0,) * len(s))
    layer = lambda s, col=0: _layer_spec(l, s, col)
    return pl.pallas_call(
        functools.partial(_mix_kernel, route_ctx),
        out_shape=(jax.ShapeDtypeStruct((NB, LT, D), F32),
                   jax.ShapeDtypeStruct((N_TOK * PACK_ROWS, LANES), jnp.uint32),
                   jax.ShapeDtypeStruct((4 * TOP_K, N_TOK), F32),
                   jax.ShapeDtypeStruct((N_EXP, LANES), F32)),
        grid=(NB, LT // TM_TOK),
        in_specs=[tok(D), tok(ATTN_W), tok(SSM_W), tok(SSM_W), tok(SSM_W), tok(POOL_W),
                  layer((16, N_MOD * D)), layer((1, SSM_W)), layer((SSM_W, SSM_W)), layer((1, SSM_W)),
                  layer((1, SSM_W), ATTN_W // SSM_W), layer((D, D)), layer((1, D)), layer((D, LANES)),
                  layer((1, LANES)), full((TM_TOK, TM_TOK))],
        out_specs=(tok(D),
                   pl.BlockSpec((TM_TOK * PACK_ROWS, LANES), lambda b, i: (b * (LT // TM_TOK) + i, 0)),
                   pl.BlockSpec((4 * TOP_K, TM_TOK), lambda b, i: (0, b * (LT // TM_TOK) + i)),
                   full((N_EXP, LANES))),
        scratch_shapes=[pltpu.VMEM((N_EXP, LANES), F32)],
        compiler_params=pltpu.CompilerParams(
            dimension_semantics=("arbitrary", "arbitrary"), vmem_limit_bytes=VMEM_LIMIT),
        name="mix_router",
    )(xc, attn, yf, yb, u, pool, mod, ssm_d, glu_w, glu_b, onw, w_out, n2, rw, rb, tri)


def _moe_kernel(l, te_ref, nu_ref, slot_ref, nxt_ref, x_ref, wgu_hbm, bgu_ref, wd_hbm, bd_ref, o_ref,
                wgu_f, wd_f, wgu_s, wd_s, act_s, sem):
    i = pl.program_id(0)

    @pl.when(i >= nu_ref[0])
    def _():
        o_ref[...] = jnp.zeros_like(o_ref)

    def fetch(e, slot):
        return (pltpu.make_async_copy(wgu_hbm.at[l, e], wgu_f.at[slot], sem.at[0, slot]),
                pltpu.make_async_copy(wd_hbm.at[l, e], wd_f.at[slot], sem.at[1, slot]))

    @pl.when(i < nu_ref[0])
    def _():
        e = te_ref[i]
        prev = te_ref[jnp.maximum(i - 1, 0)]
        slot = slot_ref[i]

        @pl.when(i == 0)
        def _():
            for cp in fetch(e, slot):
                cp.start()

        @pl.when((i == 0) | (e != prev))
        def _():
            for cp in fetch(e, slot):
                cp.wait()
            nxt = nxt_ref[i]

            @pl.when(nxt >= 0)
            def _():
                for cp in fetch(nxt, 1 - slot):
                    cp.start()

            def cast_gu(r, c):
                rs = pl.ds(pl.multiple_of(r * 128, 128), 128)
                wgu_s[rs, :] = wgu_f[slot, rs, :].astype(BF)
                return c

            def cast_d(r, c):
                rs = pl.ds(pl.multiple_of(r * 128, 128), 128)
                wd_s[rs, :] = wd_f[slot, rs, :].astype(BF)
                return c

            lax.fori_loop(0, D // 128, cast_gu, 0)
            lax.fori_loop(0, D_FF // 128, cast_d, 0)

        x = _unpack_rows(x_ref, TM_MOE).astype(BF)
        for c in range(D_FF // FF_CHUNK):
            lo = c * FF_CHUNK
            gate = _dot(x, wgu_s[:, lo:lo + FF_CHUNK]) + bgu_ref[:, lo:lo + FF_CHUNK]
            up = (_dot(x, wgu_s[:, D_FF + lo:D_FF + lo + FF_CHUNK])
                  + bgu_ref[:, D_FF + lo:D_FF + lo + FF_CHUNK])
            gate = jnp.minimum(gate, SWIGLU_LIMIT)
            up = jnp.clip(up, -SWIGLU_LIMIT, SWIGLU_LIMIT)
            act = (up + 1.0) * (gate * jax.nn.sigmoid(SWIGLU_ALPHA * gate))
            act_s[:, lo:lo + FF_CHUNK] = act.astype(BF)
        def bf16_bits(v):
            return lax.bitcast_convert_type(v.astype(BF).astype(F32), jnp.uint32)

        for n in range(D // 2 // OUT_BLOCK):
            lo = n * OUT_BLOCK
            y_hi = _dot(act_s[...], wd_s[:, lo:lo + OUT_BLOCK]) + bd_ref[:, lo:lo + OUT_BLOCK]
            y_lo = (_dot(act_s[...], wd_s[:, D // 2 + lo:D // 2 + lo + OUT_BLOCK])
                    + bd_ref[:, D // 2 + lo:D // 2 + lo + OUT_BLOCK])
            words = bf16_bits(y_hi) | (bf16_bits(y_lo) >> 16)
            for j in range(OUT_BLOCK // LANES):
                o_ref[pl.ds(lo // LANES + j, TM_MOE, stride=PACK_ROWS), :] = words[:, j * LANES:(j + 1) * LANES]


def _moe(l, tile_e, n_used, slot, nxt, xs, w_gu, b_gu, w_down, b_down):
    def tile(i, te, nu, *_):
        return (jnp.minimum(i, nu[0] - 1), 0)

    def bias(i, te, nu, *_):
        return (l, te[jnp.minimum(i, nu[0] - 1)], 0, 0)

    return pl.pallas_call(
        functools.partial(_moe_kernel, l),
        out_shape=jax.ShapeDtypeStruct((R_MOE * PACK_ROWS, LANES), jnp.uint32),
        grid_spec=pltpu.PrefetchScalarGridSpec(
            num_scalar_prefetch=4,
            grid=(NT_MOE,),
            in_specs=[
                pl.BlockSpec((TM_MOE * PACK_ROWS, LANES), tile),
                pl.BlockSpec(memory_space=pl.ANY),
                pl.BlockSpec((None, None, 1, 2 * D_FF), bias),
                pl.BlockSpec(memory_space=pl.ANY),
                pl.BlockSpec((None, None, 1, D), bias),
            ],
            out_specs=pl.BlockSpec((TM_MOE * PACK_ROWS, LANES), lambda i, *_: (i, 0)),
            scratch_shapes=[pltpu.VMEM((2, D, 2 * D_FF), F32), pltpu.VMEM((2, D_FF, D), F32),
                            pltpu.VMEM((D, 2 * D_FF), BF), pltpu.VMEM((D_FF, D), BF),
                            pltpu.VMEM((TM_MOE, D_FF), BF), pltpu.SemaphoreType.DMA((2, 2))],
        ),
        compiler_params=pltpu.CompilerParams(
            dimension_semantics=("arbitrary",), vmem_limit_bytes=VMEM_LIMIT),
        name="moe_experts",
    )(tile_e, n_used, slot, nxt, xs, w_gu, b_gu.reshape(DEPTH, N_EXP, 1, 2 * D_FF), w_down,
      b_down.reshape(DEPTH, N_EXP, 1, D))


def _row_gather(table, idx):
    n_rows = idx.shape[0]
    per_worker = n_rows // SC_WORKERS
    n_chunks = per_worker // SC_CHUNK
    assert per_worker * SC_WORKERS == n_rows and n_chunks * SC_CHUNK == per_worker and n_chunks % 2 == 0
    mesh = plsc.VectorSubcoreMesh(core_axis_name="c", subcore_axis_name="s")
    row_tile = (SC_CHUNK,) + table.shape[1:]

    @functools.partial(
        pl.kernel, mesh=mesh,
        out_type=jax.ShapeDtypeStruct((n_rows,) + table.shape[1:], table.dtype),
        scratch_types=[pltpu.VMEM((n_chunks, SC_CHUNK), jnp.int32),
                       pltpu.VMEM(row_tile, table.dtype), pltpu.VMEM(row_tile, table.dtype),
                       pltpu.SemaphoreType.DMA, pltpu.SemaphoreType.DMA],
        name="sc_row_gather",
    )
    def gather(table_hbm, idx_hbm, out_hbm, idx_v, buf0, buf1, sem0, sem1):
        wid = lax.axis_index("s") * SC_CORES + lax.axis_index("c")
        pltpu.sync_copy(idx_hbm.at[pl.ds(wid * n_chunks, n_chunks)], idx_v)
        base = wid * per_worker

        def fetch(chunk, buf, sem):
            return pltpu.make_async_copy(table_hbm.at[idx_v.at[chunk]], buf, sem)

        def emit(chunk, buf):
            pltpu.sync_copy(buf, out_hbm.at[pl.ds(base + chunk * SC_CHUNK, SC_CHUNK)])

        fetch(0, buf0, sem0).start()

        @pl.loop(0, n_chunks, step=2)
        def _(c):
            fetch(c + 1, buf1, sem1).start()
            fetch(c, buf0, sem0).wait()
            emit(c, buf0)

            @pl.when(c + 2 < n_chunks)
            def _():
                fetch(c + 2, buf0, sem0).start()

            fetch(c + 1, buf1, sem1).wait()
            emit(c + 1, buf1)

    return gather(table, idx.reshape(n_rows // SC_CHUNK, SC_CHUNK))


def _row_dispatch(rows, dest_t, n_out):
    per_worker = N_TOK // SC_WORKERS
    n_chunks = per_worker // SC_CHUNK
    assert per_worker * SC_WORKERS == N_TOK and n_chunks * SC_CHUNK == per_worker and n_chunks % 2 == 0
    idx_rows = TOP_K * n_chunks
    assert idx_rows % SUBLANES == 0
    mesh = plsc.VectorSubcoreMesh(core_axis_name="c", subcore_axis_name="s")
    row_tile = (SC_CHUNK,) + rows.shape[1:]

    @functools.partial(
        pl.kernel, mesh=mesh,
        out_type=jax.ShapeDtypeStruct((n_out,) + rows.shape[1:], rows.dtype),
        scratch_types=[pltpu.VMEM((idx_rows, SC_CHUNK), jnp.int32),
                       pltpu.VMEM(row_tile, rows.dtype), pltpu.VMEM(row_tile, rows.dtype),
                       pltpu.SemaphoreType.DMA, pltpu.SemaphoreType.DMA],
        name="sc_row_dispatch",
    )
    def dispatch(rows_hbm, idx_hbm, out_hbm, idx_v, buf0, buf1, sem0, sem1):
        wid = lax.axis_index("s") * SC_CORES + lax.axis_index("c")
        pltpu.sync_copy(idx_hbm.at[pl.ds(wid * idx_rows, idx_rows)], idx_v)
        base = wid * per_worker

        def fetch(chunk, buf, sem):
            return pltpu.make_async_copy(rows_hbm.at[pl.ds(base + chunk * SC_CHUNK, SC_CHUNK)], buf, sem)

        def emit(chunk, buf):
            for k in range(TOP_K):
                pltpu.sync_copy(buf, out_hbm.at[idx_v.at[k * n_chunks + chunk]])

        fetch(0, buf0, sem0).start()

        @pl.loop(0, n_chunks, step=2)
        def _(c):
            fetch(c + 1, buf1, sem1).start()
            fetch(c, buf0, sem0).wait()
            emit(c, buf0)

            @pl.when(c + 2 < n_chunks)
            def _():
                fetch(c + 2, buf0, sem0).start()

            fetch(c + 1, buf1, sem1).wait()
            emit(c + 1, buf1)

    idx = dest_t.reshape(TOP_K, SC_WORKERS, n_chunks, SC_CHUNK).transpose(1, 0, 2, 3)
    return dispatch(rows, idx.reshape(SC_WORKERS * idx_rows, SC_CHUNK))


def _combine_kernel(first_tile, x1_ref, rt_ref, mod_ref, g0_ref, g1_ref, g2_ref, g3_ref, o_ref):
    t = pl.program_id(0)
    per_b = LT // TQ - first_tile
    b = t // per_b
    is_ctx = (t % per_b + first_tile) == 0
    route = rt_ref[...].T
    y = jnp.zeros((TQ, D), F32)
    for k, g_ref in enumerate((g0_ref, g1_ref, g2_ref, g3_ref)):
        y = y + route[:, TOP_K + k:TOP_K + k + 1] * _unpack_rows(g_ref, TQ)
    gate = jnp.where(is_ctx, mod_ref[NB:NB + 1, 5 * D:6 * D], mod_ref[pl.ds(b, 1), 5 * D:6 * D])
    o_ref[0] = x1_ref[0] + gate * y


def _combine(l, x1, route, mod, gathered, latent_only):
    tiles_b = LT // TQ
    first_tile = L_CTX // TQ if latent_only else 0
    per_b = tiles_b - first_tile
    tile_of = lambda t: (t // per_b, t % per_b + first_tile)
    tok = lambda w: pl.BlockSpec((1, TQ, w), lambda t: tile_of(t) + (0,))
    slot = lambda k: pl.BlockSpec(
        (TQ * PACK_ROWS, LANES),
        lambda t: (k * NB * tiles_b + tile_of(t)[0] * tiles_b + tile_of(t)[1], 0))
    return pl.pallas_call(
        functools.partial(_combine_kernel, first_tile),
        out_shape=jax.ShapeDtypeStruct((NB, per_b * TQ, D), F32),
        grid=(NB * per_b,),
        in_specs=[tok(D),
                  pl.BlockSpec((4 * TOP_K, TQ), lambda t: (0, tile_of(t)[0] * tiles_b + tile_of(t)[1])),
                  _layer_spec(l, (16, N_MOD * D)),
                  slot(0), slot(1), slot(2), slot(3)],
        out_specs=pl.BlockSpec((1, TQ, D), lambda t: (t // per_b, t % per_b, 0)),
        compiler_params=pltpu.CompilerParams(
            dimension_semantics=("arbitrary",), vmem_limit_bytes=VMEM_LIMIT),
        name="moe_combine",
    )(x1, route, mod, gathered, gathered, gathered, gathered)


def _rope_tables():
    pos = jnp.arange(L_LAT)
    row = (pos // GRID_W).astype(F32)
    col = (pos % GRID_W).astype(F32)
    inv = ROPE_BASE ** (-jnp.arange(0, 32, 2, dtype=F32) / 32)
    ang_r = row[:, None] * inv
    ang_c = col[:, None] * inv
    ang = jnp.concatenate([ang_r, ang_r, ang_c, ang_c], axis=-1)
    cos = jnp.concatenate([jnp.ones((L_CTX, HEAD_DIM), F32), jnp.cos(ang)], axis=0)
    sin = jnp.concatenate([jnp.zeros((L_CTX, HEAD_DIM), F32), jnp.sin(ang)], axis=0)
    sign = jnp.where((jnp.arange(HEAD_DIM) % 32) < 16, -1.0, 1.0).astype(F32)
    return jnp.tile(cos, (1, 2)), jnp.tile(sin * sign, (1, 2))


def _s5_tables(a_re, a_im, log_dt, b_re, b_im, c_re, c_im):
    dt = jnp.exp(log_dt)[..., None]
    mag = jnp.exp(a_re * dt)
    ar = mag * jnp.cos(a_im * dt)
    ai = mag * jnp.sin(a_im * dt)
    den = a_re * a_re + a_im * a_im
    qr = ((ar - 1) * a_re + ai * a_im) / den
    qi = (ai * a_re - (ar - 1) * a_im) / den
    bbr = qr[..., None] * b_re - qi[..., None] * b_im
    bbi = qr[..., None] * b_im + qi[..., None] * b_re
    eye = jnp.eye(SSM_GROUPS, dtype=F32)
    to_in = lambda m: jnp.einsum('ldgnp,gh->ldgphn', m, eye).reshape(DEPTH, 2, SSM_W, NS)
    bm = jnp.concatenate([to_in(bbr), to_in(bbi)], axis=-1).astype(BF)
    to_out = lambda m: jnp.einsum('ldgpn,gh->ldgnhp', m, eye).reshape(DEPTH, 2, NS, SSM_W)
    cm = jnp.concatenate([to_out(c_re), -to_out(c_im)], axis=2).astype(BF)
    a_tab = jnp.stack([ar.reshape(DEPTH, 2, NS), ai.reshape(DEPTH, 2, NS)], axis=2)
    return a_tab, bm, cm


def kernel(x, c, ctx, c_ctx, w_mod, b_mod, norm1_w, norm2_w, w_in, q_norm_w, k_norm_w, attn_sink,
           ssm_a_re, ssm_a_im, ssm_log_dt, ssm_b_re, ssm_b_im, ssm_c_re, ssm_c_im, ssm_d, glu_w, glu_b,
           pool_w, pool_scale, out_norm_w, w_out, router_w, router_b, exp_w_gu, exp_b_gu, exp_w_down,
           exp_b_down):
    xc = jnp.concatenate([ctx, x], axis=1)
    cc = jnp.concatenate([c, c_ctx[None, :], jnp.zeros((16 - NB - 1, D), F32)], axis=0)
    mod = _adaln(cc, w_mod, b_mod)
    cos2, sin2 = _rope_tables()
    tri = (jnp.arange(TM_TOK)[:, None] < jnp.arange(TM_TOK)[None, :]).astype(BF)

    row = lambda a: a.reshape(DEPTH, 1, -1)
    dup = lambda m: jnp.concatenate([m[..., :64], m[..., :64], m[..., 64:], m[..., 64:]], axis=-1)
    w_ext = jnp.concatenate([w_in[..., :512], dup(w_in[..., 512:640]), dup(w_in[..., 640:768]),
                             w_in[..., 768:]], axis=-1).astype(BF)
    qw2 = row(jnp.tile(q_norm_w, (1, 2)))
    kw2 = row(jnp.tile(k_norm_w, (1, 2)))
    a_tab, bm, cm = _s5_tables(ssm_a_re, ssm_a_im, ssm_log_dt, ssm_b_re, ssm_b_im, ssm_c_re, ssm_c_im)
    pw_bd = jnp.einsum('lgcd,gh->lgchd', pool_w, jnp.eye(4, dtype=F32)).reshape(DEPTH, POOL_W, POOL_W).astype(BF)
    rw = jnp.concatenate([router_w, jnp.zeros((DEPTH, D, LANES - N_EXP), F32)], axis=-1)
    rb = row(jnp.concatenate([router_b, jnp.full((DEPTH, LANES - N_EXP), -1e30, F32)], axis=-1))
    glu_w_bf = glu_w.astype(BF)
    w_out_bf = w_out.astype(BF)
    n1, n2, onw = row(norm1_w), row(norm2_w), row(out_norm_w)
    ssm_d3, glu_b3, pool_sc3 = row(ssm_d), row(glu_b), row(pool_scale)
    sink = attn_sink.reshape(-1)

    tok = jnp.arange(N_TOK, dtype=jnp.int32)
    is_ctx_tok = (tok % LT) < L_CTX

    for l in range(DEPTH):
        last = l == DEPTH - 1
        q, kd, vd, u, p = _inproj(l, xc, mod, n1, w_ext, qw2, kw2, cos2, sin2)
        attn = _attention(l, sink, q, kd, vd, onw)
        yf, yb = _s5_scan(l, u, a_tab, bm, cm)
        pool = _pool(l, p, pw_bd, pool_sc3)
        x1, h2, route, cnt = _mix(l, not last, xc, attn, yf, yb, u, pool, mod, ssm_d3, glu_w_bf, glu_b3, onw,
                                  w_out_bf, n2, rw, rb, tri)

        top_i = route[0:TOP_K].astype(jnp.int32)
        rank = route[2 * TOP_K:3 * TOP_K].astype(jnp.int32)
        counts = cnt[:, 0].astype(jnp.int32)
        padded = (counts + TM_MOE - 1) // TM_MOE * TM_MOE
        pend = jnp.cumsum(padded)
        pstart = pend - padded
        experts = jnp.arange(N_EXP, dtype=jnp.int32)
        dest_t = jnp.sum(jnp.where(top_i[..., None] == experts, pstart, 0), axis=-1) + rank
        n_used = (pend[-1] // TM_MOE).astype(jnp.int32)[None]
        tile_lo = jnp.arange(NT_MOE, dtype=jnp.int32) * TM_MOE
        tile_e = jnp.minimum(jnp.sum(pend[None, :] <= tile_lo[:, None], axis=1), N_EXP - 1).astype(jnp.int32)
        change = jnp.concatenate([jnp.ones((1,), jnp.int32), (tile_e[1:] != tile_e[:-1]).astype(jnp.int32)])
        slot = (jnp.cumsum(change) - 1) % 2
        next_first = jnp.sum(jnp.where(tile_e[:, None] == experts, pend // TM_MOE, 0), axis=1)
        tiles = jnp.arange(NT_MOE, dtype=jnp.int32)
        next_e = jnp.sum(jnp.where(next_first[:, None] == tiles, tile_e, 0), axis=1)
        nxt = jnp.where(next_first < n_used[0], next_e, -1)
        h2t = h2.reshape(N_TOK, PACK_ROWS, LANES)
        if last:
            xs = _row_dispatch(h2t, jnp.where(is_ctx_tok, R_MOE + tok % TM_MOE, dest_t), R_MOE + TM_MOE)
            dest_t = jnp.where(is_ctx_tok, tok, dest_t)
        else:
            xs = _row_dispatch(h2t, dest_t, R_MOE)

        yb_rows = _moe(l, tile_e, n_used, slot.astype(jnp.int32), nxt.astype(jnp.int32),
                       xs.reshape(-1, LANES), exp_w_gu, exp_b_gu, exp_w_down, exp_b_down)
        gathered = _row_gather(yb_rows.reshape(R_MOE, PACK_ROWS, LANES),
                               dest_t.reshape(-1)).reshape(-1, LANES)
        xc = _combine(l, x1, route, mod, gathered, latent_only=last)
    return xc
```
